```python
import math
import jax, jax.numpy as jnp
from jax import lax
import numpy as np

D_MODEL = 1024
BATCH = 4
SEQ = 8192
DEPTH = 4

N_MIXERS = 2
EPS = 1e-6
NEG_INF = -1e30
FORCE_SCORE = 1e9

N_HEADS = 16
HEAD_DIM = 64
N_KV_GROUPS = 4
HEADS_PER_GROUP = N_HEADS // N_KV_GROUPS
ATTN_WIDTH = N_HEADS * HEAD_DIM
KV_WIDTH = N_KV_GROUPS * HEAD_DIM
CMP_BLOCK = 32
CMP_STRIDE = 16
CMP_HIDDEN = 256
SEL_BLOCK = 64
SEL_TOPK = 16
WINDOW = 512
Q_BLOCK = 64
NSA_IN = ATTN_WIDTH + 6 * KV_WIDTH + 3 * N_HEADS + ATTN_WIDTH

SSM_WIDTH = D_MODEL
SSM_GROUP = 16
SSM_GROUPS = SSM_WIDTH // SSM_GROUP
SSM_STATE = 64
SCAN_CHUNK = 128
DT_MIN = 1e-3
DT_MAX = 1e-1

kernel_name = "nsa_s5_interleaved_hybrid"


def _rmsnorm(x, g):
    xf = x.astype(jnp.float32)
    y = xf * lax.rsqrt(jnp.mean(xf * xf, axis=-1, keepdims=True) + EPS) * g.astype(jnp.float32)
    return y.astype(x.dtype)


def _masked_softmax(s, mask):
    s = jnp.where(mask, s, NEG_INF)
    m = jnp.max(s, axis=-1, keepdims=True)
    p = jnp.where(mask, jnp.exp(s - m), 0.0)
    return p / jnp.maximum(jnp.sum(p, axis=-1, keepdims=True), 1e-30)


def _alibi_slopes():
    h = jnp.arange(1, N_HEADS + 1, dtype=jnp.float32)
    return jnp.exp2(-8.0 * h / N_HEADS).reshape(N_KV_GROUPS, HEADS_PER_GROUP)


def _compress(k, cmp_idx, pe, w1, w2):
    bsz = k.shape[0]
    n_cmp = cmp_idx.shape[0]
    blk = k[:, cmp_idx] + pe[:, None, :]
    blk = jnp.transpose(blk, (0, 1, 3, 2, 4)).reshape(bsz, n_cmp, N_KV_GROUPS, CMP_BLOCK * HEAD_DIM)
    return jax.nn.silu(blk @ w1) @ w2


def _nsa_mixer(h, w_in, cmp_k_pe, cmp_k_w1, cmp_k_w2, cmp_v_pe, cmp_v_w1, cmp_v_w2, w_out):
    f32 = jnp.float32
    bsz, seq, _ = h.shape
    G, R = N_KV_GROUPS, HEADS_PER_GROUP
    n_cmp = (seq - CMP_BLOCK) // CMP_STRIDE + 1
    n_blocks = seq // SEL_BLOCK
    n_sel = min(SEL_TOPK, n_blocks)

    splits = [ATTN_WIDTH + i * KV_WIDTH for i in range(7)] + [ATTN_WIDTH + 6 * KV_WIDTH + 3 * N_HEADS]
    q, kc, vc, ks, vs, kw, vw, gates, z = jnp.split(h @ w_in, splits, axis=-1)
    q = q.reshape(bsz, seq, G, R, HEAD_DIM) * (HEAD_DIM ** -0.5)
    kc, vc, ks, vs, kw, vw = [a.reshape(bsz, seq, G, HEAD_DIM) for a in (kc, vc, ks, vs, kw, vw)]
    gates = jax.nn.sigmoid(gates.astype(f32)).reshape(bsz, seq, 3, G, R)
    slopes = _alibi_slopes()

    cmp_start = jnp.arange(n_cmp) * CMP_STRIDE
    cmp_idx = cmp_start[:, None] + jnp.arange(CMP_BLOCK)[None, :]
    k_cmp = _compress(kc, cmp_idx, cmp_k_pe, cmp_k_w1, cmp_k_w2)
    v_cmp = _compress(vc, cmp_idx, cmp_v_pe, cmp_v_w1, cmp_v_w2)
    cmp_end = cmp_start + CMP_BLOCK - 1
    cmp_center = cmp_start.astype(f32) + 0.5 * (CMP_BLOCK - 1)

    sel_start = jnp.arange(n_blocks) * SEL_BLOCK
    overlap = ((cmp_start[:, None] <= sel_start[None, :] + SEL_BLOCK - 1)
               & (cmp_end[:, None] >= sel_start[None, :])).astype(f32)

    ks_blk = jnp.transpose(ks.reshape(bsz, n_blocks, SEL_BLOCK, G, HEAD_DIM), (0, 3, 1, 2, 4))
    vs_blk = jnp.transpose(vs.reshape(bsz, n_blocks, SEL_BLOCK, G, HEAD_DIM), (0, 3, 1, 2, 4))
    gather = jax.vmap(jax.vmap(lambda kb, ix: kb[ix]))

    kw_pad = jnp.pad(kw, ((0, 0), (WINDOW, 0), (0, 0), (0, 0)))
    vw_pad = jnp.pad(vw, ((0, 0), (WINDOW, 0), (0, 0), (0, 0)))

    def block(qi):
        q0 = qi * Q_BLOCK
        t = q0 + jnp.arange(Q_BLOCK, dtype=jnp.int32)
        tf = t.astype(f32)
        qb = lax.dynamic_slice_in_dim(q, q0, Q_BLOCK, axis=1)
        gb = lax.dynamic_slice_in_dim(gates, q0, Q_BLOCK, axis=1)

        s_c = jnp.einsum('bqgrd,bngd->bgrqn', qb, k_cmp, preferred_element_type=f32)
        s_c = s_c - slopes[None, :, :, None, None] * (tf[:, None] - cmp_center[None, :])
        p_c = _masked_softmax(s_c, (cmp_end[None, :] <= t[:, None]))
        o_c = jnp.einsum('bgrqn,bngd->bqgrd', p_c, v_cmp)

        imp = jnp.einsum('bgrqn,nj->bgqj', p_c, overlap)
        j = jnp.arange(n_blocks)[None, :]
        jt = (t // SEL_BLOCK)[:, None]
        forced = (j == 0) | (j == jt) | (j == jt - 1)
        imp = jnp.where(forced, FORCE_SCORE, jnp.where(j > jt, -FORCE_SCORE, imp))
        _, sel_idx = lax.top_k(imp, n_sel)

        k_sel = gather(ks_blk, sel_idx)
        v_sel = gather(vs_blk, sel_idx).reshape(bsz, G, Q_BLOCK, n_sel * SEL_BLOCK, HEAD_DIM)
        pos_s = sel_idx[..., None] * SEL_BLOCK + jnp.arange(SEL_BLOCK)
        d_s = t[None, None, :, None, None] - pos_s
        s_s = jnp.einsum('bqgrd,bgqnkd->bgrqnk', qb, k_sel, preferred_element_type=f32)
        s_s = s_s - slopes[None, :, :, None, None, None] * d_s[:, :, None].astype(f32)
        s_s = s_s.reshape(bsz, G, R, Q_BLOCK, n_sel * SEL_BLOCK)
        m_s = (d_s >= 0)[:, :, None].reshape(bsz, G, 1, Q_BLOCK, n_sel * SEL_BLOCK)
        p_s = _masked_softmax(s_s, m_s)
        o_s = jnp.einsum('bgrqm,bgqmd->bqgrd', p_s, v_sel)

        kwb = lax.dynamic_slice_in_dim(kw_pad, q0, Q_BLOCK + WINDOW, axis=1)
        vwb = lax.dynamic_slice_in_dim(vw_pad, q0, Q_BLOCK + WINDOW, axis=1)
        pos_w = q0 - WINDOW + jnp.arange(Q_BLOCK + WINDOW, dtype=jnp.int32)
        d_w = t[:, None] - pos_w[None, :]
        m_w = (pos_w[None, :] >= 0) & (d_w >= 0) & (d_w < WINDOW)
        s_w = jnp.einsum('bqgrd,bkgd->bgrqk', qb, kwb, preferred_element_type=f32)
        s_w = s_w - slopes[None, :, :, None, None] * d_w.astype(f32)
        p_w = _masked_softmax(s_w, m_w)
        o_w = jnp.einsum('bgrqk,bkgd->bqgrd', p_w, vwb)

        return (gb[:, :, 0, :, :, None] * o_c + gb[:, :, 1, :, :, None] * o_s
                + gb[:, :, 2, :, :, None] * o_w)

    o = lax.map(block, jnp.arange(seq // Q_BLOCK, dtype=jnp.int32))
    o = jnp.moveaxis(o, 0, 1).reshape(bsz, seq, ATTN_WIDTH)
    o = (o * jax.nn.silu(z.astype(f32))).astype(h.dtype)
    return o @ w_out


def _complex_affine_combine(left, right):
    ar_i, ai_i, br_i, bi_i = left
    ar_j, ai_j, br_j, bi_j = right
    ar = ar_j * ar_i - ai_j * ai_i
    ai = ar_j * ai_i + ai_j * ar_i
    br = ar_j * br_i - ai_j * bi_i + br_j
    bi = ar_j * bi_i + ai_j * br_i + bi_j
    return ar, ai, br, bi


def _s5_mixer(h, w_in, log_dt, lambda_re, lambda_im, b_re, b_im, c_re, c_im, d_skip, w_glu, w_out):
    f32 = jnp.float32
    bsz, seq, _ = h.shape
    G, C, P = SSM_GROUPS, SSM_GROUP, SSM_STATE
    u, z = jnp.split(h @ w_in, 2, axis=-1)
    u = u.astype(f32).reshape(bsz, seq, G, C)

    dt = jnp.exp(log_dt.astype(f32))[:, None]
    lre = jnp.minimum(lambda_re.astype(f32), -1e-4)
    lim = lambda_im.astype(f32)
    mag = jnp.exp(lre * dt)
    ab_re = mag * jnp.cos(lim * dt)
    ab_im = mag * jnp.sin(lim * dt)
    den = lre * lre + lim * lim
    nr = ab_re - 1.0
    coef_re = (nr * lre + ab_im * lim) / den
    coef_im = (ab_im * lre - nr * lim) / den
    br32, bi32 = b_re.astype(f32), b_im.astype(f32)
    bb_re = coef_re[..., None] * br32 - coef_im[..., None] * bi32
    bb_im = coef_re[..., None] * bi32 + coef_im[..., None] * br32
    cr32, ci32 = c_re.astype(f32), c_im.astype(f32)

    n_chunks = seq // SCAN_CHUNK
    u_chunks = jnp.moveaxis(u.reshape(bsz, n_chunks, SCAN_CHUNK, G, C), 1, 0)
    a_re = jnp.broadcast_to(ab_re, (bsz, SCAN_CHUNK, G, P))
    a_im = jnp.broadcast_to(ab_im, (bsz, SCAN_CHUNK, G, P))

    def chunk_step(carry, u_c):
        x_re, x_im = carry
        bu_re = jnp.einsum('gpc,bkgc->bkgp', bb_re, u_c)
        bu_im = jnp.einsum('gpc,bkgc->bkgp', bb_im, u_c)
        ar, ai, sr, si = lax.associative_scan(_complex_affine_combine, (a_re, a_im, bu_re, bu_im), axis=1)
        xr = sr + ar * x_re[:, None] - ai * x_im[:, None]
        xi = si + ar * x_im[:, None] + ai * x_re[:, None]
        y = jnp.einsum('gcp,bkgp->bkgc', cr32, xr) - jnp.einsum('gcp,bkgp->bkgc', ci32, xi)
        return (xr[:, -1], xi[:, -1]), y

    x0 = jnp.zeros((bsz, G, P), f32)
    _, y = lax.scan(chunk_step, (x0, x0), u_chunks)
    y = jnp.moveaxis(y, 0, 1).reshape(bsz, seq, G, C) + d_skip.astype(f32) * u
    y = jax.nn.gelu(y.reshape(bsz, seq, SSM_WIDTH)).astype(h.dtype)
    ga, gb = jnp.split(y @ w_glu, 2, axis=-1)
    y = ga * jax.nn.sigmoid(gb)
    y = y * jax.nn.silu(z)
    return y @ w_out


def _normal(key, shape, scale):
    return jax.random.normal(key, shape, jnp.float32) * scale


def _nsa_params(key, p):
    ks = jax.random.split(key, 9)
    return {
        p + 'norm': 1.0 + _normal(ks[0], (D_MODEL,), 0.02),
        p + 'w_in': _normal(ks[1], (D_MODEL, NSA_IN), D_MODEL ** -0.5),
        p + 'cmp_k_pe': _normal(ks[2], (CMP_BLOCK, HEAD_DIM), 0.02),
        p + 'cmp_k_w1': _normal(ks[3], (CMP_BLOCK * HEAD_DIM, CMP_HIDDEN), (CMP_BLOCK * HEAD_DIM) ** -0.5),
        p + 'cmp_k_w2': _normal(ks[4], (CMP_HIDDEN, HEAD_DIM), CMP_HIDDEN ** -0.5),
        p + 'cmp_v_pe': _normal(ks[5], (CMP_BLOCK, HEAD_DIM), 0.02),
        p + 'cmp_v_w1': _normal(ks[6], (CMP_BLOCK * HEAD_DIM, CMP_HIDDEN), (CMP_BLOCK * HEAD_DIM) ** -0.5),
        p + 'cmp_v_w2': _normal(ks[7], (CMP_HIDDEN, HEAD_DIM), CMP_HIDDEN ** -0.5),
        p + 'w_out': _normal(ks[8], (ATTN_WIDTH, D_MODEL), ATTN_WIDTH ** -0.5),
    }


def _s5_params(key, p):
    ks = jax.random.split(key, 12)
    n = jnp.arange(SSM_STATE, dtype=jnp.float32)[None, :]
    return {
        p + 'norm': 1.0 + _normal(ks[0], (D_MODEL,), 0.02),
        p + 'w_in': _normal(ks[1], (D_MODEL, 2 * SSM_WIDTH), D_MODEL ** -0.5),
        p + 'log_dt': jax.random.uniform(ks[2], (SSM_GROUPS,), jnp.float32, math.log(DT_MIN), math.log(DT_MAX)),
        p + 'lambda_re': -0.5 + _normal(ks[3], (SSM_GROUPS, SSM_STATE), 0.01),
        p + 'lambda_im': math.pi * n + _normal(ks[4], (SSM_GROUPS, SSM_STATE), 0.01),
        p + 'b_re': _normal(ks[5], (SSM_GROUPS, SSM_STATE, SSM_GROUP), (2 * SSM_GROUP) ** -0.5),
        p + 'b_im': _normal(ks[6], (SSM_GROUPS, SSM_STATE, SSM_GROUP), (2 * SSM_GROUP) ** -0.5),
        p + 'c_re': _normal(ks[7], (SSM_GROUPS, SSM_GROUP, SSM_STATE), SSM_STATE ** -0.5),
        p + 'c_im': _normal(ks[8], (SSM_GROUPS, SSM_GROUP, SSM_STATE), SSM_STATE ** -0.5),
        p + 'd': _normal(ks[9], (SSM_GROUPS, SSM_GROUP), 1.0),
        p + 'w_glu': _normal(ks[10], (SSM_WIDTH, 2 * SSM_WIDTH), SSM_WIDTH ** -0.5),
        p + 'w_out': _normal(ks[11], (SSM_WIDTH, D_MODEL), SSM_WIDTH ** -0.5),
    }


def setup_inputs(seed: int = 0) -> dict:
    key = jax.random.key(seed)
    kx, k0, k1, k2, k3, kf = jax.random.split(key, 6)
    inputs = {'x': jax.random.normal(kx, (BATCH, SEQ, D_MODEL), jnp.float32)}
    inputs.update(_nsa_params(k0, 'l0_'))
    inputs.update(_s5_params(k1, 'l1_'))
    inputs.update(_nsa_params(k2, 'l2_'))
    inputs.update(_s5_params(k3, 'l3_'))
    inputs['final_norm'] = 1.0 + _normal(kf, (D_MODEL,), 0.02)
    return inputs


def reference(x,
              l0_norm, l0_w_in, l0_cmp_k_pe, l0_cmp_k_w1, l0_cmp_k_w2, l0_cmp_v_pe, l0_cmp_v_w1, l0_cmp_v_w2, l0_w_out,
              l1_norm, l1_w_in, l1_log_dt, l1_lambda_re, l1_lambda_im, l1_b_re, l1_b_im, l1_c_re, l1_c_im, l1_d, l1_w_glu, l1_w_out,
              l2_norm, l2_w_in, l2_cmp_k_pe, l2_cmp_k_w1, l2_cmp_k_w2, l2_cmp_v_pe, l2_cmp_v_w1, l2_cmp_v_w2, l2_w_out,
              l3_norm, l3_w_in, l3_log_dt, l3_lambda_re, l3_lambda_im, l3_b_re, l3_b_im, l3_c_re, l3_c_im, l3_d, l3_w_glu, l3_w_out,
              final_norm):
    nsa_layers = [
        (l0_norm, l0_w_in, l0_cmp_k_pe, l0_cmp_k_w1, l0_cmp_k_w2, l0_cmp_v_pe, l0_cmp_v_w1, l0_cmp_v_w2, l0_w_out),
        (l2_norm, l2_w_in, l2_cmp_k_pe, l2_cmp_k_w1, l2_cmp_k_w2, l2_cmp_v_pe, l2_cmp_v_w1, l2_cmp_v_w2, l2_w_out),
    ]
    s5_layers = [
        (l1_norm, l1_w_in, l1_log_dt, l1_lambda_re, l1_lambda_im, l1_b_re, l1_b_im, l1_c_re, l1_c_im, l1_d, l1_w_glu, l1_w_out),
        (l3_norm, l3_w_in, l3_log_dt, l3_lambda_re, l3_lambda_im, l3_b_re, l3_b_im, l3_c_re, l3_c_im, l3_d, l3_w_glu, l3_w_out),
    ]
    h = x
    for i in range(DEPTH):
        if i % N_MIXERS == 0:
            params = nsa_layers[i // N_MIXERS]
            y = _nsa_mixer(_rmsnorm(h, params[0]), *params[1:])
        else:
            params = s5_layers[i // N_MIXERS]
            y = _s5_mixer(_rmsnorm(h, params[0]), *params[1:])
        h = h + y.astype(h.dtype)
    return _rmsnorm(h, final_norm)
```

```python
import functools

import jax
import jax.numpy as jnp
from jax import lax
from jax.experimental import pallas as pl
from jax.experimental.pallas import tpu as pltpu

F32 = jnp.float32
BF16 = jnp.bfloat16

D_MODEL = 1024
EPS = 1e-6
NEG_INF = -1e30
FORCE_SCORE = 1e9

N_HEADS = 16
HEAD_DIM = 64
N_KV_GROUPS = 4
HEADS_PER_GROUP = N_HEADS // N_KV_GROUPS
ATTN_WIDTH = N_HEADS * HEAD_DIM
KV_WIDTH = N_KV_GROUPS * HEAD_DIM
CMP_BLOCK = 32
CMP_STRIDE = 16
CMP_HIDDEN = 256
SEL_BLOCK = 64
SEL_TOPK = 16
WINDOW = 512
N_GATES = 3 * N_HEADS
GATES_PAD = 128

SSM_WIDTH = D_MODEL
SSM_GROUP = 16
SSM_GROUPS = SSM_WIDTH // SSM_GROUP
SSM_STATE = 64
S5_CHUNK = 16
S5_PAIR = 2 * S5_CHUNK * SSM_GROUP

ROW_TILE = 256
ATTN_TQ = 128
ATTN_TK = 512
VMEM_LIMIT = 56 * 1024 * 1024


def _params(*sem):
    return pltpu.CompilerParams(dimension_semantics=sem, vmem_limit_bytes=VMEM_LIMIT)


def _rms(x, g):
    return x * lax.rsqrt(jnp.mean(x * x, axis=-1, keepdims=True) + EPS) * g


def _silu(x):
    return x * jax.nn.sigmoid(x)


def _nsa_in_kernel(h_ref, g_ref, w_ref, q_ref, kvc_ref, kvsw_ref, gates_ref, z_ref):
    xn = _rms(h_ref[...], g_ref[...])
    y = jnp.dot(xn.astype(BF16), w_ref[...], preferred_element_type=F32)
    o = 0
    q_ref[...] = (y[:, o:o + ATTN_WIDTH] * (HEAD_DIM ** -0.5)).astype(BF16)
    o += ATTN_WIDTH
    kvc_ref[...] = y[:, o:o + 2 * KV_WIDTH]
    o += 2 * KV_WIDTH
    kvsw_ref[...] = y[:, o:o + 4 * KV_WIDTH].astype(BF16)
    o += 4 * KV_WIDTH
    z_ref[...] = y[:, o:o + ATTN_WIDTH]
    o += ATTN_WIDTH
    gates_ref[...] = y[:, o:o + GATES_PAD]


def _nsa_in(h, norm, w_in):
    t = h.shape[0]
    a, b = ATTN_WIDTH + 6 * KV_WIDTH, ATTN_WIDTH + 6 * KV_WIDTH + N_GATES
    w = jnp.concatenate([w_in[:, :a], w_in[:, b:], w_in[:, a:b],
                         jnp.zeros((D_MODEL, GATES_PAD - N_GATES), w_in.dtype)], axis=1).astype(BF16)
    n = w.shape[1]
    row = lambda width: pl.BlockSpec((ROW_TILE, width), lambda i: (i, 0))
    return pl.pallas_call(
        _nsa_in_kernel,
        grid=(t // ROW_TILE,),
        in_specs=[row(D_MODEL), pl.BlockSpec((1, D_MODEL), lambda i: (0, 0)),
                  pl.BlockSpec((D_MODEL, n), lambda i: (0, 0))],
        out_specs=[row(ATTN_WIDTH), row(2 * KV_WIDTH), row(4 * KV_WIDTH), row(GATES_PAD), row(ATTN_WIDTH)],
        out_shape=[jax.ShapeDtypeStruct((t, ATTN_WIDTH), BF16), jax.ShapeDtypeStruct((t, 2 * KV_WIDTH), F32),
                   jax.ShapeDtypeStruct((t, 4 * KV_WIDTH), BF16), jax.ShapeDtypeStruct((t, GATES_PAD), F32),
                   jax.ShapeDtypeStruct((t, ATTN_WIDTH), F32)],
        compiler_params=_params("parallel"),
        name="nsa_in_proj",
    )(h, norm.reshape(1, D_MODEL), w)


def _s5_in_kernel(h_ref, g_ref, w_ref, u_ref, z_ref):
    xn = _rms(h_ref[...], g_ref[...])
    y = jnp.dot(xn.astype(BF16), w_ref[...], preferred_element_type=F32)
    u_ref[...] = y[:, :SSM_WIDTH]
    z_ref[...] = y[:, SSM_WIDTH:]


def _s5_in(h, norm, w_in):
    t = h.shape[0]
    row = lambda width: pl.BlockSpec((ROW_TILE, width), lambda i: (i, 0))
    return pl.pallas_call(
        _s5_in_kernel,
        grid=(t // ROW_TILE,),
        in_specs=[row(D_MODEL), pl.BlockSpec((1, D_MODEL), lambda i: (0, 0)),
                  pl.BlockSpec((D_MODEL, 2 * SSM_WIDTH), lambda i: (0, 0))],
        out_specs=[row(SSM_WIDTH), row(SSM_WIDTH)],
        out_shape=[jax.ShapeDtypeStruct((t, SSM_WIDTH), F32), jax.ShapeDtypeStruct((t, SSM_WIDTH), F32)],
        compiler_params=_params("parallel"),
        name="s5_in_proj",
    )(h, norm.reshape(1, D_MODEL), w_in.astype(BF16))


def _nsa_out_kernel(o_ref, z_ref, w_ref, res_ref, out_ref):
    a = o_ref[...] * _silu(z_ref[...])
    out_ref[...] = res_ref[...] + jnp.dot(a.astype(BF16), w_ref[...], preferred_element_type=F32)


def _nsa_out(o, z, w_out, res):
    t = o.shape[0]
    row = pl.BlockSpec((ROW_TILE, D_MODEL), lambda i: (i, 0))
    return pl.pallas_call(
        _nsa_out_kernel,
        grid=(t // ROW_TILE,),
        in_specs=[row, row, pl.BlockSpec((ATTN_WIDTH, D_MODEL), lambda i: (0, 0)), row],
        out_specs=row,
        out_shape=jax.ShapeDtypeStruct((t, D_MODEL), F32),
        compiler_params=_params("parallel"),
        name="nsa_out_proj",
    )(o, z, w_out.astype(BF16), res)


def _s5_out_kernel(y_ref, z_ref, wg_ref, wo_ref, res_ref, fn_ref, out_ref, *, final):
    gl = jnp.dot(y_ref[...], wg_ref[...], preferred_element_type=F32)
    v = gl[:, :SSM_WIDTH] * jax.nn.sigmoid(gl[:, SSM_WIDTH:])
    v = v * _silu(z_ref[...])
    h = res_ref[...] + jnp.dot(v.astype(BF16), wo_ref[...], preferred_element_type=F32)
    out_ref[...] = _rms(h, fn_ref[...]) if final else h


def _s5_out(y, z, w_glu, w_out, res, final_norm, final):
    t = y.shape[0]
    row = pl.BlockSpec((ROW_TILE, D_MODEL), lambda i: (i, 0))
    return pl.pallas_call(
        functools.partial(_s5_out_kernel, final=final),
        grid=(t // ROW_TILE,),
        in_specs=[row, row, pl.BlockSpec((SSM_WIDTH, 2 * SSM_WIDTH), lambda i: (0, 0)),
                  pl.BlockSpec((SSM_WIDTH, D_MODEL), lambda i: (0, 0)), row,
                  pl.BlockSpec((1, D_MODEL), lambda i: (0, 0))],
        out_specs=row,
        out_shape=jax.ShapeDtypeStruct((t, D_MODEL), F32),
        compiler_params=_params("parallel"),
        name="s5_glu_out_proj",
    )(y, z, w_glu.astype(BF16), w_out.astype(BF16), res, final_norm.reshape(1, D_MODEL))


def _cmp_one(a_ref, pe_ref, w1_ref, w2_ref, out_ref):
    nc = a_ref.shape[0]
    a = a_ref[...]
    pe = pe_ref[...]
    h_top = jnp.dot((a + pe[0:1]).astype(BF16), w1_ref[0], preferred_element_type=F32)
    h_bot = jnp.dot((a + pe[1:2]).astype(BF16), w1_ref[1], preferred_element_type=F32)
    hid = _silu(h_top + pltpu.roll(h_bot, nc - 1, 0))
    out = jnp.dot(hid.astype(BF16), w2_ref[...], preferred_element_type=F32)
    keep = lax.broadcasted_iota(jnp.int32, out.shape, 0) < nc - 1
    out_ref[...] = jnp.where(keep, out, 0.0).astype(BF16)


def _cmp_kernel(ka_ref, va_ref, kpe_ref, kw1_ref, kw2_ref, vpe_ref, vw1_ref, vw2_ref, kc_ref, vc_ref):
    _cmp_one(ka_ref, kpe_ref, kw1_ref, kw2_ref, kc_ref)
    _cmp_one(va_ref, vpe_ref, vw1_ref, vw2_ref, vc_ref)


def _compress(ka, va, k_pe, k_w1, k_w2, v_pe, v_w1, v_w2):
    bg, nc, half = ka.shape
    blk = pl.BlockSpec((None, nc, half), lambda i: (i, 0, 0))
    full = lambda shape: pl.BlockSpec(shape, lambda i: (0,) * len(shape))
    prep = lambda pe, w1, w2: (pe.reshape(2, half), w1.reshape(2, half, CMP_HIDDEN).astype(BF16), w2.astype(BF16))
    out = pl.BlockSpec((None, nc, HEAD_DIM), lambda i: (i, 0, 0))
    wspecs = [full((2, half)), full((2, half, CMP_HIDDEN)), full((CMP_HIDDEN, HEAD_DIM))]
    return pl.pallas_call(
        _cmp_kernel,
        grid=(bg,),
        in_specs=[blk, blk] + wspecs + wspecs,
        out_specs=[out, out],
        out_shape=[jax.ShapeDtypeStruct((bg, nc, HEAD_DIM), BF16)] * 2,
        compiler_params=_params("parallel"),
        name="nsa_compress",
    )(ka, va, *prep(k_pe, k_w1, k_w2), *prep(v_pe, v_w1, v_w2))


def _dot_nt(a, b):
    return lax.dot_general(a, b, (((1,), (1,)), ((), ())), preferred_element_type=F32)


def _split3(x):
    hi = x.astype(BF16)
    r1 = x - hi.astype(F32)
    mid = r1.astype(BF16)
    lo = (r1 - mid.astype(F32)).astype(BF16)
    return hi, mid, lo


def _attn_kernel(q_ref, ks_ref, vs_ref, kw_ref, vw_ref, kc_ref, vc_ref, gt_ref, ovt_ref, e_ref,
                 o_ref, m_scr, l_scr, acc_scr, *, seq, tq, tk):
    nb = seq // SEL_BLOCK
    nc = seq // CMP_STRIDE
    rows = HEADS_PER_GROUP * tq
    g = pl.program_id(0) % N_KV_GROUPS
    q0 = pl.program_id(1) * tq

    q2 = q_ref[...].reshape(rows, HEAD_DIM)
    row = lax.broadcasted_iota(jnp.int32, (rows, 1), 0)
    head = row // tq
    t_col = q0 + (row - head * tq)
    t_f = t_col.astype(F32)
    slope = jnp.exp2(-0.5 * (g * HEADS_PER_GROUP + head + 1).astype(F32))

    ci = lax.broadcasted_iota(jnp.int32, (1, nc), 1)
    center = (ci * CMP_STRIDE).astype(F32) + 0.5 * (CMP_BLOCK - 1)
    mask = (ci * CMP_STRIDE + (CMP_BLOCK - 1)) <= t_col
    s = _dot_nt(q2, kc_ref[...]) - slope * (t_f - center)
    s = jnp.where(mask, s, NEG_INF)
    p = jnp.where(mask, jnp.exp(s - jnp.max(s, axis=-1, keepdims=True)), 0.0)
    p = p / jnp.maximum(jnp.sum(p, axis=-1, keepdims=True), 1e-30)
    o_c = jnp.dot(p.astype(BF16), vc_ref[...], preferred_element_type=F32)

    psum = p[0:tq] + p[tq:2 * tq] + p[2 * tq:3 * tq] + p[3 * tq:4 * tq]
    ovt = ovt_ref[...]
    imp = sum(_dot_nt(ovt, part) for part in _split3(psum))
    j = lax.broadcasted_iota(jnp.int32, (nb, tq), 0)
    jt = (q0 + lax.broadcasted_iota(jnp.int32, (nb, tq), 1)) // SEL_BLOCK
    forced = (j == 0) | (j == jt) | (j == jt - 1)
    vals = jnp.where(forced, FORCE_SCORE, jnp.where(j > jt, -FORCE_SCORE, imp))
    jf = j.astype(F32)
    sel_t = jnp.zeros((nb, tq), F32)
    for _ in range(min(SEL_TOPK, nb)):
        best = jnp.max(vals, axis=0, keepdims=True)
        first = jnp.min(jnp.where(vals == best, jf, float(nb)), axis=0, keepdims=True)
        hit = jf == first
        sel_t = jnp.where(hit, 1.0, sel_t)
        vals = jnp.where(hit, -3e38, vals)
    sel = sel_t.T.astype(BF16)

    m_scr[...] = jnp.full((rows, 1), NEG_INF, F32)
    l_scr[...] = jnp.zeros((rows, 1), F32)
    acc_scr[...] = jnp.zeros((rows, HEAD_DIM), F32)

    def sel_step(kt, carry):
        kb = pl.multiple_of(kt * tk, tk)
        k = ks_ref[pl.ds(kb, tk), :]
        v = vs_ref[pl.ds(kb, tk), :]
        d = t_col - (kb + lax.broadcasted_iota(jnp.int32, (1, tk), 1))
        chosen = jnp.dot(sel, e_ref[:, pl.ds(kb, tk)], preferred_element_type=F32)
        chosen = jnp.concatenate([chosen] * HEADS_PER_GROUP, axis=0)
        ok = (d >= 0) & (chosen > 0.5)
        sc = jnp.where(ok, _dot_nt(q2, k) - slope * d.astype(F32), NEG_INF)
        m_old = m_scr[...]
        m_new = jnp.maximum(m_old, jnp.max(sc, axis=-1, keepdims=True))
        pr = jnp.where(ok, jnp.exp(sc - m_new), 0.0)
        alpha = jnp.exp(m_old - m_new)
        l_scr[...] = alpha * l_scr[...] + jnp.sum(pr, axis=-1, keepdims=True)
        acc_scr[...] = alpha * acc_scr[...] + jnp.dot(pr.astype(BF16), v, preferred_element_type=F32)
        m_scr[...] = m_new
        return carry

    lax.fori_loop(0, (q0 + tq + tk - 1) // tk, sel_step, 0)
    o_s = acc_scr[...] / jnp.maximum(l_scr[...], 1e-30)

    wlen = WINDOW + tq
    w0 = pl.multiple_of(jnp.maximum(q0 - WINDOW, 0), tq)
    d = t_col - (w0 + lax.broadcasted_iota(jnp.int32, (1, wlen), 1))
    ok = (d >= 0) & (d < WINDOW)
    sw = jnp.where(ok, _dot_nt(q2, kw_ref[pl.ds(w0, wlen), :]) - slope * d.astype(F32), NEG_INF)
    pw = jnp.where(ok, jnp.exp(sw - jnp.max(sw, axis=-1, keepdims=True)), 0.0)
    lw = jnp.maximum(jnp.sum(pw, axis=-1, keepdims=True), 1e-30)
    o_w = jnp.dot(pw.astype(BF16), vw_ref[pl.ds(w0, wlen), :], preferred_element_type=F32) / lw

    gate = jax.nn.sigmoid(gt_ref[...])
    for r in range(HEADS_PER_GROUP):
        sl = slice(r * tq, (r + 1) * tq)
        gc = lambda x: gate[:, x * HEADS_PER_GROUP + r:x * HEADS_PER_GROUP + r + 1]
        o_ref[:, r * HEAD_DIM:(r + 1) * HEAD_DIM] = gc(0) * o_c[sl] + gc(1) * o_s[sl] + gc(2) * o_w[sl]


def _attention(qr, ks, vs, kw, vw, kc, vc, gt, batch, seq):
    tq, tk = ATTN_TQ, ATTN_TK
    bg = batch * N_KV_GROUPS
    nq = seq // tq
    nb, nc = seq // SEL_BLOCK, seq // CMP_STRIDE
    rows = HEADS_PER_GROUP * tq
    gq = HEADS_PER_GROUP * HEAD_DIM
    cs = jnp.arange(nc)[None, :] * CMP_STRIDE
    ss = jnp.arange(nb)[:, None] * SEL_BLOCK
    ovt = ((cs <= ss + SEL_BLOCK - 1) & (cs + CMP_BLOCK - 1 >= ss)).astype(BF16)
    expand = (jnp.arange(seq)[None, :] // SEL_BLOCK == jnp.arange(nb)[:, None]).astype(BF16)
    per_bg = lambda n: pl.BlockSpec((None, n, HEAD_DIM), lambda i, j: (i, 0, 0))
    return pl.pallas_call(
        functools.partial(_attn_kernel, seq=seq, tq=tq, tk=tk),
        grid=(bg, nq),
        in_specs=[pl.BlockSpec((None, HEADS_PER_GROUP, tq, HEAD_DIM), lambda i, j: (i, 0, j, 0)),
                  per_bg(seq), per_bg(seq), per_bg(seq), per_bg(seq), per_bg(nc), per_bg(nc),
                  pl.BlockSpec((None, tq, 3 * HEADS_PER_GROUP), lambda i, j: (i, j, 0)),
                  pl.BlockSpec((nb, nc), lambda i, j: (0, 0)),
                  pl.BlockSpec((nb, seq), lambda i, j: (0, 0))],
        out_specs=pl.BlockSpec((tq, gq), lambda i, j: ((i // N_KV_GROUPS) * nq + j, i % N_KV_GROUPS)),
        out_shape=jax.ShapeDtypeStruct((batch * seq, ATTN_WIDTH), F32),
        scratch_shapes=[pltpu.VMEM((rows, 1), F32), pltpu.VMEM((rows, 1), F32),
                        pltpu.VMEM((rows, HEAD_DIM), F32)],
        compiler_params=_params("parallel", "arbitrary"),
        name="nsa_attention",
    )(qr, ks, vs, kw, vw, kc, vc, gt, ovt, expand)


def _nsa_layer(h, batch, seq, norm, w_in, k_pe, k_w1, k_w2, v_pe, v_w1, v_w2, w_out):
    gg, rr, dh = N_KV_GROUPS, HEADS_PER_GROUP, HEAD_DIM
    q, kvc, kvsw, gates, z = _nsa_in(h, norm, w_in)
    qr = q.reshape(batch, seq, gg, rr, dh).transpose(0, 2, 3, 1, 4).reshape(batch * gg, rr, seq, dh)
    kv4 = kvsw.reshape(batch, seq, 4, gg, dh).transpose(2, 0, 3, 1, 4).reshape(4, batch * gg, seq, dh)
    half = CMP_STRIDE * dh
    kvc2 = (kvc.reshape(batch, seq // CMP_STRIDE, CMP_STRIDE, 2, gg, dh).transpose(3, 0, 4, 1, 2, 5)
            .reshape(2, batch * gg, seq // CMP_STRIDE, half))
    gt = (gates[:, :N_GATES].reshape(batch, seq, 3, gg, rr).transpose(0, 3, 1, 2, 4)
          .reshape(batch * gg, seq, 3 * rr))
    kc, vc = _compress(kvc2[0], kvc2[1], k_pe, k_w1, k_w2, v_pe, v_w1, v_w2)
    o = _attention(qr, kv4[0], kv4[1], kv4[2], kv4[3], kc, vc, gt, batch, seq)
    return _nsa_out(o, z, w_out, h)


def _split2(x):
    hi = x.astype(BF16)
    return hi, (x - hi.astype(F32)).astype(BF16)


def _dot3(a, b_hi, b_lo):
    a_hi, a_lo = _split2(a)
    return (jnp.dot(a_hi, b_hi, preferred_element_type=F32) + jnp.dot(a_lo, b_hi, preferred_element_type=F32)
            + jnp.dot(a_hi, b_lo, preferred_element_type=F32))


def _s5_state_kernel(u_ref, wh_ref, wl_ref, xr_ref, xi_ref):
    x = _dot3(u_ref[...], wh_ref[...], wl_ref[...])
    half = x.shape[1] // 2
    xr_ref[...] = x[:, :half]
    xi_ref[...] = x[:, half:]


def _s5_scan_kernel(xr_ref, xi_ref, ar_ref, ai_ref, cr_ref, ci_ref, *, batch, nchunk):
    ar, ai = ar_ref[...], ai_ref[...]
    width = ar.shape[1]

    def step(n, carry):
        new = []
        for b in range(batch):
            cr, ci = carry[2 * b], carry[2 * b + 1]
            idx = b * nchunk + n
            cr_ref[pl.ds(idx, 1), :] = cr
            ci_ref[pl.ds(idx, 1), :] = ci
            lr, li = xr_ref[pl.ds(idx, 1), :], xi_ref[pl.ds(idx, 1), :]
            new += [ar * cr - ai * ci + lr, ar * ci + ai * cr + li]
        return tuple(new)

    zero = jnp.zeros((1, width), F32)
    lax.fori_loop(0, nchunk, step, (zero,) * (2 * batch))


def _s5_y_kernel(u_ref, cr_ref, ci_ref, mh_ref, ml_ref, vh_ref, vl_ref, d_ref, y_ref):
    u = u_ref[...]
    carry = jnp.concatenate([cr_ref[...], ci_ref[...]], axis=1)
    y = _dot3(u, mh_ref[...], ml_ref[...]) + _dot3(carry, vh_ref[...], vl_ref[...]) + d_ref[...] * u
    y_ref[...] = jax.nn.gelu(y).astype(BF16)


def _s5_matrices(log_dt, lambda_re, lambda_im, b_re, b_im, c_re, c_im, d_skip):
    hp = lax.Precision.HIGHEST
    gn, pn, cn, ln = SSM_GROUPS, SSM_STATE, SSM_GROUP, S5_CHUNK
    dt = jnp.exp(log_dt.astype(F32))[:, None]
    lre = jnp.minimum(lambda_re.astype(F32), -1e-4)
    lim = lambda_im.astype(F32)
    mag = jnp.exp(lre * dt)
    ab_re, ab_im = mag * jnp.cos(lim * dt), mag * jnp.sin(lim * dt)
    den = lre * lre + lim * lim
    nr = ab_re - 1.0
    coef_re = (nr * lre + ab_im * lim) / den
    coef_im = (ab_im * lre - nr * lim) / den
    br32, bi32 = b_re.astype(F32), b_im.astype(F32)
    bb_re = coef_re[..., None] * br32 - coef_im[..., None] * bi32
    bb_im = coef_re[..., None] * bi32 + coef_im[..., None] * br32
    cr32, ci32 = c_re.astype(F32), c_im.astype(F32)
    pr, pi = [jnp.ones_like(ab_re)], [jnp.zeros_like(ab_re)]
    for _ in range(ln):
        pr, pi = pr + [pr[-1] * ab_re - pi[-1] * ab_im], pi + [pr[-1] * ab_im + pi[-1] * ab_re]
    pw_re, pw_im = jnp.stack(pr), jnp.stack(pi)
    ab_b_re = pw_re[:ln, :, :, None] * bb_re - pw_im[:ln, :, :, None] * bb_im
    ab_b_im = pw_re[:ln, :, :, None] * bb_im + pw_im[:ln, :, :, None] * bb_re
    kern = (jnp.einsum('gop,kgpi->kgoi', cr32, ab_b_re, precision=hp)
            - jnp.einsum('gop,kgpi->kgoi', ci32, ab_b_im, precision=hp))
    s_idx, t_idx = jnp.arange(ln)[:, None], jnp.arange(ln)[None, :]
    lag = jnp.clip(t_idx - s_idx, 0, ln - 1)
    toep = jnp.where((t_idx >= s_idx)[:, :, None, None, None], kern[lag], 0.0)
    m_in_out = toep.transpose(2, 0, 4, 1, 3).reshape(gn, ln * cn, ln * cn)
    w_re = ab_b_re[::-1].transpose(1, 0, 3, 2).reshape(gn, ln * cn, pn)
    w_im = ab_b_im[::-1].transpose(1, 0, 3, 2).reshape(gn, ln * cn, pn)
    ar1, ai1 = pw_re[1:], pw_im[1:]
    v_re = (cr32[None] * ar1[:, :, None, :] - ci32[None] * ai1[:, :, None, :])
    v_im = -(cr32[None] * ai1[:, :, None, :] + ci32[None] * ar1[:, :, None, :])
    v_re = v_re.transpose(1, 3, 0, 2).reshape(gn, pn, ln * cn)
    v_im = v_im.transpose(1, 3, 0, 2).reshape(gn, pn, ln * cn)

    def pair_diag(x):
        e, o = x[0::2], x[1::2]
        z = jnp.zeros_like(e)
        return jnp.concatenate([jnp.concatenate([e, z], axis=2), jnp.concatenate([z, o], axis=2)], axis=1)

    m_pair = pair_diag(m_in_out)
    zw = jnp.zeros_like(w_re[0::2])
    w_pair = jnp.concatenate([jnp.concatenate([w_re[0::2], zw, w_im[0::2], zw], axis=2),
                              jnp.concatenate([zw, w_re[1::2], zw, w_im[1::2]], axis=2)], axis=1)
    zv = jnp.zeros_like(v_re[0::2])
    v_pair = jnp.concatenate([jnp.concatenate([v_re[0::2], zv], axis=2), jnp.concatenate([zv, v_re[1::2]], axis=2),
                              jnp.concatenate([v_im[0::2], zv], axis=2), jnp.concatenate([zv, v_im[1::2]], axis=2)],
                             axis=1)
    d_pair = jnp.tile(d_skip.astype(F32).reshape(gn // 2, 2, 1, cn), (1, 1, ln, 1)).reshape(gn // 2, 1, S5_PAIR)
    a_end_re = pw_re[ln].reshape(1, gn * pn)
    a_end_im = pw_im[ln].reshape(1, gn * pn)
    return _split2(m_pair), _split2(w_pair), _split2(v_pair), d_pair, a_end_re, a_end_im


def _s5_layer(h, batch, seq, norm, w_in, log_dt, lambda_re, lambda_im, b_re, b_im, c_re, c_im, d_skip,
              w_glu, w_out, final_norm, final):
    gp, ln, cn = SSM_GROUPS // 2, S5_CHUNK, SSM_GROUP
    nchunk = seq // ln
    rows = batch * nchunk
    state_w = 2 * SSM_STATE
    (mh, ml), (wh, wl), (vh, vl), d_pair, a_re, a_im = _s5_matrices(
        log_dt, lambda_re, lambda_im, b_re, b_im, c_re, c_im, d_skip)
    u, z = _s5_in(h, norm, w_in)
    up = u.reshape(batch, nchunk, ln, gp, 2, cn).transpose(3, 0, 1, 4, 2, 5).reshape(gp, rows, S5_PAIR)
    pair3 = lambda a, b: pl.BlockSpec((None, a, b), lambda i: (i, 0, 0))
    slab = pl.BlockSpec((rows, state_w), lambda i: (0, i))
    xr, xi = pl.pallas_call(
        _s5_state_kernel,
        grid=(gp,),
        in_specs=[pair3(rows, S5_PAIR), pair3(S5_PAIR, 2 * state_w), pair3(S5_PAIR, 2 * state_w)],
        out_specs=[slab, slab],
        out_shape=[jax.ShapeDtypeStruct((rows, gp * state_w), F32)] * 2,
        compiler_params=_params("parallel"),
        name="s5_chunk_state",
    )(up, wh, wl)
    scan_w = 4 * state_w
    wide = pl.BlockSpec((rows, scan_w), lambda i: (0, i))
    coef = pl.BlockSpec((1, scan_w), lambda i: (0, i))
    cr, ci = pl.pallas_call(
        functools.partial(_s5_scan_kernel, batch=batch, nchunk=nchunk),
        grid=(gp * state_w // scan_w,),
        in_specs=[wide, wide, coef, coef],
        out_specs=[wide, wide],
        out_shape=[jax.ShapeDtypeStruct((rows, gp * state_w), F32)] * 2,
        compiler_params=_params("parallel"),
        name="s5_carry_scan",
    )(xr, xi, a_re, a_im)
    yp = pl.pallas_call(
        _s5_y_kernel,
        grid=(gp,),
        in_specs=[pair3(rows, S5_PAIR), slab, slab, pair3(S5_PAIR, S5_PAIR), pair3(S5_PAIR, S5_PAIR),
                  pair3(2 * state_w, S5_PAIR), pair3(2 * state_w, S5_PAIR), pair3(1, S5_PAIR)],
        out_specs=pair3(rows, S5_PAIR),
        out_shape=jax.ShapeDtypeStruct((gp, rows, S5_PAIR), BF16),
        compiler_params=_params("parallel"),
        name="s5_chunk_output",
    )(up, cr, ci, mh, ml, vh, vl, d_pair)
    y = yp.reshape(gp, batch, nchunk, 2, ln, cn).transpose(1, 2, 4, 0, 3, 5).reshape(batch * seq, SSM_WIDTH)
    return _s5_out(y, z, w_glu, w_out, h, final_norm, final)


def kernel(x, l0_norm, l0_w_in, l0_cmp_k_pe, l0_cmp_k_w1, l0_cmp_k_w2, l0_cmp_v_pe, l0_cmp_v_w1, l0_cmp_v_w2, l0_w_out, l1_norm, l1_w_in, l1_log_dt, l1_lambda_re, l1_lambda_im, l1_b_re, l1_b_im, l1_c_re, l1_c_im, l1_d, l1_w_glu, l1_w_out, l2_norm, l2_w_in, l2_cmp_k_pe, l2_cmp_k_w1, l2_cmp_k_w2, l2_cmp_v_pe, l2_cmp_v_w1, l2_cmp_v_w2, l2_w_out, l3_norm, l3_w_in, l3_log_dt, l3_lambda_re, l3_lambda_im, l3_b_re, l3_b_im, l3_c_re, l3_c_im, l3_d, l3_w_glu, l3_w_out, final_norm):
    batch, seq, _ = x.shape
    h = x.reshape(batch * seq, D_MODEL)
    h = _nsa_layer(h, batch, seq, l0_norm, l0_w_in, l0_cmp_k_pe, l0_cmp_k_w1, l0_cmp_k_w2,
                   l0_cmp_v_pe, l0_cmp_v_w1, l0_cmp_v_w2, l0_w_out)
    h = _s5_layer(h, batch, seq, l1_norm, l1_w_in, l1_log_dt, l1_lambda_re, l1_lambda_im, l1_b_re, l1_b_im,
                  l1_c_re, l1_c_im, l1_d, l1_w_glu, l1_w_out, final_norm, False)
    h = _nsa_layer(h, batch, seq, l2_norm, l2_w_in, l2_cmp_k_pe, l2_cmp_k_w1, l2_cmp_k_w2,
                   l2_cmp_v_pe, l2_cmp_v_w1, l2_cmp_v_w2, l2_w_out)
    h = _s5_layer(h, batch, seq, l3_norm, l3_w_in, l3_log_dt, l3_lambda_re, l3_lambda_im, l3_b_re, l3_b_im,
                  l3_c_re, l3_c_im, l3_d, l3_w_glu, l3_w_out, final_norm, True)
    return h.reshape(batch, seq, D_MODEL)
```

```python
import functools

import jax
import jax.numpy as jnp
from jax import lax
from jax.experimental import pallas as pl
from jax.experimental.pallas import tpu as pltpu

F32 = jnp.float32
BF16 = jnp.bfloat16

D_MODEL = 1024
EPS = 1e-6
NEG_INF = -1e30
FORCE_SCORE = 1e9

N_HEADS = 16
HEAD_DIM = 64
N_KV_GROUPS = 4
HEADS_PER_GROUP = N_HEADS // N_KV_GROUPS
ATTN_WIDTH = N_HEADS * HEAD_DIM
KV_WIDTH = N_KV_GROUPS * HEAD_DIM
CMP_BLOCK = 32
CMP_STRIDE = 16
CMP_HIDDEN = 256
SEL_BLOCK = 64
SEL_TOPK = 16
WINDOW = 512
N_GATES = 3 * N_HEADS
GATES_PAD = 128

SSM_WIDTH = D_MODEL
SSM_GROUP = 16
SSM_GROUPS = SSM_WIDTH // SSM_GROUP
SSM_STATE = 64
S5_CHUNK = 16
S5_PAIRS = SSM_GROUPS // 2
S5_PAIR_LANES = 2 * SSM_GROUP
S5_PAIR = S5_CHUNK * S5_PAIR_LANES

ROW_TILE = 256
ATTN_TQ = 128
ATTN_TK = 512
AUG_HALF = 128
MASK_BIAS = -2.0 ** 100
VMEM_LIMIT = 56 * 1024 * 1024


def _params(*sem):
    return pltpu.CompilerParams(dimension_semantics=sem, vmem_limit_bytes=VMEM_LIMIT)


def _rms(x, g):
    return x * lax.rsqrt(jnp.mean(x * x, axis=-1, keepdims=True) + EPS) * g


def _silu(x):
    return x * jax.nn.sigmoid(x)


def _nsa_in_kernel(h_ref, g_ref, w_ref, q_ref, kvc_ref, kvsw_ref, gates_ref, z_ref):
    xn = _rms(h_ref[...], g_ref[...])
    y = jnp.dot(xn.astype(BF16), w_ref[...], preferred_element_type=F32)
    o = 0
    q_ref[...] = (y[:, o:o + ATTN_WIDTH] * (HEAD_DIM ** -0.5)).astype(BF16)
    o += ATTN_WIDTH
    kvc_ref[...] = y[:, o:o + 2 * KV_WIDTH]
    o += 2 * KV_WIDTH
    kvsw_ref[...] = y[:, o:o + 4 * KV_WIDTH].astype(BF16)
    o += 4 * KV_WIDTH
    z_ref[...] = y[:, o:o + ATTN_WIDTH]
    o += ATTN_WIDTH
    gates_ref[...] = y[:, o:o + GATES_PAD]


def _nsa_in(h, norm, w_in):
    t = h.shape[0]
    a, b = ATTN_WIDTH + 6 * KV_WIDTH, ATTN_WIDTH + 6 * KV_WIDTH + N_GATES
    w = jnp.concatenate([w_in[:, :a], w_in[:, b:], w_in[:, a:b],
                         jnp.zeros((D_MODEL, GATES_PAD - N_GATES), w_in.dtype)], axis=1).astype(BF16)
    n = w.shape[1]
    row = lambda width: pl.BlockSpec((ROW_TILE, width), lambda i: (i, 0))
    return pl.pallas_call(
        _nsa_in_kernel,
        grid=(t // ROW_TILE,),
        in_specs=[row(D_MODEL), pl.BlockSpec((1, D_MODEL), lambda i: (0, 0)),
                  pl.BlockSpec((D_MODEL, n), lambda i: (0, 0))],
        out_specs=[row(ATTN_WIDTH), row(2 * KV_WIDTH), row(4 * KV_WIDTH), row(GATES_PAD), row(ATTN_WIDTH)],
        out_shape=[jax.ShapeDtypeStruct((t, ATTN_WIDTH), BF16), jax.ShapeDtypeStruct((t, 2 * KV_WIDTH), F32),
                   jax.ShapeDtypeStruct((t, 4 * KV_WIDTH), BF16), jax.ShapeDtypeStruct((t, GATES_PAD), F32),
                   jax.ShapeDtypeStruct((t, ATTN_WIDTH), F32)],
        compiler_params=_params("parallel"),
        name="nsa_in_proj",
    )(h, norm.reshape(1, D_MODEL), w)


def _quarter_exchange(parts):
    quarter = lax.broadcasted_iota(jnp.int32, parts[0].shape, 1) // S5_PAIR_LANES
    outs = []
    for b in range(4):
        acc = None
        for a in range(4):
            moved = parts[a] if a == b else pltpu.roll(parts[a], ((a - b) % 4) * S5_PAIR_LANES, 1)
            acc = moved if acc is None else jnp.where(quarter == a, moved, acc)
        outs.append(acc)
    return outs


def _s5_in_kernel(h_ref, g_ref, w_ref, u_ref, z_ref, u_scr):
    xn = _rms(h_ref[...], g_ref[...])
    y = jnp.dot(xn.astype(BF16), w_ref[...], preferred_element_type=F32)
    z_ref[...] = y[:, SSM_WIDTH:]
    nch = ROW_TILE // S5_CHUNK
    for m in range(SSM_WIDTH // 128):
        u_scr[m] = y[:, m * 128:(m + 1) * 128]
        for k in range(S5_CHUNK // 4):
            steps = [u_scr[m, pl.ds(4 * k + a, nch, stride=S5_CHUNK), :] for a in range(4)]
            for b, tile in enumerate(_quarter_exchange(steps)):
                u_ref[4 * m + b, :, k * 128:(k + 1) * 128] = tile


def _s5_in(h, norm, w_in):
    t = h.shape[0]
    nch = ROW_TILE // S5_CHUNK
    row = lambda width: pl.BlockSpec((ROW_TILE, width), lambda i: (i, 0))
    return pl.pallas_call(
        _s5_in_kernel,
        grid=(t // ROW_TILE,),
        in_specs=[row(D_MODEL), pl.BlockSpec((1, D_MODEL), lambda i: (0, 0)),
                  pl.BlockSpec((D_MODEL, 2 * SSM_WIDTH), lambda i: (0, 0))],
        out_specs=[pl.BlockSpec((S5_PAIRS, nch, S5_PAIR), lambda i: (0, i, 0)), row(SSM_WIDTH)],
        out_shape=[jax.ShapeDtypeStruct((S5_PAIRS, t // S5_CHUNK, S5_PAIR), F32),
                   jax.ShapeDtypeStruct((t, SSM_WIDTH), F32)],
        scratch_shapes=[pltpu.VMEM((SSM_WIDTH // 128, ROW_TILE, 128), F32)],
        compiler_params=_params("parallel"),
        name="s5_in_proj",
    )(h, norm.reshape(1, D_MODEL), w_in.astype(BF16))


def _nsa_out_kernel(o_ref, z_ref, w_ref, res_ref, out_ref):
    a = o_ref[...] * _silu(z_ref[...])
    out_ref[...] = res_ref[...] + jnp.dot(a.astype(BF16), w_ref[...], preferred_element_type=F32)


def _nsa_out(o, z, w_out, res):
    t = o.shape[0]
    row = pl.BlockSpec((ROW_TILE, D_MODEL), lambda i: (i, 0))
    return pl.pallas_call(
        _nsa_out_kernel,
        grid=(t // ROW_TILE,),
        in_specs=[row, row, pl.BlockSpec((ATTN_WIDTH, D_MODEL), lambda i: (0, 0)), row],
        out_specs=row,
        out_shape=jax.ShapeDtypeStruct((t, D_MODEL), F32),
        compiler_params=_params("parallel"),
        name="nsa_out_proj",
    )(o, z, w_out.astype(BF16), res)


def _s5_out_kernel(y_ref, z_ref, wg_ref, wo_ref, res_ref, fn_ref, out_ref, y_scr, *, final):
    nch = ROW_TILE // S5_CHUNK
    for m in range(SSM_WIDTH // 128):
        for k in range(S5_CHUNK // 4):
            pairs = [y_ref[4 * m + b, :, k * 128:(k + 1) * 128] for b in range(4)]
            for a, tile in enumerate(_quarter_exchange(pairs)):
                y_scr[m, pl.ds(4 * k + a, nch, stride=S5_CHUNK), :] = tile
    y = jnp.concatenate([y_scr[m] for m in range(SSM_WIDTH // 128)], axis=1)
    gl = jnp.dot(y.astype(BF16), wg_ref[...], preferred_element_type=F32)
    v = gl[:, :SSM_WIDTH] * jax.nn.sigmoid(gl[:, SSM_WIDTH:])
    v = v * _silu(z_ref[...])
    h = res_ref[...] + jnp.dot(v.astype(BF16), wo_ref[...], preferred_element_type=F32)
    out_ref[...] = _rms(h, fn_ref[...]) if final else h


def _s5_out(y, z, w_glu, w_out, res, final_norm, final):
    t = z.shape[0]
    row = pl.BlockSpec((ROW_TILE, D_MODEL), lambda i: (i, 0))
    return pl.pallas_call(
        functools.partial(_s5_out_kernel, final=final),
        grid=(t // ROW_TILE,),
        in_specs=[pl.BlockSpec((S5_PAIRS, ROW_TILE // S5_CHUNK, S5_PAIR), lambda i: (0, i, 0)), row,
                  pl.BlockSpec((SSM_WIDTH, 2 * SSM_WIDTH), lambda i: (0, 0)),
                  pl.BlockSpec((SSM_WIDTH, D_MODEL), lambda i: (0, 0)), row,
                  pl.BlockSpec((1, D_MODEL), lambda i: (0, 0))],
        out_specs=row,
        out_shape=jax.ShapeDtypeStruct((t, D_MODEL), F32),
        scratch_shapes=[pltpu.VMEM((SSM_WIDTH // 128, ROW_TILE, 128), F32)],
        compiler_params=_params("parallel"),
        name="s5_glu_out_proj",
    )(y, z, w_glu.astype(BF16), w_out.astype(BF16), res, final_norm.reshape(1, D_MODEL))


def _cmp_one(a_ref, pe_ref, w1_ref, w2_ref, out_ref):
    nc = a_ref.shape[0]
    a = a_ref[...]
    pe = pe_ref[...]
    h_top = jnp.dot((a + pe[0:1]).astype(BF16), w1_ref[0], preferred_element_type=F32)
    h_bot = jnp.dot((a + pe[1:2]).astype(BF16), w1_ref[1], preferred_element_type=F32)
    hid = _silu(h_top + pltpu.roll(h_bot, nc - 1, 0))
    out = jnp.dot(hid.astype(BF16), w2_ref[...], preferred_element_type=F32)
    keep = lax.broadcasted_iota(jnp.int32, out.shape, 0) < nc - 1
    out_ref[...] = jnp.where(keep, out, 0.0).astype(BF16)


def _cmp_kernel(ka_ref, va_ref, kpe_ref, kw1_ref, kw2_ref, vpe_ref, vw1_ref, vw2_ref, kc_ref, vc_ref):
    _cmp_one(ka_ref, kpe_ref, kw1_ref, kw2_ref, kc_ref)
    _cmp_one(va_ref, vpe_ref, vw1_ref, vw2_ref, vc_ref)


def _compress(ka, va, k_pe, k_w1, k_w2, v_pe, v_w1, v_w2):
    bg, nc, half = ka.shape
    blk = pl.BlockSpec((None, nc, half), lambda i: (i, 0, 0))
    full = lambda shape: pl.BlockSpec(shape, lambda i: (0,) * len(shape))
    prep = lambda pe, w1, w2: (pe.reshape(2, half), w1.reshape(2, half, CMP_HIDDEN).astype(BF16), w2.astype(BF16))
    out = pl.BlockSpec((None, nc, HEAD_DIM), lambda i: (i, 0, 0))
    wspecs = [full((2, half)), full((2, half, CMP_HIDDEN)), full((CMP_HIDDEN, HEAD_DIM))]
    return pl.pallas_call(
        _cmp_kernel,
        grid=(bg,),
        in_specs=[blk, blk] + wspecs + wspecs,
        out_specs=[out, out],
        out_shape=[jax.ShapeDtypeStruct((bg, nc, HEAD_DIM), BF16)] * 2,
        compiler_params=_params("parallel"),
        name="nsa_compress",
    )(ka, va, *prep(k_pe, k_w1, k_w2), *prep(v_pe, v_w1, v_w2))


def _dot_nt(a, b):
    return lax.dot_general(a, b, (((1,), (1,)), ((), ())), preferred_element_type=F32)


def _split3(x):
    hi = x.astype(BF16)
    r1 = x - hi.astype(F32)
    mid = r1.astype(BF16)
    lo = (r1 - mid.astype(F32)).astype(BF16)
    return hi, mid, lo


def _attn_kernel(qa_ref, ksa_ref, vsa_ref, kw_ref, vw_ref, kc_ref, vc_ref, gt_ref, ovt_ref,
                 o_ref, m_scr, acc_scr, qaug_scr, used_smem, *, seq, tq, tk):
    nb = seq // SEL_BLOCK
    nc = seq // CMP_STRIDE
    rows = HEADS_PER_GROUP * tq
    g = pl.program_id(0) % N_KV_GROUPS
    q0 = pl.program_id(1) * tq

    qa = qa_ref[...].reshape(rows, AUG_HALF)
    q2 = qa[:, :HEAD_DIM]
    row = lax.broadcasted_iota(jnp.int32, (rows, 1), 0)
    head = row // tq
    t_col = q0 + (row - head * tq)
    t_f = t_col.astype(F32)
    slope = jnp.exp2(-0.5 * (g * HEADS_PER_GROUP + head + 1).astype(F32))

    ci = lax.broadcasted_iota(jnp.int32, (1, nc), 1)
    center = (ci * CMP_STRIDE).astype(F32) + 0.5 * (CMP_BLOCK - 1)
    mask = (ci * CMP_STRIDE + (CMP_BLOCK - 1)) <= t_col
    s = _dot_nt(q2, kc_ref[...]) - slope * (t_f - center)
    s = jnp.where(mask, s, NEG_INF)
    p = jnp.where(mask, jnp.exp(s - jnp.max(s, axis=-1, keepdims=True)), 0.0)
    p = p / jnp.maximum(jnp.sum(p, axis=-1, keepdims=True), 1e-30)
    o_c = jnp.dot(p.astype(BF16), vc_ref[...], preferred_element_type=F32)

    wlen = WINDOW + tq
    w0 = pl.multiple_of(jnp.maximum(q0 - WINDOW, 0), tq)
    d = t_col - (w0 + lax.broadcasted_iota(jnp.int32, (1, wlen), 1))
    ok = (d >= 0) & (d < WINDOW)
    sw = jnp.where(ok, _dot_nt(q2, kw_ref[pl.ds(w0, wlen), :]) - slope * d.astype(F32), NEG_INF)
    pw = jnp.exp(sw - jnp.max(sw, axis=-1, keepdims=True))
    lw = jnp.maximum(jnp.sum(pw, axis=-1, keepdims=True), 1e-30)
    o_w = jnp.dot(pw.astype(BF16), vw_ref[pl.ds(w0, wlen), :], preferred_element_type=F32) / lw

    psum = p[0:tq] + p[tq:2 * tq] + p[2 * tq:3 * tq] + p[3 * tq:4 * tq]
    ovt = ovt_ref[...]
    imp = sum(_dot_nt(ovt, part) for part in _split3(psum))
    j = lax.broadcasted_iota(jnp.int32, (nb, tq), 0)
    jt = (q0 + lax.broadcasted_iota(jnp.int32, (nb, tq), 1)) // SEL_BLOCK
    forced = (j == 0) | (j == jt) | (j == jt - 1)
    vals = jnp.where(forced, FORCE_SCORE, jnp.where(j > jt, -FORCE_SCORE, imp))
    jf = j.astype(F32)
    sel_t = jnp.zeros((nb, tq), F32)
    for _ in range(min(SEL_TOPK, nb)):
        best = jnp.max(vals, axis=0, keepdims=True)
        first = jnp.min(jnp.where(vals == best, jf, float(nb)), axis=0, keepdims=True)
        hit = jf == first
        sel_t = jnp.where(hit, 1.0, sel_t)
        vals = jnp.where(hit, -3e38, vals)

    not_chosen = ((1.0 - sel_t) * MASK_BIAS).T.astype(BF16)
    qaug_scr[:, :AUG_HALF] = qa
    for r in range(HEADS_PER_GROUP):
        qaug_scr[r * tq:(r + 1) * tq, AUG_HALF:AUG_HALF + nb] = not_chosen
    if nb < AUG_HALF:
        qaug_scr[:, AUG_HALF + nb:] = jnp.zeros((rows, AUG_HALF - nb), BF16)
    row_l = lax.broadcasted_iota(jnp.int32, (rows, AUG_HALF), 0)
    head_l = row_l // tq
    t_l = q0 + (row_l - head_l * tq)
    slope_l = jnp.exp2(-0.5 * (g * HEADS_PER_GROUP + head_l + 1).astype(F32))
    ntile = tk // AUG_HALF

    def tile_scores(kt, causal):
        kb = pl.multiple_of(kt * tk, tk)
        sc = _dot_nt(qaug_scr[...], ksa_ref[pl.ds(kb, tk), :])
        if causal:
            key = kb + lax.broadcasted_iota(jnp.int32, (rows, tk), 1)
            sc = jnp.where(jnp.concatenate([t_l] * ntile, axis=1) >= key, sc, NEG_INF)
        return kb, sc, slope_l * (kb - t_l).astype(F32)

    def max_tile(kt, causal):
        _, sc, shift = tile_scores(kt, causal)
        part = sc[:, :AUG_HALF]
        for c in range(1, ntile):
            part = jnp.maximum(part, sc[:, c * AUG_HALF:(c + 1) * AUG_HALF])
        m_scr[...] = jnp.maximum(m_scr[...], part + shift)

    def acc_tile(kt, causal):
        kb, sc, shift = tile_scores(kt, causal)
        pr = jnp.exp(sc - jnp.concatenate([m_scr[...] - shift] * ntile, axis=1))
        acc_scr[...] += jnp.dot(pr.astype(BF16), vsa_ref[pl.ds(kb, tk), :], preferred_element_type=F32)

    blocks_per_tile = tk // SEL_BLOCK
    for i in range(seq // tk):
        chosen_here = jnp.max(sel_t[i * blocks_per_tile:(i + 1) * blocks_per_tile, :])
        used_smem[i] = (chosen_here > 0.5).astype(jnp.int32)

    def sweep(tile_fn):
        def step(kt, carry):
            @pl.when(used_smem[kt] > 0)
            def _():
                tile_fn(kt, False)
            return carry
        n_before = q0 // tk
        lax.fori_loop(0, n_before, step, 0)
        tile_fn(n_before, True)

    m_scr[...] = jnp.full((rows, AUG_HALF), NEG_INF, F32)
    sweep(max_tile)
    m_scr[...] = jnp.broadcast_to(jnp.max(m_scr[...], axis=-1, keepdims=True), (rows, AUG_HALF))
    acc_scr[...] = jnp.zeros((rows, AUG_HALF), F32)
    sweep(acc_tile)
    acc = acc_scr[...]
    o_s = acc[:, :HEAD_DIM] / jnp.maximum(acc[:, HEAD_DIM:HEAD_DIM + 1], 1e-30)

    gate = jax.nn.sigmoid(gt_ref[...])
    for r in range(HEADS_PER_GROUP):
        sl = slice(r * tq, (r + 1) * tq)
        gc = lambda x: gate[:, x * HEADS_PER_GROUP + r:x * HEADS_PER_GROUP + r + 1]
        o_ref[:, r * HEAD_DIM:(r + 1) * HEAD_DIM] = gc(0) * o_c[sl] + gc(1) * o_s[sl] + gc(2) * o_w[sl]


def _attention(qa, ksa, vs, kw, vw, kc, vc, gt, batch, seq):
    tq, tk = ATTN_TQ, ATTN_TK
    bg = batch * N_KV_GROUPS
    nq = seq // tq
    nb, nc = seq // SEL_BLOCK, seq // CMP_STRIDE
    assert nb <= AUG_HALF and seq % tk == 0 and tk % tq == 0
    rows = HEADS_PER_GROUP * tq
    gq = HEADS_PER_GROUP * HEAD_DIM
    cs = jnp.arange(nc)[None, :] * CMP_STRIDE
    ss = jnp.arange(nb)[:, None] * SEL_BLOCK
    ovt = ((cs <= ss + SEL_BLOCK - 1) & (cs + CMP_BLOCK - 1 >= ss)).astype(BF16)
    per_bg = lambda n, w: pl.BlockSpec((None, n, w), lambda i, j: (i, 0, 0))
    return pl.pallas_call(
        functools.partial(_attn_kernel, seq=seq, tq=tq, tk=tk),
        grid=(bg, nq),
        in_specs=[pl.BlockSpec((None, HEADS_PER_GROUP, tq, AUG_HALF), lambda i, j: (i, 0, j, 0)),
                  per_bg(seq, 2 * AUG_HALF), per_bg(seq, AUG_HALF), per_bg(seq, HEAD_DIM), per_bg(seq, HEAD_DIM),
                  per_bg(nc, HEAD_DIM), per_bg(nc, HEAD_DIM),
                  pl.BlockSpec((None, tq, 3 * HEADS_PER_GROUP), lambda i, j: (i, j, 0)),
                  pl.BlockSpec((nb, nc), lambda i, j: (0, 0))],
        out_specs=pl.BlockSpec((tq, gq), lambda i, j: ((i // N_KV_GROUPS) * nq + j, i % N_KV_GROUPS)),
        out_shape=jax.ShapeDtypeStruct((batch * seq, ATTN_WIDTH), F32),
        scratch_shapes=[pltpu.VMEM((rows, AUG_HALF), F32), pltpu.VMEM((rows, AUG_HALF), F32),
                        pltpu.VMEM((rows, 2 * AUG_HALF), BF16), pltpu.SMEM((seq // tk,), jnp.int32)],
        compiler_params=_params("parallel", "arbitrary"),
        name="nsa_attention",
    )(qa, ksa, vs, kw, vw, kc, vc, gt, ovt)


def _alibi_columns(batch, seq):
    gg, rr, dh = N_KV_GROUPS, HEADS_PER_GROUP, HEAD_DIM
    nb = seq // SEL_BLOCK
    head = jnp.arange(1, N_HEADS + 1, dtype=F32).reshape(gg, rr)
    parts = _split3(jnp.exp2(-8.0 * head / N_HEADS))
    qcols = jnp.stack([float(SEL_BLOCK) * p.astype(F32) for p in parts] + [p.astype(F32) for p in parts], axis=-1)
    qcols = jnp.pad(qcols, ((0, 0), (0, 0), (0, AUG_HALF - dh - qcols.shape[-1]))).astype(BF16)
    qcols = jnp.broadcast_to(qcols[None, :, :, None, :], (batch, gg, rr, seq, AUG_HALF - dh))
    pos = jnp.arange(seq) % ATTN_TK
    kcols = jnp.stack([pos // SEL_BLOCK] * 3 + [pos % SEL_BLOCK] * 3, axis=-1).astype(F32)
    kcols = jnp.pad(kcols, ((0, 0), (0, AUG_HALF - dh - kcols.shape[-1])))
    onehot = (jnp.arange(seq)[:, None] // SEL_BLOCK == jnp.arange(AUG_HALF)[None, :]) & (jnp.arange(AUG_HALF) < nb)
    kcols = jnp.concatenate([kcols, onehot.astype(F32)], axis=1).astype(BF16)
    kcols = jnp.broadcast_to(kcols[None], (batch * gg, seq, kcols.shape[1]))
    return qcols.reshape(batch * gg, rr, seq, AUG_HALF - dh), kcols


def _nsa_layer(h, batch, seq, norm, w_in, k_pe, k_w1, k_w2, v_pe, v_w1, v_w2, w_out):
    gg, rr, dh = N_KV_GROUPS, HEADS_PER_GROUP, HEAD_DIM
    q, kvc, kvsw, gates, z = _nsa_in(h, norm, w_in)
    qr = q.reshape(batch, seq, gg, rr, dh).transpose(0, 2, 3, 1, 4).reshape(batch * gg, rr, seq, dh)
    kv4 = kvsw.reshape(batch, seq, 4, gg, dh).transpose(2, 0, 3, 1, 4).reshape(4, batch * gg, seq, dh)
    qcols, kcols = _alibi_columns(batch, seq)
    qa = jnp.concatenate([qr, qcols], axis=-1)
    ksa = jnp.concatenate([kv4[0], kcols], axis=-1)
    ones_col = (jnp.arange(AUG_HALF - dh) == 0).astype(BF16)
    vsa = jnp.concatenate([kv4[1], jnp.broadcast_to(ones_col, (batch * gg, seq, AUG_HALF - dh))], axis=-1)
    half = CMP_STRIDE * dh
    kvc2 = (kvc.reshape(batch, seq // CMP_STRIDE, CMP_STRIDE, 2, gg, dh).transpose(3, 0, 4, 1, 2, 5)
            .reshape(2, batch * gg, seq // CMP_STRIDE, half))
    gt = (gates[:, :N_GATES].reshape(batch, seq, 3, gg, rr).transpose(0, 3, 1, 2, 4)
          .reshape(batch * gg, seq, 3 * rr))
    kc, vc = _compress(kvc2[0], kvc2[1], k_pe, k_w1, k_w2, v_pe, v_w1, v_w2)
    o = _attention(qa, ksa, vsa, kv4[2], kv4[3], kc, vc, gt, batch, seq)
    return _nsa_out(o, z, w_out, h)


def _split2(x):
    hi = x.astype(BF16)
    return hi, (x - hi.astype(F32)).astype(BF16)


def _dot3(a, b_hi, b_lo):
    a_hi, a_lo = _split2(a)
    return (jnp.dot(a_hi, b_hi, preferred_element_type=F32) + jnp.dot(a_lo, b_hi, preferred_element_type=F32)
            + jnp.dot(a_hi, b_lo, preferred_element_type=F32))


def _s5_state_kernel(u_ref, wh_ref, wl_ref, xr_ref, xi_ref):
    x = _dot3(u_ref[...], wh_ref[...], wl_ref[...])
    half = x.shape[1] // 2
    xr_ref[...] = x[:, :half]
    xi_ref[...] = x[:, half:]


def _s5_scan_kernel(xr_ref, xi_ref, ar_ref, ai_ref, cr_ref, ci_ref, *, batch, nchunk):
    ar, ai = ar_ref[...], ai_ref[...]
    width = ar.shape[1]

    def step(n, carry):
        new = []
        for b in range(batch):
            cr, ci = carry[2 * b], carry[2 * b + 1]
            idx = b * nchunk + n
            cr_ref[pl.ds(idx, 1), :] = cr
            ci_ref[pl.ds(idx, 1), :] = ci
            lr, li = xr_ref[pl.ds(idx, 1), :], xi_ref[pl.ds(idx, 1), :]
            new += [ar * cr - ai * ci + lr, ar * ci + ai * cr + li]
        return tuple(new)

    zero = jnp.zeros((1, width), F32)
    lax.fori_loop(0, nchunk, step, (zero,) * (2 * batch))


def _s5_y_kernel(u_ref, cr_ref, ci_ref, mh_ref, ml_ref, vh_ref, vl_ref, d_ref, y_ref):
    u = u_ref[...]
    carry = jnp.concatenate([cr_ref[...], ci_ref[...]], axis=1)
    y = _dot3(u, mh_ref[...], ml_ref[...]) + _dot3(carry, vh_ref[...], vl_ref[...]) + d_ref[...] * u
    y_ref[...] = jax.nn.gelu(y)


def _s5_matrices(log_dt, lambda_re, lambda_im, b_re, b_im, c_re, c_im, d_skip):
    hp = lax.Precision.HIGHEST
    gn, pn, cn, ln = SSM_GROUPS, SSM_STATE, SSM_GROUP, S5_CHUNK
    dt = jnp.exp(log_dt.astype(F32))[:, None]
    lre = jnp.minimum(lambda_re.astype(F32), -1e-4)
    lim = lambda_im.astype(F32)
    mag = jnp.exp(lre * dt)
    ab_re, ab_im = mag * jnp.cos(lim * dt), mag * jnp.sin(lim * dt)
    den = lre * lre + lim * lim
    nr = ab_re - 1.0
    coef_re = (nr * lre + ab_im * lim) / den
    coef_im = (ab_im * lre - nr * lim) / den
    br32, bi32 = b_re.astype(F32), b_im.astype(F32)
    bb_re = coef_re[..., None] * br32 - coef_im[..., None] * bi32
    bb_im = coef_re[..., None] * bi32 + coef_im[..., None] * br32
    cr32, ci32 = c_re.astype(F32), c_im.astype(F32)
    pr, pi = [jnp.ones_like(ab_re)], [jnp.zeros_like(ab_re)]
    for _ in range(ln):
        pr, pi = pr + [pr[-1] * ab_re - pi[-1] * ab_im], pi + [pr[-1] * ab_im + pi[-1] * ab_re]
    pw_re, pw_im = jnp.stack(pr), jnp.stack(pi)
    ab_b_re = pw_re[:ln, :, :, None] * bb_re - pw_im[:ln, :, :, None] * bb_im
    ab_b_im = pw_re[:ln, :, :, None] * bb_im + pw_im[:ln, :, :, None] * bb_re
    kern = (jnp.einsum('gop,kgpi->kgoi', cr32, ab_b_re, precision=hp)
            - jnp.einsum('gop,kgpi->kgoi', ci32, ab_b_im, precision=hp))
    s_idx, t_idx = jnp.arange(ln)[:, None], jnp.arange(ln)[None, :]
    lag = jnp.clip(t_idx - s_idx, 0, ln - 1)
    toep = jnp.where((t_idx >= s_idx)[:, :, None, None, None], kern[lag], 0.0)
    gp = gn // 2
    same = jnp.eye(2, dtype=F32)
    toep = toep.reshape(ln, ln, gp, 2, cn, cn)
    m_pair = toep[:, :, :, :, None] * same[None, None, None, :, :, None, None]
    m_pair = m_pair.transpose(2, 0, 3, 6, 1, 4, 5).reshape(gp, S5_PAIR, S5_PAIR)
    w = jnp.stack([ab_b_re[::-1], ab_b_im[::-1]]).reshape(2, ln, gp, 2, pn, cn)
    w_pair = w[:, :, :, :, None] * same[None, None, None, :, :, None, None]
    w_pair = w_pair.transpose(2, 1, 3, 6, 0, 4, 5).reshape(gp, S5_PAIR, 4 * pn)
    ar1, ai1 = pw_re[1:], pw_im[1:]
    v_re = cr32[None] * ar1[:, :, None, :] - ci32[None] * ai1[:, :, None, :]
    v_im = -(cr32[None] * ai1[:, :, None, :] + ci32[None] * ar1[:, :, None, :])
    v = jnp.stack([v_re, v_im]).reshape(2, ln, gp, 2, cn, pn)
    v_pair = v[:, :, :, :, None] * same[None, None, None, :, :, None, None]
    v_pair = v_pair.transpose(2, 0, 4, 6, 1, 3, 5).reshape(gp, 4 * pn, S5_PAIR)
    d_pair = jnp.tile(d_skip.astype(F32).reshape(gp, 1, 2 * cn), (1, 1, ln))
    a_end_re = pw_re[ln].reshape(1, gn * pn)
    a_end_im = pw_im[ln].reshape(1, gn * pn)
    return _split2(m_pair), _split2(w_pair), _split2(v_pair), d_pair, a_end_re, a_end_im


def _s5_layer(h, batch, seq, norm, w_in, log_dt, lambda_re, lambda_im, b_re, b_im, c_re, c_im, d_skip,
              w_glu, w_out, final_norm, final):
    gp, ln, cn = SSM_GROUPS // 2, S5_CHUNK, SSM_GROUP
    nchunk = seq // ln
    rows = batch * nchunk
    state_w = 2 * SSM_STATE
    (mh, ml), (wh, wl), (vh, vl), d_pair, a_re, a_im = _s5_matrices(
        log_dt, lambda_re, lambda_im, b_re, b_im, c_re, c_im, d_skip)
    up, z = _s5_in(h, norm, w_in)
    pair3 = lambda a, b: pl.BlockSpec((None, a, b), lambda i: (i, 0, 0))
    slab = pl.BlockSpec((rows, state_w), lambda i: (0, i))
    xr, xi = pl.pallas_call(
        _s5_state_kernel,
        grid=(gp,),
        in_specs=[pair3(rows, S5_PAIR), pair3(S5_PAIR, 2 * state_w), pair3(S5_PAIR, 2 * state_w)],
        out_specs=[slab, slab],
        out_shape=[jax.ShapeDtypeStruct((rows, gp * state_w), F32)] * 2,
        compiler_params=_params("parallel"),
        name="s5_chunk_state",
    )(up, wh, wl)
    scan_w = 4 * state_w
    wide = pl.BlockSpec((rows, scan_w), lambda i: (0, i))
    coef = pl.BlockSpec((1, scan_w), lambda i: (0, i))
    cr, ci = pl.pallas_call(
        functools.partial(_s5_scan_kernel, batch=batch, nchunk=nchunk),
        grid=(gp * state_w // scan_w,),
        in_specs=[wide, wide, coef, coef],
        out_specs=[wide, wide],
        out_shape=[jax.ShapeDtypeStruct((rows, gp * state_w), F32)] * 2,
        compiler_params=_params("parallel"),
        name="s5_carry_scan",
    )(xr, xi, a_re, a_im)
    yp = pl.pallas_call(
        _s5_y_kernel,
        grid=(gp,),
        in_specs=[pair3(rows, S5_PAIR), slab, slab, pair3(S5_PAIR, S5_PAIR), pair3(S5_PAIR, S5_PAIR),
                  pair3(2 * state_w, S5_PAIR), pair3(2 * state_w, S5_PAIR), pair3(1, S5_PAIR)],
        out_specs=pair3(rows, S5_PAIR),
        out_shape=jax.ShapeDtypeStruct((gp, rows, S5_PAIR), F32),
        compiler_params=_params("parallel"),
        name="s5_chunk_output",
    )(up, cr, ci, mh, ml, vh, vl, d_pair)
    return _s5_out(yp, z, w_glu, w_out, h, final_norm, final)


def kernel(x, l0_norm, l0_w_in, l0_cmp_k_pe, l0_cmp_k_w1, l0_cmp_k_w2, l0_cmp_v_pe, l0_cmp_v_w1, l0_cmp_v_w2, l0_w_out, l1_norm, l1_w_in, l1_log_dt, l1_lambda_re, l1_lambda_im, l1_b_re, l1_b_im, l1_c_re, l1_c_im, l1_d, l1_w_glu, l1_w_out, l2_norm, l2_w_in, l2_cmp_k_pe, l2_cmp_k_w1, l2_cmp_k_w2, l2_cmp_v_pe, l2_cmp_v_w1, l2_cmp_v_w2, l2_w_out, l3_norm, l3_w_in, l3_log_dt, l3_lambda_re, l3_lambda_im, l3_b_re, l3_b_im, l3_c_re, l3_c_im, l3_d, l3_w_glu, l3_w_out, final_norm):
    batch, seq, _ = x.shape
    h = x.reshape(batch * seq, D_MODEL)
    h = _nsa_layer(h, batch, seq, l0_norm, l0_w_in, l0_cmp_k_pe, l0_cmp_k_w1, l0_cmp_k_w2,
                   l0_cmp_v_pe, l0_cmp_v_w1, l0_cmp_v_w2, l0_w_out)
    h = _s5_layer(h, batch, seq, l1_norm, l1_w_in, l1_log_dt, l1_lambda_re, l1_lambda_im, l1_b_re, l1_b_im,
                  l1_c_re, l1_c_im, l1_d, l1_w_glu, l1_w_out, final_norm, False)
    h = _nsa_layer(h, batch, seq, l2_norm, l2_w_in, l2_cmp_k_pe, l2_cmp_k_w1, l2_cmp_k_w2,
                   l2_cmp_v_pe, l2_cmp_v_w1, l2_cmp_v_w2, l2_w_out)
    h = _s5_layer(h, batch, seq, l3_norm, l3_w_in, l3_log_dt, l3_lambda_re, l3_lambda_im, l3_b_re, l3_b_im,
                  l3_c_re, l3_c_im, l3_d, l3_w_glu, l3_w_out, final_norm, True)
    return h.reshape(batch, seq, D_MODEL)
```

```python
import functools

import jax
import jax.numpy as jnp
from jax import lax
from jax.experimental import pallas as pl
from jax.experimental.pallas import tpu as pltpu

F32 = jnp.float32
BF16 = jnp.bfloat16

D_MODEL = 1024
EPS = 1e-6
NEG_INF = -1e30
FORCE_SCORE = 1e9

N_HEADS = 16
HEAD_DIM = 64
N_KV_GROUPS = 4
HEADS_PER_GROUP = N_HEADS // N_KV_GROUPS
ATTN_WIDTH = N_HEADS * HEAD_DIM
KV_WIDTH = N_KV_GROUPS * HEAD_DIM
CMP_BLOCK = 32
CMP_STRIDE = 16
CMP_HIDDEN = 256
SEL_BLOCK = 64
SEL_TOPK = 16
WINDOW = 512
N_GATES = 3 * N_HEADS
GATES_PAD = 128

SSM_WIDTH = D_MODEL
SSM_GROUP = 16
SSM_GROUPS = SSM_WIDTH // SSM_GROUP
SSM_STATE = 64
S5_CHUNK = 16
S5_PAIRS = SSM_GROUPS // 2
S5_PAIR_LANES = 2 * SSM_GROUP
S5_PAIR = S5_CHUNK * S5_PAIR_LANES

ROW_TILE = 256
ATTN_TQ = 256
ATTN_TK = 512
LANES = 128
MASK_BIAS = -2.0 ** 100
VMEM_LIMIT = 56 * 1024 * 1024


def _params(*sem):
    return pltpu.CompilerParams(dimension_semantics=sem, vmem_limit_bytes=VMEM_LIMIT)


def _rms(x, g):
    return x * lax.rsqrt(jnp.mean(x * x, axis=-1, keepdims=True) + EPS) * g


def _silu(x):
    return x * jax.nn.sigmoid(x)


def _nsa_in_kernel(h_ref, g_ref, w_ref, q_ref, kvc_ref, kvsw_ref, gates_ref, z_ref):
    xn = _rms(h_ref[...], g_ref[...])
    y = jnp.dot(xn.astype(BF16), w_ref[...], preferred_element_type=F32)
    o = 0
    q_ref[...] = (y[:, o:o + ATTN_WIDTH] * (HEAD_DIM ** -0.5)).astype(BF16)
    o += ATTN_WIDTH
    kvc_ref[...] = y[:, o:o + 2 * KV_WIDTH]
    o += 2 * KV_WIDTH
    kvsw_ref[...] = y[:, o:o + 4 * KV_WIDTH].astype(BF16)
    o += 4 * KV_WIDTH
    z_ref[...] = y[:, o:o + ATTN_WIDTH]
    o += ATTN_WIDTH
    gates_ref[...] = y[:, o:o + GATES_PAD]


def _nsa_in(h, norm, w_in):
    t = h.shape[0]
    a, b = ATTN_WIDTH + 6 * KV_WIDTH, ATTN_WIDTH + 6 * KV_WIDTH + N_GATES
    w = jnp.concatenate([w_in[:, :a], w_in[:, b:], w_in[:, a:b],
                         jnp.zeros((D_MODEL, GATES_PAD - N_GATES), w_in.dtype)], axis=1).astype(BF16)
    n = w.shape[1]
    row = lambda width: pl.BlockSpec((ROW_TILE, width), lambda i: (i, 0))
    return pl.pallas_call(
        _nsa_in_kernel,
        grid=(t // ROW_TILE,),
        in_specs=[row(D_MODEL), pl.BlockSpec((1, D_MODEL), lambda i: (0, 0)),
                  pl.BlockSpec((D_MODEL, n), lambda i: (0, 0))],
        out_specs=[row(ATTN_WIDTH), row(2 * KV_WIDTH), row(4 * KV_WIDTH), row(GATES_PAD), row(ATTN_WIDTH)],
        out_shape=[jax.ShapeDtypeStruct((t, ATTN_WIDTH), BF16), jax.ShapeDtypeStruct((t, 2 * KV_WIDTH), F32),
                   jax.ShapeDtypeStruct((t, 4 * KV_WIDTH), BF16), jax.ShapeDtypeStruct((t, GATES_PAD), F32),
                   jax.ShapeDtypeStruct((t, ATTN_WIDTH), F32)],
        compiler_params=_params("parallel"),
        name="nsa_in_proj",
    )(h, norm.reshape(1, D_MODEL), w)


def _quarter_exchange(parts):
    quarter = lax.broadcasted_iota(jnp.int32, parts[0].shape, 1) // S5_PAIR_LANES
    outs = []
    for b in range(4):
        acc = None
        for a in range(4):
            moved = parts[a] if a == b else pltpu.roll(parts[a], ((a - b) % 4) * S5_PAIR_LANES, 1)
            acc = moved if acc is None else jnp.where(quarter == a, moved, acc)
        outs.append(acc)
    return outs


def _s5_in_kernel(h_ref, g_ref, w_ref, u_ref, z_ref, u_scr):
    xn = _rms(h_ref[...], g_ref[...])
    y = jnp.dot(xn.astype(BF16), w_ref[...], preferred_element_type=F32)
    z_ref[...] = y[:, SSM_WIDTH:]
    nch = ROW_TILE // S5_CHUNK
    for m in range(SSM_WIDTH // 128):
        u_scr[m] = y[:, m * 128:(m + 1) * 128]
        for k in range(S5_CHUNK // 4):
            steps = [u_scr[m, pl.ds(4 * k + a, nch, stride=S5_CHUNK), :] for a in range(4)]
            for b, tile in enumerate(_quarter_exchange(steps)):
                u_ref[4 * m + b, :, k * 128:(k + 1) * 128] = tile


def _s5_in(h, norm, w_in):
    t = h.shape[0]
    nch = ROW_TILE // S5_CHUNK
    row = lambda width: pl.BlockSpec((ROW_TILE, width), lambda i: (i, 0))
    return pl.pallas_call(
        _s5_in_kernel,
        grid=(t // ROW_TILE,),
        in_specs=[row(D_MODEL), pl.BlockSpec((1, D_MODEL), lambda i: (0, 0)),
                  pl.BlockSpec((D_MODEL, 2 * SSM_WIDTH), lambda i: (0, 0))],
        out_specs=[pl.BlockSpec((S5_PAIRS, nch, S5_PAIR), lambda i: (0, i, 0)), row(SSM_WIDTH)],
        out_shape=[jax.ShapeDtypeStruct((S5_PAIRS, t // S5_CHUNK, S5_PAIR), F32),
                   jax.ShapeDtypeStruct((t, SSM_WIDTH), F32)],
        scratch_shapes=[pltpu.VMEM((SSM_WIDTH // 128, ROW_TILE, 128), F32)],
        compiler_params=_params("parallel"),
        name="s5_in_proj",
    )(h, norm.reshape(1, D_MODEL), w_in.astype(BF16))


def _nsa_out_kernel(o_ref, z_ref, w_ref, res_ref, out_ref):
    a = o_ref[...] * _silu(z_ref[...])
    out_ref[...] = res_ref[...] + jnp.dot(a.astype(BF16), w_ref[...], preferred_element_type=F32)


def _nsa_out(o, z, w_out, res):
    t = o.shape[0]
    row = pl.BlockSpec((ROW_TILE, D_MODEL), lambda i: (i, 0))
    return pl.pallas_call(
        _nsa_out_kernel,
        grid=(t // ROW_TILE,),
        in_specs=[row, row, pl.BlockSpec((ATTN_WIDTH, D_MODEL), lambda i: (0, 0)), row],
        out_specs=row,
        out_shape=jax.ShapeDtypeStruct((t, D_MODEL), F32),
        compiler_params=_params("parallel"),
        name="nsa_out_proj",
    )(o, z, w_out.astype(BF16), res)


def _s5_out_kernel(y_ref, z_ref, wg_ref, wo_ref, res_ref, fn_ref, out_ref, y_scr, *, final):
    nch = ROW_TILE // S5_CHUNK
    for m in range(SSM_WIDTH // 128):
        for k in range(S5_CHUNK // 4):
            pairs = [y_ref[4 * m + b, :, k * 128:(k + 1) * 128] for b in range(4)]
            for a, tile in enumerate(_quarter_exchange(pairs)):
                y_scr[m, pl.ds(4 * k + a, nch, stride=S5_CHUNK), :] = tile
    y = jnp.concatenate([y_scr[m] for m in range(SSM_WIDTH // 128)], axis=1)
    gl = jnp.dot(y.astype(BF16), wg_ref[...], preferred_element_type=F32)
    v = gl[:, :SSM_WIDTH] * jax.nn.sigmoid(gl[:, SSM_WIDTH:])
    v = v * _silu(z_ref[...])
    h = res_ref[...] + jnp.dot(v.astype(BF16), wo_ref[...], preferred_element_type=F32)
    out_ref[...] = _rms(h, fn_ref[...]) if final else h


def _s5_out(y, z, w_glu, w_out, res, final_norm, final):
    t = z.shape[0]
    row = pl.BlockSpec((ROW_TILE, D_MODEL), lambda i: (i, 0))
    return pl.pallas_call(
        functools.partial(_s5_out_kernel, final=final),
        grid=(t // ROW_TILE,),
        in_specs=[pl.BlockSpec((S5_PAIRS, ROW_TILE // S5_CHUNK, S5_PAIR), lambda i: (0, i, 0)), row,
                  pl.BlockSpec((SSM_WIDTH, 2 * SSM_WIDTH), lambda i: (0, 0)),
                  pl.BlockSpec((SSM_WIDTH, D_MODEL), lambda i: (0, 0)), row,
                  pl.BlockSpec((1, D_MODEL), lambda i: (0, 0))],
        out_specs=row,
        out_shape=jax.ShapeDtypeStruct((t, D_MODEL), F32),
        scratch_shapes=[pltpu.VMEM((SSM_WIDTH // 128, ROW_TILE, 128), F32)],
        compiler_params=_params("parallel"),
        name="s5_glu_out_proj",
    )(y, z, w_glu.astype(BF16), w_out.astype(BF16), res, final_norm.reshape(1, D_MODEL))


def _cmp_one(a_ref, pe_ref, w1_ref, w2_ref, out_ref):
    nc = a_ref.shape[0]
    a = a_ref[...]
    pe = pe_ref[...]
    h_top = jnp.dot((a + pe[0:1]).astype(BF16), w1_ref[0], preferred_element_type=F32)
    h_bot = jnp.dot((a + pe[1:2]).astype(BF16), w1_ref[1], preferred_element_type=F32)
    hid = _silu(h_top + pltpu.roll(h_bot, nc - 1, 0))
    out = jnp.dot(hid.astype(BF16), w2_ref[...], preferred_element_type=F32)
    keep = lax.broadcasted_iota(jnp.int32, out.shape, 0) < nc - 1
    out_ref[...] = jnp.where(keep, out, 0.0).astype(BF16)


def _cmp_kernel(ka_ref, va_ref, kpe_ref, kw1_ref, kw2_ref, vpe_ref, vw1_ref, vw2_ref, kc_ref, vc_ref):
    _cmp_one(ka_ref, kpe_ref, kw1_ref, kw2_ref, kc_ref)
    _cmp_one(va_ref, vpe_ref, vw1_ref, vw2_ref, vc_ref)


def _compress(ka, va, k_pe, k_w1, k_w2, v_pe, v_w1, v_w2):
    bg, nc, half = ka.shape
    blk = pl.BlockSpec((None, nc, half), lambda i: (i, 0, 0))
    full = lambda shape: pl.BlockSpec(shape, lambda i: (0,) * len(shape))
    prep = lambda pe, w1, w2: (pe.reshape(2, half), w1.reshape(2, half, CMP_HIDDEN).astype(BF16), w2.astype(BF16))
    out = pl.BlockSpec((None, nc, HEAD_DIM), lambda i: (i, 0, 0))
    wspecs = [full((2, half)), full((2, half, CMP_HIDDEN)), full((CMP_HIDDEN, HEAD_DIM))]
    return pl.pallas_call(
        _cmp_kernel,
        grid=(bg,),
        in_specs=[blk, blk] + wspecs + wspecs,
        out_specs=[out, out],
        out_shape=[jax.ShapeDtypeStruct((bg, nc, HEAD_DIM), BF16)] * 2,
        compiler_params=_params("parallel"),
        name="nsa_compress",
    )(ka, va, *prep(k_pe, k_w1, k_w2), *prep(v_pe, v_w1, v_w2))


def _dot_nt(a, b):
    return lax.dot_general(a, b, (((1,), (1,)), ((), ())), preferred_element_type=F32)


def _split3(x):
    hi = x.astype(BF16)
    r1 = x - hi.astype(F32)
    mid = r1.astype(BF16)
    lo = (r1 - mid.astype(F32)).astype(BF16)
    return hi, mid, lo


def _softmax_values(parts, m_row, va):
    p = [jnp.exp(sc - m_row) for sc in parts]
    acc = jnp.dot(jnp.concatenate(p, axis=1).astype(BF16), va, preferred_element_type=F32)
    return acc, p


def _row_max(parts):
    m = parts[0]
    for sc in parts[1:]:
        m = jnp.maximum(m, sc)
    return jnp.broadcast_to(jnp.max(m, axis=-1, keepdims=True), m.shape)


def _attn_kernel(qa_ref, ksa_ref, vsa_ref, kwa_ref, vwa_ref, kca_ref, vca_ref, gt_ref, ovt_ref,
                 o_ref, m_scr, acc_scr, qaug_scr, used_smem, *, seq, tq, tk):
    nb = seq // SEL_BLOCK
    nc = seq // CMP_STRIDE
    rows = HEADS_PER_GROUP * tq
    q0 = pl.program_id(1) * tq

    qa = qa_ref[...].reshape(rows, LANES)
    row_l = lax.broadcasted_iota(jnp.int32, (rows, LANES), 0)
    lane = lax.broadcasted_iota(jnp.int32, (rows, LANES), 1)
    t_l = q0 + (row_l - (row_l // tq) * tq)

    sc_all = _dot_nt(qa, kca_ref[...])
    last_ok = (t_l - (CMP_BLOCK - 1)) >> 4
    parts = [jnp.where(lane + c * LANES <= last_ok, sc_all[:, c * LANES:(c + 1) * LANES], NEG_INF)
             for c in range(nc // LANES)]
    m_c = jnp.where(t_l >= CMP_BLOCK - 1, _row_max(parts), -NEG_INF)
    acc_c, p_c = _softmax_values(parts, m_c, vca_ref[...])
    inv_c = jnp.broadcast_to(1.0 / jnp.maximum(acc_c[:, HEAD_DIM:HEAD_DIM + 1], 1e-30), (rows, LANES))
    o_c = acc_c[:, :HEAD_DIM] * inv_c[:, :HEAD_DIM]
    p = jnp.concatenate([pc * inv_c for pc in p_c], axis=1)

    wlen = WINDOW + tq
    w0 = pl.multiple_of(jnp.maximum(q0 - WINDOW, 0), tq)
    sw_all = _dot_nt(qa, kwa_ref[pl.ds(w0, wlen), :])
    parts = []
    for c in range(wlen // LANES):
        key = w0 + c * LANES + lane
        ok = key <= t_l
        if c * LANES < tq:
            ok = ok & (key > t_l - WINDOW)
        parts.append(jnp.where(ok, sw_all[:, c * LANES:(c + 1) * LANES], NEG_INF))
    acc_w, _ = _softmax_values(parts, _row_max(parts), vwa_ref[pl.ds(w0, wlen), :])
    o_w = acc_w[:, :HEAD_DIM] / jnp.maximum(acc_w[:, HEAD_DIM:HEAD_DIM + 1], 1e-30)

    psum = p[0:tq] + p[tq:2 * tq] + p[2 * tq:3 * tq] + p[3 * tq:4 * tq]
    ovt = ovt_ref[...]
    imp = sum(_dot_nt(ovt, part) for part in _split3(psum))
    j = lax.broadcasted_iota(jnp.int32, (nb, tq), 0)
    jt = (q0 + lax.broadcasted_iota(jnp.int32, (nb, tq), 1)) // SEL_BLOCK
    forced = (j == 0) | (j == jt) | (j == jt - 1)
    sel_t = jnp.where(forced, 1.0, 0.0)
    vals = jnp.where(forced, -3e38, jnp.where(j > jt, -FORCE_SCORE, imp))
    jf = j.astype(F32)
    for _ in range(min(SEL_TOPK, nb) - 3):
        best = jnp.max(vals, axis=0, keepdims=True)
        first = jnp.min(jnp.where(vals == best, jf, float(nb)), axis=0, keepdims=True)
        hit = jf == first
        sel_t = jnp.where(hit, 1.0, sel_t)
        vals = jnp.where(hit, -3e38, vals)

    not_chosen = ((1.0 - sel_t) * MASK_BIAS).T.astype(BF16)
    qaug_scr[:, :LANES] = qa
    for r in range(HEADS_PER_GROUP):
        qaug_scr[r * tq:(r + 1) * tq, LANES:LANES + nb] = not_chosen
    if nb < LANES:
        qaug_scr[:, LANES + nb:] = jnp.zeros((rows, LANES - nb), BF16)
    ntile = tk // LANES

    def tile_scores(kt, causal):
        kb = pl.multiple_of(kt * tk, tk)
        sc = _dot_nt(qaug_scr[...], ksa_ref[pl.ds(kb, tk), :])
        if causal:
            key = kb + lax.broadcasted_iota(jnp.int32, (rows, tk), 1)
            sc = jnp.where(jnp.concatenate([t_l] * ntile, axis=1) >= key, sc, NEG_INF)
        return kb, sc

    def max_tile(kt, causal):
        _, sc = tile_scores(kt, causal)
        part = m_scr[...]
        for c in range(ntile):
            part = jnp.maximum(part, sc[:, c * LANES:(c + 1) * LANES])
        m_scr[...] = part

    def acc_tile(kt, causal):
        kb, sc = tile_scores(kt, causal)
        pr = jnp.exp(sc - jnp.concatenate([m_scr[...]] * ntile, axis=1))
        acc_scr[...] += jnp.dot(pr.astype(BF16), vsa_ref[pl.ds(kb, tk), :], preferred_element_type=F32)

    blocks_per_tile = tk // SEL_BLOCK
    for i in range(seq // tk):
        chosen_here = jnp.max(sel_t[i * blocks_per_tile:(i + 1) * blocks_per_tile, :])
        used_smem[i] = (chosen_here > 0.5).astype(jnp.int32)

    def sweep(tile_fn):
        def step(kt, carry):
            @pl.when(used_smem[kt] > 0)
            def _():
                tile_fn(kt, False)
            return carry
        n_before = q0 // tk
        lax.fori_loop(0, n_before, step, 0)
        tile_fn(n_before, True)

    m_scr[...] = jnp.full((rows, LANES), NEG_INF, F32)
    sweep(max_tile)
    m_scr[...] = jnp.broadcast_to(jnp.max(m_scr[...], axis=-1, keepdims=True), (rows, LANES))
    acc_scr[...] = jnp.zeros((rows, LANES), F32)
    sweep(acc_tile)
    acc = acc_scr[...]
    o_s = acc[:, :HEAD_DIM] / jnp.maximum(acc[:, HEAD_DIM:HEAD_DIM + 1], 1e-30)

    gate = jax.nn.sigmoid(gt_ref[...])
    for r in range(HEADS_PER_GROUP):
        sl = slice(r * tq, (r + 1) * tq)
        gc = lambda x: gate[:, x * HEADS_PER_GROUP + r:x * HEADS_PER_GROUP + r + 1]
        o_ref[:, r * HEAD_DIM:(r + 1) * HEAD_DIM] = gc(0) * o_c[sl] + gc(1) * o_s[sl] + gc(2) * o_w[sl]


def _attention(qa, ksa, vsa, kwa, vwa, kca, vca, gt, batch, seq):
    tq, tk = ATTN_TQ, ATTN_TK
    bg = batch * N_KV_GROUPS
    nq = seq // tq
    nb, nc = seq // SEL_BLOCK, seq // CMP_STRIDE
    assert SEL_TOPK <= nb <= LANES and seq % tk == 0 and tk % tq == 0 and tq % LANES == 0 and nc % LANES == 0
    rows = HEADS_PER_GROUP * tq
    gq = HEADS_PER_GROUP * HEAD_DIM
    cs = jnp.arange(nc)[None, :] * CMP_STRIDE
    ss = jnp.arange(nb)[:, None] * SEL_BLOCK
    ovt = ((cs <= ss + SEL_BLOCK - 1) & (cs + CMP_BLOCK - 1 >= ss)).astype(BF16)
    per_bg = lambda n, w: pl.BlockSpec((None, n, w), lambda i, j: (i, 0, 0))
    return pl.pallas_call(
        functools.partial(_attn_kernel, seq=seq, tq=tq, tk=tk),
        grid=(bg, nq),
        in_specs=[pl.BlockSpec((None, HEADS_PER_GROUP, tq, LANES), lambda i, j: (i, 0, j, 0)),
                  per_bg(seq, 2 * LANES), per_bg(seq, LANES), per_bg(seq, LANES), per_bg(seq, LANES),
                  per_bg(nc, LANES), per_bg(nc, LANES),
                  pl.BlockSpec((None, tq, 3 * HEADS_PER_GROUP), lambda i, j: (i, j, 0)),
                  pl.BlockSpec((nb, nc), lambda i, j: (0, 0))],
        out_specs=pl.BlockSpec((tq, gq), lambda i, j: ((i // N_KV_GROUPS) * nq + j, i % N_KV_GROUPS)),
        out_shape=jax.ShapeDtypeStruct((batch * seq, ATTN_WIDTH), F32),
        scratch_shapes=[pltpu.VMEM((rows, LANES), F32), pltpu.VMEM((rows, LANES), F32),
                        pltpu.VMEM((rows, 2 * LANES), BF16), pltpu.SMEM((seq // tk,), jnp.int32)],
        compiler_params=_params("parallel", "arbitrary"),
        name="nsa_attention",
    )(qa, ksa, vsa, kwa, vwa, kca, vca, gt, ovt)


def _alibi_columns(seq):
    dh = HEAD_DIM
    nb, nc = seq // SEL_BLOCK, seq // CMP_STRIDE
    head = jnp.arange(1, N_HEADS + 1, dtype=F32).reshape(N_KV_GROUPS, HEADS_PER_GROUP)
    parts = _split3(jnp.exp2(-8.0 * head / N_HEADS))
    qcols = jnp.stack([float(SEL_BLOCK) * p.astype(F32) for p in parts] + [p.astype(F32) for p in parts], axis=-1)
    qcols = jnp.pad(qcols, ((0, 0), (0, 0), (0, LANES - dh - qcols.shape[-1]))).astype(BF16)

    def key_cols(u, scale):
        cols = jnp.stack([u // SEL_BLOCK] * 3 + [u % SEL_BLOCK] * 3, axis=-1).astype(F32) * scale
        return jnp.pad(cols, ((0, 0), (0, LANES - dh - cols.shape[-1])))

    pos = jnp.arange(seq)
    onehot = (pos[:, None] // SEL_BLOCK == jnp.arange(LANES)[None, :]) & (jnp.arange(LANES) < nb)
    win_cols = key_cols(pos, 1.0).astype(BF16)
    sel_cols = jnp.concatenate([win_cols, onehot.astype(BF16)], axis=1)
    cmp_cols = key_cols(jnp.arange(nc), float(CMP_STRIDE)).astype(BF16)
    ones_cols = (jnp.arange(LANES - dh) == 0).astype(BF16)
    return qcols, sel_cols, win_cols, cmp_cols, ones_cols


def _nsa_layer(h, batch, seq, norm, w_in, k_pe, k_w1, k_w2, v_pe, v_w1, v_w2, w_out):
    gg, rr, dh = N_KV_GROUPS, HEADS_PER_GROUP, HEAD_DIM
    bg = batch * gg
    q, kvc, kvsw, gates, z = _nsa_in(h, norm, w_in)
    qr = q.reshape(batch, seq, gg, rr, dh).transpose(0, 2, 3, 1, 4)
    kv4 = kvsw.reshape(batch, seq, 4, gg, dh).transpose(2, 0, 3, 1, 4).reshape(4, bg, seq, dh)
    qcols, sel_cols, win_cols, cmp_cols, ones_cols = _alibi_columns(seq)
    with_cols = lambda x, cols: jnp.concatenate([x, jnp.broadcast_to(cols, x.shape[:-1] + cols.shape[-1:])], axis=-1)
    qa = with_cols(qr, qcols[None, :, :, None, :]).reshape(bg, rr, seq, LANES)
    ksa, vsa = with_cols(kv4[0], sel_cols), with_cols(kv4[1], ones_cols)
    kwa, vwa = with_cols(kv4[2], win_cols), with_cols(kv4[3], ones_cols)
    half = CMP_STRIDE * dh
    kvc2 = (kvc.reshape(batch, seq // CMP_STRIDE, CMP_STRIDE, 2, gg, dh).transpose(3, 0, 4, 1, 2, 5)
            .reshape(2, bg, seq // CMP_STRIDE, half))
    gt = (gates[:, :N_GATES].reshape(batch, seq, 3, gg, rr).transpose(0, 3, 1, 2, 4)
          .reshape(bg, seq, 3 * rr))
    kc, vc = _compress(kvc2[0], kvc2[1], k_pe, k_w1, k_w2, v_pe, v_w1, v_w2)
    o = _attention(qa, ksa, vsa, kwa, vwa, with_cols(kc, cmp_cols), with_cols(vc, ones_cols), gt, batch, seq)
    return _nsa_out(o, z, w_out, h)


def _split2(x):
    hi = x.astype(BF16)
    return hi, (x - hi.astype(F32)).astype(BF16)


def _dot3(a, b_hi, b_lo):
    a_hi, a_lo = _split2(a)
    return (jnp.dot(a_hi, b_hi, preferred_element_type=F32) + jnp.dot(a_lo, b_hi, preferred_element_type=F32)
            + jnp.dot(a_hi, b_lo, preferred_element_type=F32))


def _s5_state_kernel(u_ref, wh_ref, wl_ref, xr_ref, xi_ref):
    x = _dot3(u_ref[...], wh_ref[...], wl_ref[...])
    half = x.shape[1] // 2
    xr_ref[...] = x[:, :half]
    xi_ref[...] = x[:, half:]


def _s5_scan_kernel(xr_ref, xi_ref, ar_ref, ai_ref, cr_ref, ci_ref, *, batch, nchunk):
    ar, ai = ar_ref[...], ai_ref[...]
    width = ar.shape[1]

    def step(n, carry):
        new = []
        for b in range(batch):
            cr, ci = carry[2 * b], carry[2 * b + 1]
            idx = b * nchunk + n
            cr_ref[pl.ds(idx, 1), :] = cr
            ci_ref[pl.ds(idx, 1), :] = ci
            lr, li = xr_ref[pl.ds(idx, 1), :], xi_ref[pl.ds(idx, 1), :]
            new += [ar * cr - ai * ci + lr, ar * ci + ai * cr + li]
        return tuple(new)

    zero = jnp.zeros((1, width), F32)
    lax.fori_loop(0, nchunk, step, (zero,) * (2 * batch))


def _s5_y_kernel(u_ref, cr_ref, ci_ref, mh_ref, ml_ref, vh_ref, vl_ref, d_ref, y_ref):
    u = u_ref[...]
    carry = jnp.concatenate([cr_ref[...], ci_ref[...]], axis=1)
    y = _dot3(u, mh_ref[...], ml_ref[...]) + _dot3(carry, vh_ref[...], vl_ref[...]) + d_ref[...] * u
    y_ref[...] = jax.nn.gelu(y)


def _s5_matrices(log_dt, lambda_re, lambda_im, b_re, b_im, c_re, c_im, d_skip):
    hp = lax.Precision.HIGHEST
    gn, pn, cn, ln = SSM_GROUPS, SSM_STATE, SSM_GROUP, S5_CHUNK
    dt = jnp.exp(log_dt.astype(F32))[:, None]
    lre = jnp.minimum(lambda_re.astype(F32), -1e-4)
    lim = lambda_im.astype(F32)
    mag = jnp.exp(lre * dt)
    ab_re, ab_im = mag * jnp.cos(lim * dt), mag * jnp.sin(lim * dt)
    den = lre * lre + lim * lim
    nr = ab_re - 1.0
    coef_re = (nr * lre + ab_im * lim) / den
    coef_im = (ab_im * lre - nr * lim) / den
    br32, bi32 = b_re.astype(F32), b_im.astype(F32)
    bb_re = coef_re[..., None] * br32 - coef_im[..., None] * bi32
    bb_im = coef_re[..., None] * bi32 + coef_im[..., None] * br32
    cr32, ci32 = c_re.astype(F32), c_im.astype(F32)
    pr, pi = [jnp.ones_like(ab_re)], [jnp.zeros_like(ab_re)]
    for _ in range(ln):
        pr, pi = pr + [pr[-1] * ab_re - pi[-1] * ab_im], pi + [pr[-1] * ab_im + pi[-1] * ab_re]
    pw_re, pw_im = jnp.stack(pr), jnp.stack(pi)
    ab_b_re = pw_re[:ln, :, :, None] * bb_re - pw_im[:ln, :, :, None] * bb_im
    ab_b_im = pw_re[:ln, :, :, None] * bb_im + pw_im[:ln, :, :, None] * bb_re
    kern = (jnp.einsum('gop,kgpi->kgoi', cr32, ab_b_re, precision=hp)
            - jnp.einsum('gop,kgpi->kgoi', ci32, ab_b_im, precision=hp))
    s_idx, t_idx = jnp.arange(ln)[:, None], jnp.arange(ln)[None, :]
    lag = jnp.clip(t_idx - s_idx, 0, ln - 1)
    toep = jnp.where((t_idx >= s_idx)[:, :, None, None, None], kern[lag], 0.0)
    gp = gn // 2
    same = jnp.eye(2, dtype=F32)
    toep = toep.reshape(ln, ln, gp, 2, cn, cn)
    m_pair = toep[:, :, :, :, None] * same[None, None, None, :, :, None, None]
    m_pair = m_pair.transpose(2, 0, 3, 6, 1, 4, 5).reshape(gp, S5_PAIR, S5_PAIR)
    w = jnp.stack([ab_b_re[::-1], ab_b_im[::-1]]).reshape(2, ln, gp, 2, pn, cn)
    w_pair = w[:, :, :, :, None] * same[None, None, None, :, :, None, None]
    w_pair = w_pair.transpose(2, 1, 3, 6, 0, 4, 5).reshape(gp, S5_PAIR, 4 * pn)
    ar1, ai1 = pw_re[1:], pw_im[1:]
    v_re = cr32[None] * ar1[:, :, None, :] - ci32[None] * ai1[:, :, None, :]
    v_im = -(cr32[None] * ai1[:, :, None, :] + ci32[None] * ar1[:, :, None, :])
    v = jnp.stack([v_re, v_im]).reshape(2, ln, gp, 2, cn, pn)
    v_pair = v[:, :, :, :, None] * same[None, None, None, :, :, None, None]
    v_pair = v_pair.transpose(2, 0, 4, 6, 1, 3, 5).reshape(gp, 4 * pn, S5_PAIR)
    d_pair = jnp.tile(d_skip.astype(F32).reshape(gp, 1, 2 * cn), (1, 1, ln))
    a_end_re = pw_re[ln].reshape(1, gn * pn)
    a_end_im = pw_im[ln].reshape(1, gn * pn)
    return _split2(m_pair), _split2(w_pair), _split2(v_pair), d_pair, a_end_re, a_end_im


def _s5_layer(h, batch, seq, norm, w_in, log_dt, lambda_re, lambda_im, b_re, b_im, c_re, c_im, d_skip,
              w_glu, w_out, final_norm, final):
    gp, ln, cn = SSM_GROUPS // 2, S5_CHUNK, SSM_GROUP
    nchunk = seq // ln
    rows = batch * nchunk
    state_w = 2 * SSM_STATE
    (mh, ml), (wh, wl), (vh, vl), d_pair, a_re, a_im = _s5_matrices(
        log_dt, lambda_re, lambda_im, b_re, b_im, c_re, c_im, d_skip)
    up, z = _s5_in(h, norm, w_in)
    pair3 = lambda a, b: pl.BlockSpec((None, a, b), lambda i: (i, 0, 0))
    slab = pl.BlockSpec((rows, state_w), lambda i: (0, i))
    xr, xi = pl.pallas_call(
        _s5_state_kernel,
        grid=(gp,),
        in_specs=[pair3(rows, S5_PAIR), pair3(S5_PAIR, 2 * state_w), pair3(S5_PAIR, 2 * state_w)],
        out_specs=[slab, slab],
        out_shape=[jax.ShapeDtypeStruct((rows, gp * state_w), F32)] * 2,
        compiler_params=_params("parallel"),
        name="s5_chunk_state",
    )(up, wh, wl)
    scan_w = 4 * state_w
    wide = pl.BlockSpec((rows, scan_w), lambda i: (0, i))
    coef = pl.BlockSpec((1, scan_w), lambda i: (0, i))
    cr, ci = pl.pallas_call(
        functools.partial(_s5_scan_kernel, batch=batch, nchunk=nchunk),
        grid=(gp * state_w // scan_w,),
        in_specs=[wide, wide, coef, coef],
        out_specs=[wide, wide],
        out_shape=[jax.ShapeDtypeStruct((rows, gp * state_w), F32)] * 2,
        compiler_params=_params("parallel"),
        name="s5_carry_scan",
    )(xr, xi, a_re, a_im)
    yp = pl.pallas_call(
        _s5_y_kernel,
        grid=(gp,),
        in_specs=[pair3(rows, S5_PAIR), slab, slab, pair3(S5_PAIR, S5_PAIR), pair3(S5_PAIR, S5_PAIR),
                  pair3(2 * state_w, S5_PAIR), pair3(2 * state_w, S5_PAIR), pair3(1, S5_PAIR)],
        out_specs=pair3(rows, S5_PAIR),
        out_shape=jax.ShapeDtypeStruct((gp, rows, S5_PAIR), F32),
        compiler_params=_params("parallel"),
        name="s5_chunk_output",
    )(up, cr, ci, mh, ml, vh, vl, d_pair)
    return _s5_out(yp, z, w_glu, w_out, h, final_norm, final)


def kernel(x, l0_norm, l0_w_in, l0_cmp_k_pe, l0_cmp_k_w1, l0_cmp_k_w2, l0_cmp_v_pe, l0_cmp_v_w1, l0_cmp_v_w2, l0_w_out, l1_norm, l1_w_in, l1_log_dt, l1_lambda_re, l1_lambda_im, l1_b_re, l1_b_im, l1_c_re, l1_c_im, l1_d, l1_w_glu, l1_w_out, l2_norm, l2_w_in, l2_cmp_k_pe, l2_cmp_k_w1, l2_cmp_k_w2, l2_cmp_v_pe, l2_cmp_v_w1, l2_cmp_v_w2, l2_w_out, l3_norm, l3_w_in, l3_log_dt, l3_lambda_re, l3_lambda_im, l3_b_re, l3_b_im, l3_c_re, l3_c_im, l3_d, l3_w_glu, l3_w_out, final_norm):
    batch, seq, _ = x.shape
    h = x.reshape(batch * seq, D_MODEL)
    h = _nsa_layer(h, batch, seq, l0_norm, l0_w_in, l0_cmp_k_pe, l0_cmp_k_w1, l0_cmp_k_w2,
                   l0_cmp_v_pe, l0_cmp_v_w1, l0_cmp_v_w2, l0_w_out)
    h = _s5_layer(h, batch, seq, l1_norm, l1_w_in, l1_log_dt, l1_lambda_re, l1_lambda_im, l1_b_re, l1_b_im,
                  l1_c_re, l1_c_im, l1_d, l1_w_glu, l1_w_out, final_norm, False)
    h = _nsa_layer(h, batch, seq, l2_norm, l2_w_in, l2_cmp_k_pe, l2_cmp_k_w1, l2_cmp_k_w2,
                   l2_cmp_v_pe, l2_cmp_v_w1, l2_cmp_v_w2, l2_w_out)
    h = _s5_layer(h, batch, seq, l3_norm, l3_w_in, l3_log_dt, l3_lambda_re, l3_lambda_im, l3_b_re, l3_b_im,
                  l3_c_re, l3_c_im, l3_d, l3_w_glu, l3_w_out, final_norm, True)
    return h.reshape(batch, seq, D_MODEL)
```

```python
import functools

import jax
import jax.numpy as jnp
from jax import lax
from jax.experimental import pallas as pl
from jax.experimental.pallas import tpu as pltpu

F32 = jnp.float32
BF16 = jnp.bfloat16

D_MODEL = 1024
EPS = 1e-6
NEG_INF = -1e30
FORCE_SCORE = 1e9

N_HEADS = 16
HEAD_DIM = 64
N_KV_GROUPS = 4
HEADS_PER_GROUP = N_HEADS // N_KV_GROUPS
ATTN_WIDTH = N_HEADS * HEAD_DIM
KV_WIDTH = N_KV_GROUPS * HEAD_DIM
CMP_BLOCK = 32
CMP_STRIDE = 16
CMP_HIDDEN = 256
SEL_BLOCK = 64
SEL_TOPK = 16
WINDOW = 512
N_GATES = 3 * N_HEADS
GATES_PAD = 128

SSM_WIDTH = D_MODEL
SSM_GROUP = 16
SSM_GROUPS = SSM_WIDTH // SSM_GROUP
SSM_STATE = 64
S5_CHUNK = 16
S5_PAIRS = SSM_GROUPS // 2
S5_PAIR_LANES = 2 * SSM_GROUP
S5_PAIR = S5_CHUNK * S5_PAIR_LANES

ROW_TILE = 256
ATTN_TQ = 256
ATTN_TK = 512
LANES = 128
MASK_BIAS = -2.0 ** 100
VMEM_LIMIT = 56 * 1024 * 1024


def _params(*sem):
    return pltpu.CompilerParams(dimension_semantics=sem, vmem_limit_bytes=VMEM_LIMIT)


def _rms(x, g):
    return x * lax.rsqrt(jnp.mean(x * x, axis=-1, keepdims=True) + EPS) * g


def _silu(x):
    return x * jax.nn.sigmoid(x)


def _nsa_in_kernel(h_ref, g_ref, w_ref, qcols_ref, selc_ref, winc_ref, ones_ref,
                   qa_ref, ksa_ref, vsa_ref, kwa_ref, vwa_ref, kvc_ref, gt_ref, z_ref, c_scr):
    gg, rr, dh = N_KV_GROUPS, HEADS_PER_GROUP, HEAD_DIM
    xn = _rms(h_ref[...], g_ref[...])
    y = jnp.dot(xn.astype(BF16), w_ref[...], preferred_element_type=F32)
    for g in range(gg):
        for r in range(rr):
            k = g * rr + r
            qa_ref[g, r] = jnp.broadcast_to(qcols_ref[g, r:r + 1, :], (ROW_TILE, LANES))
            qa_ref[g, r, :, :dh] = (y[:, k * dh:(k + 1) * dh] * (dh ** -0.5)).astype(BF16)
    o = ATTN_WIDTH + 2 * KV_WIDTH
    consts = (selc_ref[...], jnp.broadcast_to(ones_ref[...], (ROW_TILE, LANES)), winc_ref[...],
              jnp.broadcast_to(ones_ref[...], (ROW_TILE, LANES)))
    for kind, ref in enumerate((ksa_ref, vsa_ref, kwa_ref, vwa_ref)):
        for g in range(gg):
            c = o + (kind * gg + g) * dh
            ref[g] = consts[kind]
            ref[g, :, :dh] = y[:, c:c + dh].astype(BF16)
    lane = lax.broadcasted_iota(jnp.int32, (ROW_TILE // CMP_STRIDE, LANES), 1)
    for n in range(2 * KV_WIDTH // LANES):
        c_scr[n] = y[:, ATTN_WIDTH + n * LANES:ATTN_WIDTH + (n + 1) * LANES]
        kind, g0 = divmod(2 * n, gg)
        for a in range(CMP_STRIDE // 2):
            even = c_scr[n, pl.ds(2 * a, ROW_TILE // CMP_STRIDE, stride=CMP_STRIDE), :]
            odd = c_scr[n, pl.ds(2 * a + 1, ROW_TILE // CMP_STRIDE, stride=CMP_STRIDE), :]
            kvc_ref[kind, g0, :, a * LANES:(a + 1) * LANES] = jnp.where(lane < dh, even, pltpu.roll(odd, dh, 1))
            kvc_ref[kind, g0 + 1, :, a * LANES:(a + 1) * LANES] = jnp.where(lane < dh, pltpu.roll(even, dh, 1), odd)
    o += 4 * KV_WIDTH
    z_ref[...] = y[:, o:o + ATTN_WIDTH]
    o += ATTN_WIDTH
    per_group = 3 * rr
    for g in range(gg):
        gt_ref[g] = y[:, o + g * per_group:o + (g + 1) * per_group]


def _nsa_in(h, norm, w_in, batch, seq):
    gg, rr, dh = N_KV_GROUPS, HEADS_PER_GROUP, HEAD_DIM
    nt = seq // ROW_TILE
    nc_tile = ROW_TILE // CMP_STRIDE
    a, b = ATTN_WIDTH + 6 * KV_WIDTH, ATTN_WIDTH + 6 * KV_WIDTH + N_GATES
    w_gates = w_in[:, a:b].reshape(D_MODEL, 3, gg, rr).transpose(0, 2, 1, 3).reshape(D_MODEL, N_GATES)
    w = jnp.concatenate([w_in[:, :a], w_in[:, b:], w_gates,
                         jnp.zeros((D_MODEL, GATES_PAD - N_GATES), w_in.dtype)], axis=1).astype(BF16)
    n = w.shape[1]
    qcols, sel_cols, win_cols, cmp_cols, ones_cols = _alibi_columns(seq)
    lead = lambda cols: jnp.pad(cols, [(0, 0)] * (cols.ndim - 1) + [(dh, 0)])
    const = lambda shape: pl.BlockSpec(shape, lambda i: (0,) * len(shape))
    by_pos = lambda width: pl.BlockSpec((ROW_TILE, width), lambda i: (i % nt, 0))
    per_b = lambda *tail: pl.BlockSpec((None, gg) + tail, lambda i: (i // nt, 0) + (0,) * (len(tail) - 2) + (i % nt, 0))
    outs = pl.pallas_call(
        _nsa_in_kernel,
        grid=(batch * nt,),
        in_specs=[pl.BlockSpec((ROW_TILE, D_MODEL), lambda i: (i, 0)), const((1, D_MODEL)), const((D_MODEL, n)),
                  const((gg, rr, LANES)), by_pos(2 * LANES), by_pos(LANES), const((1, LANES))],
        out_specs=[per_b(rr, ROW_TILE, LANES), per_b(ROW_TILE, 2 * LANES), per_b(ROW_TILE, LANES),
                   per_b(ROW_TILE, LANES), per_b(ROW_TILE, LANES),
                   pl.BlockSpec((2, None, gg, nc_tile, CMP_STRIDE * dh), lambda i: (0, i // nt, 0, i % nt, 0)),
                   per_b(ROW_TILE, 3 * rr), pl.BlockSpec((ROW_TILE, ATTN_WIDTH), lambda i: (i, 0))],
        out_shape=[jax.ShapeDtypeStruct((batch, gg, rr, seq, LANES), BF16),
                   jax.ShapeDtypeStruct((batch, gg, seq, 2 * LANES), BF16),
                   jax.ShapeDtypeStruct((batch, gg, seq, LANES), BF16),
                   jax.ShapeDtypeStruct((batch, gg, seq, LANES), BF16),
                   jax.ShapeDtypeStruct((batch, gg, seq, LANES), BF16),
                   jax.ShapeDtypeStruct((2, batch, gg, seq // CMP_STRIDE, CMP_STRIDE * dh), F32),
                   jax.ShapeDtypeStruct((batch, gg, seq, 3 * rr), F32),
                   jax.ShapeDtypeStruct((batch * seq, ATTN_WIDTH), F32)],
        scratch_shapes=[pltpu.VMEM((2 * KV_WIDTH // LANES, ROW_TILE, LANES), F32)],
        compiler_params=_params("parallel"),
        name="nsa_in_proj",
    )(h, norm.reshape(1, D_MODEL), w, lead(qcols), lead(sel_cols), lead(win_cols), lead(ones_cols)[None])
    return outs, cmp_cols, ones_cols


def _quarter_exchange(parts):
    quarter = lax.broadcasted_iota(jnp.int32, parts[0].shape, 1) // S5_PAIR_LANES
    outs = []
    for b in range(4):
        acc = None
        for a in range(4):
            moved = parts[a] if a == b else pltpu.roll(parts[a], ((a - b) % 4) * S5_PAIR_LANES, 1)
            acc = moved if acc is None else jnp.where(quarter == a, moved, acc)
        outs.append(acc)
    return outs


def _s5_in_kernel(h_ref, g_ref, w_ref, u_ref, z_ref, u_scr):
    xn = _rms(h_ref[...], g_ref[...])
    y = jnp.dot(xn.astype(BF16), w_ref[...], preferred_element_type=F32)
    z_ref[...] = y[:, SSM_WIDTH:]
    nch = ROW_TILE // S5_CHUNK
    for m in range(SSM_WIDTH // 128):
        u_scr[m] = y[:, m * 128:(m + 1) * 128]
        for k in range(S5_CHUNK // 4):
            steps = [u_scr[m, pl.ds(4 * k + a, nch, stride=S5_CHUNK), :] for a in range(4)]
            for b, tile in enumerate(_quarter_exchange(steps)):
                u_ref[4 * m + b, :, k * 128:(k + 1) * 128] = tile


def _s5_in(h, norm, w_in):
    t = h.shape[0]
    nch = ROW_TILE // S5_CHUNK
    row = lambda width: pl.BlockSpec((ROW_TILE, width), lambda i: (i, 0))
    return pl.pallas_call(
        _s5_in_kernel,
        grid=(t // ROW_TILE,),
        in_specs=[row(D_MODEL), pl.BlockSpec((1, D_MODEL), lambda i: (0, 0)),
                  pl.BlockSpec((D_MODEL, 2 * SSM_WIDTH), lambda i: (0, 0))],
        out_specs=[pl.BlockSpec((S5_PAIRS, nch, S5_PAIR), lambda i: (0, i, 0)), row(SSM_WIDTH)],
        out_shape=[jax.ShapeDtypeStruct((S5_PAIRS, t // S5_CHUNK, S5_PAIR), F32),
                   jax.ShapeDtypeStruct((t, SSM_WIDTH), F32)],
        scratch_shapes=[pltpu.VMEM((SSM_WIDTH // 128, ROW_TILE, 128), F32)],
        compiler_params=_params("parallel"),
        name="s5_in_proj",
    )(h, norm.reshape(1, D_MODEL), w_in.astype(BF16))


def _nsa_out_kernel(o_ref, z_ref, w_ref, res_ref, out_ref):
    a = o_ref[...] * _silu(z_ref[...])
    out_ref[...] = res_ref[...] + jnp.dot(a.astype(BF16), w_ref[...], preferred_element_type=F32)


def _nsa_out(o, z, w_out, res):
    t = o.shape[0]
    row = pl.BlockSpec((ROW_TILE, D_MODEL), lambda i: (i, 0))
    return pl.pallas_call(
        _nsa_out_kernel,
        grid=(t // ROW_TILE,),
        in_specs=[row, row, pl.BlockSpec((ATTN_WIDTH, D_MODEL), lambda i: (0, 0)), row],
        out_specs=row,
        out_shape=jax.ShapeDtypeStruct((t, D_MODEL), F32),
        compiler_params=_params("parallel"),
        name="nsa_out_proj",
    )(o, z, w_out.astype(BF16), res)


def _s5_out_kernel(y_ref, z_ref, wg_ref, wo_ref, res_ref, fn_ref, out_ref, y_scr, *, final):
    nch = ROW_TILE // S5_CHUNK
    for m in range(SSM_WIDTH // 128):
        for k in range(S5_CHUNK // 4):
            pairs = [y_ref[4 * m + b, :, k * 128:(k + 1) * 128] for b in range(4)]
            for a, tile in enumerate(_quarter_exchange(pairs)):
                y_scr[m, pl.ds(4 * k + a, nch, stride=S5_CHUNK), :] = tile
    y = jnp.concatenate([y_scr[m] for m in range(SSM_WIDTH // 128)], axis=1)
    gl = jnp.dot(y.astype(BF16), wg_ref[...], preferred_element_type=F32)
    v = gl[:, :SSM_WIDTH] * jax.nn.sigmoid(gl[:, SSM_WIDTH:])
    v = v * _silu(z_ref[...])
    h = res_ref[...] + jnp.dot(v.astype(BF16), wo_ref[...], preferred_element_type=F32)
    out_ref[...] = _rms(h, fn_ref[...]) if final else h


def _s5_out(y, z, w_glu, w_out, res, final_norm, final):
    t = z.shape[0]
    row = pl.BlockSpec((ROW_TILE, D_MODEL), lambda i: (i, 0))
    return pl.pallas_call(
        functools.partial(_s5_out_kernel, final=final),
        grid=(t // ROW_TILE,),
        in_specs=[pl.BlockSpec((S5_PAIRS, ROW_TILE // S5_CHUNK, S5_PAIR), lambda i: (0, i, 0)), row,
                  pl.BlockSpec((SSM_WIDTH, 2 * SSM_WIDTH), lambda i: (0, 0)),
                  pl.BlockSpec((SSM_WIDTH, D_MODEL), lambda i: (0, 0)), row,
                  pl.BlockSpec((1, D_MODEL), lambda i: (0, 0))],
        out_specs=row,
        out_shape=jax.ShapeDtypeStruct((t, D_MODEL), F32),
        scratch_shapes=[pltpu.VMEM((SSM_WIDTH // 128, ROW_TILE, 128), F32)],
        compiler_params=_params("parallel"),
        name="s5_glu_out_proj",
    )(y, z, w_glu.astype(BF16), w_out.astype(BF16), res, final_norm.reshape(1, D_MODEL))


def _cmp_one(a_ref, pe_ref, w1_ref, w2_ref, out_ref):
    nc = a_ref.shape[0]
    a = a_ref[...]
    pe = pe_ref[...]
    h_top = jnp.dot((a + pe[0:1]).astype(BF16), w1_ref[0], preferred_element_type=F32)
    h_bot = jnp.dot((a + pe[1:2]).astype(BF16), w1_ref[1], preferred_element_type=F32)
    hid = _silu(h_top + pltpu.roll(h_bot, nc - 1, 0))
    out = jnp.dot(hid.astype(BF16), w2_ref[...], preferred_element_type=F32)
    keep = lax.broadcasted_iota(jnp.int32, out.shape, 0) < nc - 1
    out_ref[...] = jnp.where(keep, out, 0.0).astype(BF16)


def _cmp_kernel(ka_ref, va_ref, kpe_ref, kw1_ref, kw2_ref, vpe_ref, vw1_ref, vw2_ref, kc_ref, vc_ref):
    _cmp_one(ka_ref, kpe_ref, kw1_ref, kw2_ref, kc_ref)
    _cmp_one(va_ref, vpe_ref, vw1_ref, vw2_ref, vc_ref)


def _compress(ka, va, k_pe, k_w1, k_w2, v_pe, v_w1, v_w2):
    bg, nc, half = ka.shape
    blk = pl.BlockSpec((None, nc, half), lambda i: (i, 0, 0))
    full = lambda shape: pl.BlockSpec(shape, lambda i: (0,) * len(shape))
    prep = lambda pe, w1, w2: (pe.reshape(2, half), w1.reshape(2, half, CMP_HIDDEN).astype(BF16), w2.astype(BF16))
    out = pl.BlockSpec((None, nc, HEAD_DIM), lambda i: (i, 0, 0))
    wspecs = [full((2, half)), full((2, half, CMP_HIDDEN)), full((CMP_HIDDEN, HEAD_DIM))]
    return pl.pallas_call(
        _cmp_kernel,
        grid=(bg,),
        in_specs=[blk, blk] + wspecs + wspecs,
        out_specs=[out, out],
        out_shape=[jax.ShapeDtypeStruct((bg, nc, HEAD_DIM), BF16)] * 2,
        compiler_params=_params("parallel"),
        name="nsa_compress",
    )(ka, va, *prep(k_pe, k_w1, k_w2), *prep(v_pe, v_w1, v_w2))


def _dot_nt(a, b):
    return lax.dot_general(a, b, (((1,), (1,)), ((), ())), preferred_element_type=F32)


def _split3(x):
    hi = x.astype(BF16)
    r1 = x - hi.astype(F32)
    mid = r1.astype(BF16)
    lo = (r1 - mid.astype(F32)).astype(BF16)
    return hi, mid, lo


def _softmax_values(parts, m_row, va):
    p = [jnp.exp(sc - m_row) for sc in parts]
    acc = jnp.dot(jnp.concatenate(p, axis=1).astype(BF16), va, preferred_element_type=F32)
    return acc, p


def _row_max(parts):
    m = parts[0]
    for sc in parts[1:]:
        m = jnp.maximum(m, sc)
    return jnp.broadcast_to(jnp.max(m, axis=-1, keepdims=True), m.shape)


def _attn_kernel(qa_ref, ksa_ref, vsa_ref, kwa_ref, vwa_ref, kca_ref, vca_ref, gt_ref, ovt_ref,
                 o_ref, m_scr, acc_scr, qaug_scr, used_smem, *, seq, tq, tk):
    nb = seq // SEL_BLOCK
    nc = seq // CMP_STRIDE
    rows = HEADS_PER_GROUP * tq
    q0 = pl.program_id(1) * tq

    qa = qa_ref[...].reshape(rows, LANES)
    row_l = lax.broadcasted_iota(jnp.int32, (rows, LANES), 0)
    lane = lax.broadcasted_iota(jnp.int32, (rows, LANES), 1)
    t_l = q0 + (row_l - (row_l // tq) * tq)

    sc_all = _dot_nt(qa, kca_ref[...])
    last_ok = (t_l - (CMP_BLOCK - 1)) >> 4
    parts = [jnp.where(lane + c * LANES <= last_ok, sc_all[:, c * LANES:(c + 1) * LANES], NEG_INF)
             for c in range(nc // LANES)]
    m_c = jnp.where(t_l >= CMP_BLOCK - 1, _row_max(parts), -NEG_INF)
    acc_c, p_c = _softmax_values(parts, m_c, vca_ref[...])
    inv_c = jnp.broadcast_to(1.0 / jnp.maximum(acc_c[:, HEAD_DIM:HEAD_DIM + 1], 1e-30), (rows, LANES))
    o_c = acc_c[:, :HEAD_DIM] * inv_c[:, :HEAD_DIM]
    p = jnp.concatenate([pc * inv_c for pc in p_c], axis=1)

    wlen = WINDOW + tq
    w0 = pl.multiple_of(jnp.maximum(q0 - WINDOW, 0), tq)
    sw_all = _dot_nt(qa, kwa_ref[pl.ds(w0, wlen), :])
    parts = []
    for c in range(wlen // LANES):
        key = w0 + c * LANES + lane
        ok = key <= t_l
        if c * LANES < tq:
            ok = ok & (key > t_l - WINDOW)
        parts.append(jnp.where(ok, sw_all[:, c * LANES:(c + 1) * LANES], NEG_INF))
    acc_w, _ = _softmax_values(parts, _row_max(parts), vwa_ref[pl.ds(w0, wlen), :])
    o_w = acc_w[:, :HEAD_DIM] / jnp.maximum(acc_w[:, HEAD_DIM:HEAD_DIM + 1], 1e-30)

    psum = p[0:tq] + p[tq:2 * tq] + p[2 * tq:3 * tq] + p[3 * tq:4 * tq]
    ovt = ovt_ref[...]
    imp = sum(_dot_nt(ovt, part) for part in _split3(psum))
    j = lax.broadcasted_iota(jnp.int32, (nb, tq), 0)
    jt = (q0 + lax.broadcasted_iota(jnp.int32, (nb, tq), 1)) // SEL_BLOCK
    forced = (j == 0) | (j == jt) | (j == jt - 1)
    sel_t = jnp.where(forced, 1.0, 0.0)
    vals = jnp.where(forced, -3e38, jnp.where(j > jt, -FORCE_SCORE, imp))
    jf = j.astype(F32)
    for _ in range(min(SEL_TOPK, nb) - 3):
        best = jnp.max(vals, axis=0, keepdims=True)
        first = jnp.min(jnp.where(vals == best, jf, float(nb)), axis=0, keepdims=True)
        hit = jf == first
        sel_t = jnp.where(hit, 1.0, sel_t)
        vals = jnp.where(hit, -3e38, vals)

    not_chosen = ((1.0 - sel_t) * MASK_BIAS).T.astype(BF16)
    qaug_scr[:, :LANES] = qa
    for r in range(HEADS_PER_GROUP):
        qaug_scr[r * tq:(r + 1) * tq, LANES:LANES + nb] = not_chosen
    if nb < LANES:
        qaug_scr[:, LANES + nb:] = jnp.zeros((rows, LANES - nb), BF16)
    ntile = tk // LANES

    def tile_scores(kt, causal):
        kb = pl.multiple_of(kt * tk, tk)
        sc = _dot_nt(qaug_scr[...], ksa_ref[pl.ds(kb, tk), :])
        if causal:
            key = kb + lax.broadcasted_iota(jnp.int32, (rows, tk), 1)
            sc = jnp.where(jnp.concatenate([t_l] * ntile, axis=1) >= key, sc, NEG_INF)
        return kb, sc

    def max_tile(kt, causal):
        _, sc = tile_scores(kt, causal)
        part = m_scr[...]
        for c in range(ntile):
            part = jnp.maximum(part, sc[:, c * LANES:(c + 1) * LANES])
        m_scr[...] = part

    def acc_tile(kt, causal):
        kb, sc = tile_scores(kt, causal)
        pr = jnp.exp(sc - jnp.concatenate([m_scr[...]] * ntile, axis=1))
        acc_scr[...] += jnp.dot(pr.astype(BF16), vsa_ref[pl.ds(kb, tk), :], preferred_element_type=F32)

    blocks_per_tile = tk // SEL_BLOCK
    for i in range(seq // tk):
        chosen_here = jnp.max(sel_t[i * blocks_per_tile:(i + 1) * blocks_per_tile, :])
        used_smem[i] = (chosen_here > 0.5).astype(jnp.int32)

    def sweep(tile_fn):
        def step(kt, carry):
            @pl.when(used_smem[kt] > 0)
            def _():
                tile_fn(kt, False)
            return carry
        n_before = q0 // tk
        lax.fori_loop(0, n_before, step, 0)
        tile_fn(n_before, True)

    m_scr[...] = jnp.full((rows, LANES), NEG_INF, F32)
    sweep(max_tile)
    m_scr[...] = jnp.broadcast_to(jnp.max(m_scr[...], axis=-1, keepdims=True), (rows, LANES))
    acc_scr[...] = jnp.zeros((rows, LANES), F32)
    sweep(acc_tile)
    acc = acc_scr[...]
    o_s = acc[:, :HEAD_DIM] / jnp.maximum(acc[:, HEAD_DIM:HEAD_DIM + 1], 1e-30)

    gate = jax.nn.sigmoid(gt_ref[...])
    for r in range(HEADS_PER_GROUP):
        sl = slice(r * tq, (r + 1) * tq)
        gc = lambda x: gate[:, x * HEADS_PER_GROUP + r:x * HEADS_PER_GROUP + r + 1]
        o_ref[:, r * HEAD_DIM:(r + 1) * HEAD_DIM] = gc(0) * o_c[sl] + gc(1) * o_s[sl] + gc(2) * o_w[sl]


def _attention(qa, ksa, vsa, kwa, vwa, kca, vca, gt, batch, seq):
    tq, tk = ATTN_TQ, ATTN_TK
    bg = batch * N_KV_GROUPS
    nq = seq // tq
    nb, nc = seq // SEL_BLOCK, seq // CMP_STRIDE
    assert SEL_TOPK <= nb <= LANES and seq % tk == 0 and tk % tq == 0 and tq % LANES == 0 and nc % LANES == 0
    rows = HEADS_PER_GROUP * tq
    gq = HEADS_PER_GROUP * HEAD_DIM
    cs = jnp.arange(nc)[None, :] * CMP_STRIDE
    ss = jnp.arange(nb)[:, None] * SEL_BLOCK
    ovt = ((cs <= ss + SEL_BLOCK - 1) & (cs + CMP_BLOCK - 1 >= ss)).astype(BF16)
    per_bg = lambda n, w: pl.BlockSpec((None, n, w), lambda i, j: (i, 0, 0))
    return pl.pallas_call(
        functools.partial(_attn_kernel, seq=seq, tq=tq, tk=tk),
        grid=(bg, nq),
        in_specs=[pl.BlockSpec((None, HEADS_PER_GROUP, tq, LANES), lambda i, j: (i, 0, j, 0)),
                  per_bg(seq, 2 * LANES), per_bg(seq, LANES), per_bg(seq, LANES), per_bg(seq, LANES),
                  per_bg(nc, LANES), per_bg(nc, LANES),
                  pl.BlockSpec((None, tq, 3 * HEADS_PER_GROUP), lambda i, j: (i, j, 0)),
                  pl.BlockSpec((nb, nc), lambda i, j: (0, 0))],
        out_specs=pl.BlockSpec((tq, gq), lambda i, j: ((i // N_KV_GROUPS) * nq + j, i % N_KV_GROUPS)),
        out_shape=jax.ShapeDtypeStruct((batch * seq, ATTN_WIDTH), F32),
        scratch_shapes=[pltpu.VMEM((rows, LANES), F32), pltpu.VMEM((rows, LANES), F32),
                        pltpu.VMEM((rows, 2 * LANES), BF16), pltpu.SMEM((seq // tk,), jnp.int32)],
        compiler_params=_params("parallel", "arbitrary"),
        name="nsa_attention",
    )(qa, ksa, vsa, kwa, vwa, kca, vca, gt, ovt)


def _alibi_columns(seq):
    dh = HEAD_DIM
    nb, nc = seq // SEL_BLOCK, seq // CMP_STRIDE
    head = jnp.arange(1, N_HEADS + 1, dtype=F32).reshape(N_KV_GROUPS, HEADS_PER_GROUP)
    parts = _split3(jnp.exp2(-8.0 * head / N_HEADS))
    qcols = jnp.stack([float(SEL_BLOCK) * p.astype(F32) for p in parts] + [p.astype(F32) for p in parts], axis=-1)
    qcols = jnp.pad(qcols, ((0, 0), (0, 0), (0, LANES - dh - qcols.shape[-1]))).astype(BF16)

    def key_cols(u, scale):
        cols = jnp.stack([u // SEL_BLOCK] * 3 + [u % SEL_BLOCK] * 3, axis=-1).astype(F32) * scale
        return jnp.pad(cols, ((0, 0), (0, LANES - dh - cols.shape[-1])))

    pos = jnp.arange(seq)
    onehot = (pos[:, None] // SEL_BLOCK == jnp.arange(LANES)[None, :]) & (jnp.arange(LANES) < nb)
    win_cols = key_cols(pos, 1.0).astype(BF16)
    sel_cols = jnp.concatenate([win_cols, onehot.astype(BF16)], axis=1)
    cmp_cols = key_cols(jnp.arange(nc), float(CMP_STRIDE)).astype(BF16)
    ones_cols = (jnp.arange(LANES - dh) == 0).astype(BF16)
    return qcols, sel_cols, win_cols, cmp_cols, ones_cols


def _nsa_layer(h, batch, seq, norm, w_in, k_pe, k_w1, k_w2, v_pe, v_w1, v_w2, w_out):
    gg = N_KV_GROUPS
    bg = batch * gg
    (qa, ksa, vsa, kwa, vwa, kvc, gt, z), cmp_cols, ones_cols = _nsa_in(h, norm, w_in, batch, seq)
    merge = lambda x: x.reshape((bg,) + x.shape[2:])
    with_cols = lambda x, cols: jnp.concatenate([x, jnp.broadcast_to(cols, x.shape[:-1] + cols.shape[-1:])], axis=-1)
    kc, vc = _compress(merge(kvc[0]), merge(kvc[1]), k_pe, k_w1, k_w2, v_pe, v_w1, v_w2)
    o = _attention(merge(qa), merge(ksa), merge(vsa), merge(kwa), merge(vwa),
                   with_cols(kc, cmp_cols), with_cols(vc, ones_cols), merge(gt), batch, seq)
    return _nsa_out(o, z, w_out, h)


def _split2(x):
    hi = x.astype(BF16)
    return hi, (x - hi.astype(F32)).astype(BF16)


def _dot3(a, b_hi, b_lo):
    a_hi, a_lo = _split2(a)
    return (jnp.dot(a_hi, b_hi, preferred_element_type=F32) + jnp.dot(a_lo, b_hi, preferred_element_type=F32)
            + jnp.dot(a_hi, b_lo, preferred_element_type=F32))


def _member(shape, axis, width):
    return (lax.broadcasted_iota(jnp.int32, shape, axis) // width) % 2


def _s5_state_kernel(u_ref, w_ref, xr_ref, xi_ref):
    w = w_ref[...]
    p = SSM_STATE
    wide = jnp.concatenate([w[:, :p], w[:, :p], w[:, p:], w[:, p:]], axis=1)
    w_pair = jnp.where(_member(wide.shape, 0, SSM_GROUP) == _member(wide.shape, 1, p), wide, 0.0)
    x = _dot3(u_ref[...], *_split2(w_pair))
    half = x.shape[1] // 2
    xr_ref[...] = x[:, :half]
    xi_ref[...] = x[:, half:]


def _s5_scan_kernel(xr_ref, xi_ref, ar_ref, ai_ref, cr_ref, ci_ref, *, batch, nchunk):
    ar, ai = ar_ref[...], ai_ref[...]
    width = ar.shape[1]

    def step(n, carry):
        new = []
        for b in range(batch):
            cr, ci = carry[2 * b], carry[2 * b + 1]
            idx = b * nchunk + n
            cr_ref[pl.ds(idx, 1), :] = cr
            ci_ref[pl.ds(idx, 1), :] = ci
            lr, li = xr_ref[pl.ds(idx, 1), :], xi_ref[pl.ds(idx, 1), :]
            new += [ar * cr - ai * ci + lr, ar * ci + ai * cr + li]
        return tuple(new)

    zero = jnp.zeros((1, width), F32)
    lax.fori_loop(0, nchunk, step, (zero,) * (2 * batch))


def _s5_y_kernel(u_ref, cr_ref, ci_ref, k_ref, v_ref, d_ref, y_ref, m_scr):
    k = k_ref[...]
    col = lax.broadcasted_iota(jnp.int32, k.shape, 1)
    m_scr[0:S5_PAIR_LANES, :] = k
    for s in range(1, S5_CHUNK):
        shifted = pltpu.roll(k, s * S5_PAIR_LANES, 1)
        m_scr[s * S5_PAIR_LANES:(s + 1) * S5_PAIR_LANES, :] = jnp.where(col >= s * S5_PAIR_LANES, shifted, 0.0)
    v = v_ref[...]
    p = SSM_STATE
    tall = jnp.concatenate([v[:p], v[:p], v[p:], v[p:]], axis=0)
    v_pair = jnp.where(_member(tall.shape, 0, p) == _member(tall.shape, 1, SSM_GROUP), tall, 0.0)
    u = u_ref[...]
    carry = jnp.concatenate([cr_ref[...], ci_ref[...]], axis=1)
    y = _dot3(u, *_split2(m_scr[...])) + _dot3(carry, *_split2(v_pair)) + d_ref[...] * u
    y_ref[...] = jax.nn.gelu(y)


def _s5_matrices(log_dt, lambda_re, lambda_im, b_re, b_im, c_re, c_im, d_skip):
    hp = lax.Precision.HIGHEST
    gn, pn, cn, ln = SSM_GROUPS, SSM_STATE, SSM_GROUP, S5_CHUNK
    dt = jnp.exp(log_dt.astype(F32))[:, None]
    lre = jnp.minimum(lambda_re.astype(F32), -1e-4)
    lim = lambda_im.astype(F32)
    mag = jnp.exp(lre * dt)
    ab_re, ab_im = mag * jnp.cos(lim * dt), mag * jnp.sin(lim * dt)
    den = lre * lre + lim * lim
    nr = ab_re - 1.0
    coef_re = (nr * lre + ab_im * lim) / den
    coef_im = (ab_im * lre - nr * lim) / den
    br32, bi32 = b_re.astype(F32), b_im.astype(F32)
    bb_re = coef_re[..., None] * br32 - coef_im[..., None] * bi32
    bb_im = coef_re[..., None] * bi32 + coef_im[..., None] * br32
    cr32, ci32 = c_re.astype(F32), c_im.astype(F32)
    pr, pi = [jnp.ones_like(ab_re)], [jnp.zeros_like(ab_re)]
    for _ in range(ln):
        pr, pi = pr + [pr[-1] * ab_re - pi[-1] * ab_im], pi + [pr[-1] * ab_im + pi[-1] * ab_re]
    pw_re, pw_im = jnp.stack(pr), jnp.stack(pi)
    ab_b_re = pw_re[:ln, :, :, None] * bb_re - pw_im[:ln, :, :, None] * bb_im
    ab_b_im = pw_re[:ln, :, :, None] * bb_im + pw_im[:ln, :, :, None] * bb_re
    kern = (jnp.einsum('gop,kgpi->kgoi', cr32, ab_b_re, precision=hp)
            - jnp.einsum('gop,kgpi->kgoi', ci32, ab_b_im, precision=hp))
    gp = gn // 2
    same = jnp.eye(2, dtype=F32)
    k_pair = kern.reshape(ln, gp, 2, cn, cn).transpose(1, 2, 4, 0, 3)
    k_pair = k_pair[:, :, :, :, None, :] * same[None, :, None, None, :, None]
    k_pair = k_pair.reshape(gp, S5_PAIR_LANES, S5_PAIR)
    w = jnp.stack([ab_b_re[::-1], ab_b_im[::-1]]).reshape(2, ln, gp, 2, pn, cn)
    w_pair = w.transpose(2, 1, 3, 5, 0, 4).reshape(gp, S5_PAIR, 2 * pn)
    ar1, ai1 = pw_re[1:], pw_im[1:]
    v_re = cr32[None] * ar1[:, :, None, :] - ci32[None] * ai1[:, :, None, :]
    v_im = -(cr32[None] * ai1[:, :, None, :] + ci32[None] * ar1[:, :, None, :])
    v = jnp.stack([v_re, v_im]).reshape(2, ln, gp, 2, cn, pn)
    v_pair = v.transpose(2, 0, 5, 1, 3, 4).reshape(gp, 2 * pn, S5_PAIR)
    d_pair = jnp.tile(d_skip.astype(F32).reshape(gp, 1, 2 * cn), (1, 1, ln))
    a_end_re = pw_re[ln].reshape(1, gn * pn)
    a_end_im = pw_im[ln].reshape(1, gn * pn)
    return k_pair, w_pair, v_pair, d_pair, a_end_re, a_end_im


def _s5_layer(h, batch, seq, norm, w_in, log_dt, lambda_re, lambda_im, b_re, b_im, c_re, c_im, d_skip,
              w_glu, w_out, final_norm, final):
    gp, ln, cn = SSM_GROUPS // 2, S5_CHUNK, SSM_GROUP
    nchunk = seq // ln
    rows = batch * nchunk
    state_w = 2 * SSM_STATE
    k_pair, w_pair, v_pair, d_pair, a_re, a_im = _s5_matrices(
        log_dt, lambda_re, lambda_im, b_re, b_im, c_re, c_im, d_skip)
    up, z = _s5_in(h, norm, w_in)
    pair3 = lambda a, b: pl.BlockSpec((None, a, b), lambda i: (i, 0, 0))
    slab = pl.BlockSpec((rows, state_w), lambda i: (0, i))
    xr, xi = pl.pallas_call(
        _s5_state_kernel,
        grid=(gp,),
        in_specs=[pair3(rows, S5_PAIR), pair3(S5_PAIR, state_w)],
        out_specs=[slab, slab],
        out_shape=[jax.ShapeDtypeStruct((rows, gp * state_w), F32)] * 2,
        compiler_params=_params("parallel"),
        name="s5_chunk_state",
    )(up, w_pair)
    scan_w = 4 * state_w
    wide = pl.BlockSpec((rows, scan_w), lambda i: (0, i))
    coef = pl.BlockSpec((1, scan_w), lambda i: (0, i))
    cr, ci = pl.pallas_call(
        functools.partial(_s5_scan_kernel, batch=batch, nchunk=nchunk),
        grid=(gp * state_w // scan_w,),
        in_specs=[wide, wide, coef, coef],
        out_specs=[wide, wide],
        out_shape=[jax.ShapeDtypeStruct((rows, gp * state_w), F32)] * 2,
        compiler_params=_params("parallel"),
        name="s5_carry_scan",
    )(xr, xi, a_re, a_im)
    yp = pl.pallas_call(
        _s5_y_kernel,
        grid=(gp,),
        in_specs=[pair3(rows, S5_PAIR), slab, slab, pair3(S5_PAIR_LANES, S5_PAIR), pair3(state_w, S5_PAIR),
                  pair3(1, S5_PAIR)],
        out_specs=pair3(rows, S5_PAIR),
        out_shape=jax.ShapeDtypeStruct((gp, rows, S5_PAIR), F32),
        scratch_shapes=[pltpu.VMEM((S5_PAIR, S5_PAIR), F32)],
        compiler_params=_params("parallel"),
        name="s5_chunk_output",
    )(up, cr, ci, k_pair, v_pair, d_pair)
    return _s5_out(yp, z, w_glu, w_out, h, final_norm, final)


def kernel(x, l0_norm, l0_w_in, l0_cmp_k_pe, l0_cmp_k_w1, l0_cmp_k_w2, l0_cmp_v_pe, l0_cmp_v_w1, l0_cmp_v_w2, l0_w_out, l1_norm, l1_w_in, l1_log_dt, l1_lambda_re, l1_lambda_im, l1_b_re, l1_b_im, l1_c_re, l1_c_im, l1_d, l1_w_glu, l1_w_out, l2_norm, l2_w_in, l2_cmp_k_pe, l2_cmp_k_w1, l2_cmp_k_w2, l2_cmp_v_pe, l2_cmp_v_w1, l2_cmp_v_w2, l2_w_out, l3_norm, l3_w_in, l3_log_dt, l3_lambda_re, l3_lambda_im, l3_b_re, l3_b_im, l3_c_re, l3_c_im, l3_d, l3_w_glu, l3_w_out, final_norm):
    batch, seq, _ = x.shape
    h = x.reshape(batch * seq, D_MODEL)
    h = _nsa_layer(h, batch, seq, l0_norm, l0_w_in, l0_cmp_k_pe, l0_cmp_k_w1, l0_cmp_k_w2,
                   l0_cmp_v_pe, l0_cmp_v_w1, l0_cmp_v_w2, l0_w_out)
    h = _s5_layer(h, batch, seq, l1_norm, l1_w_in, l1_log_dt, l1_lambda_re, l1_lambda_im, l1_b_re, l1_b_im,
                  l1_c_re, l1_c_im, l1_d, l1_w_glu, l1_w_out, final_norm, False)
    h = _nsa_layer(h, batch, seq, l2_norm, l2_w_in, l2_cmp_k_pe, l2_cmp_k_w1, l2_cmp_k_w2,
                   l2_cmp_v_pe, l2_cmp_v_w1, l2_cmp_v_w2, l2_w_out)
    h = _s5_layer(h, batch, seq, l3_norm, l3_w_in, l3_log_dt, l3_lambda_re, l3_lambda_im, l3_b_re, l3_b_im,
                  l3_c_re, l3_c_im, l3_d, l3_w_glu, l3_w_out, final_norm, True)
    return h.reshape(batch, seq, D_MODEL)
```

```python
import functools

import jax
import jax.numpy as jnp
from jax import lax
from jax.experimental import pallas as pl
from jax.experimental.pallas import tpu as pltpu

F32 = jnp.float32
BF16 = jnp.bfloat16

D_MODEL = 1024
EPS = 1e-6
NEG_INF = -1e30
FORCE_SCORE = 1e9

N_HEADS = 16
HEAD_DIM = 64
N_KV_GROUPS = 4
HEADS_PER_GROUP = N_HEADS // N_KV_GROUPS
ATTN_WIDTH = N_HEADS * HEAD_DIM
KV_WIDTH = N_KV_GROUPS * HEAD_DIM
CMP_BLOCK = 32
CMP_STRIDE = 16
CMP_HIDDEN = 256
SEL_BLOCK = 64
SEL_TOPK = 16
WINDOW = 512
N_GATES = 3 * N_HEADS
GATES_PAD = 128

SSM_WIDTH = D_MODEL
SSM_GROUP = 16
SSM_GROUPS = SSM_WIDTH // SSM_GROUP
SSM_STATE = 64
S5_CHUNK = 16
S5_PAIRS = SSM_GROUPS // 2
S5_PAIR_LANES = 2 * SSM_GROUP
S5_PAIR = S5_CHUNK * S5_PAIR_LANES

ROW_TILE = 512
ATTN_TQ = 256
ATTN_TK = 512
LANES = 128
MASK_BIAS = -2.0 ** 100
EXP_HEADROOM = 60.0
VMEM_LIMIT = 56 * 1024 * 1024


def _params(*sem):
    return pltpu.CompilerParams(dimension_semantics=sem, vmem_limit_bytes=VMEM_LIMIT)


def _rms(x, g):
    return x * lax.rsqrt(jnp.mean(x * x, axis=-1, keepdims=True) + EPS) * g


def _silu(x):
    return x * jax.nn.sigmoid(x)


def _nsa_in_kernel(h_ref, g_ref, w_ref, qcols_ref, selc_ref, winc_ref, ones_ref,
                   qa_ref, ksa_ref, vsa_ref, kwa_ref, vwa_ref, kvc_ref, gt_ref, z_ref, c_scr):
    gg, rr, dh = N_KV_GROUPS, HEADS_PER_GROUP, HEAD_DIM
    xn = _rms(h_ref[...], g_ref[...])
    y = jnp.dot(xn.astype(BF16), w_ref[...], preferred_element_type=F32)
    for g in range(gg):
        for r in range(rr):
            k = g * rr + r
            qa_ref[g, r] = jnp.broadcast_to(qcols_ref[g, r:r + 1, :], (ROW_TILE, LANES))
            qa_ref[g, r, :, :dh] = (y[:, k * dh:(k + 1) * dh] * (dh ** -0.5)).astype(BF16)
    o = ATTN_WIDTH + 2 * KV_WIDTH
    consts = (selc_ref[...], jnp.broadcast_to(ones_ref[...], (ROW_TILE, LANES)), winc_ref[...],
              jnp.broadcast_to(ones_ref[...], (ROW_TILE, LANES)))
    for kind, ref in enumerate((ksa_ref, vsa_ref, kwa_ref, vwa_ref)):
        for g in range(gg):
            c = o + (kind * gg + g) * dh
            ref[g] = consts[kind]
            ref[g, :, :dh] = y[:, c:c + dh].astype(BF16)
    lane = lax.broadcasted_iota(jnp.int32, (ROW_TILE // CMP_STRIDE, LANES), 1)
    for n in range(2 * KV_WIDTH // LANES):
        c_scr[n] = y[:, ATTN_WIDTH + n * LANES:ATTN_WIDTH + (n + 1) * LANES]
        kind, g0 = divmod(2 * n, gg)
        for a in range(CMP_STRIDE // 2):
            even = c_scr[n, pl.ds(2 * a, ROW_TILE // CMP_STRIDE, stride=CMP_STRIDE), :]
            odd = c_scr[n, pl.ds(2 * a + 1, ROW_TILE // CMP_STRIDE, stride=CMP_STRIDE), :]
            kvc_ref[kind, g0, :, a * LANES:(a + 1) * LANES] = jnp.where(lane < dh, even, pltpu.roll(odd, dh, 1))
            kvc_ref[kind, g0 + 1, :, a * LANES:(a + 1) * LANES] = jnp.where(lane < dh, pltpu.roll(even, dh, 1), odd)
    o += 4 * KV_WIDTH
    z_ref[...] = y[:, o:o + ATTN_WIDTH]
    o += ATTN_WIDTH
    per_group = 3 * rr
    for g in range(gg):
        gt_ref[g] = y[:, o + g * per_group:o + (g + 1) * per_group]


def _nsa_in(h, norm, w_in, batch, seq):
    gg, rr, dh = N_KV_GROUPS, HEADS_PER_GROUP, HEAD_DIM
    nt = seq // ROW_TILE
    nc_tile = ROW_TILE // CMP_STRIDE
    a, b = ATTN_WIDTH + 6 * KV_WIDTH, ATTN_WIDTH + 6 * KV_WIDTH + N_GATES
    w_gates = w_in[:, a:b].reshape(D_MODEL, 3, gg, rr).transpose(0, 2, 1, 3).reshape(D_MODEL, N_GATES)
    w = jnp.concatenate([w_in[:, :a], w_in[:, b:], w_gates,
                         jnp.zeros((D_MODEL, GATES_PAD - N_GATES), w_in.dtype)], axis=1).astype(BF16)
    n = w.shape[1]
    qcols, sel_cols, win_cols, cmp_cols, ones_cols = _alibi_columns(seq)
    lead = lambda cols: jnp.pad(cols, [(0, 0)] * (cols.ndim - 1) + [(dh, 0)])
    const = lambda shape: pl.BlockSpec(shape, lambda i: (0,) * len(shape))
    by_pos = lambda width: pl.BlockSpec((ROW_TILE, width), lambda i: (i % nt, 0))
    per_b = lambda *tail: pl.BlockSpec((None, gg) + tail, lambda i: (i // nt, 0) + (0,) * (len(tail) - 2) + (i % nt, 0))
    outs = pl.pallas_call(
        _nsa_in_kernel,
        grid=(batch * nt,),
        in_specs=[pl.BlockSpec((ROW_TILE, D_MODEL), lambda i: (i, 0)), const((1, D_MODEL)), const((D_MODEL, n)),
                  const((gg, rr, LANES)), by_pos(2 * LANES), by_pos(LANES), const((1, LANES))],
        out_specs=[per_b(rr, ROW_TILE, LANES), per_b(ROW_TILE, 2 * LANES), per_b(ROW_TILE, LANES),
                   per_b(ROW_TILE, LANES), per_b(ROW_TILE, LANES),
                   pl.BlockSpec((2, None, gg, nc_tile, CMP_STRIDE * dh), lambda i: (0, i // nt, 0, i % nt, 0)),
                   per_b(ROW_TILE, 3 * rr), pl.BlockSpec((ROW_TILE, ATTN_WIDTH), lambda i: (i, 0))],
        out_shape=[jax.ShapeDtypeStruct((batch, gg, rr, seq, LANES), BF16),
                   jax.ShapeDtypeStruct((batch, gg, seq, 2 * LANES), BF16),
                   jax.ShapeDtypeStruct((batch, gg, seq, LANES), BF16),
                   jax.ShapeDtypeStruct((batch, gg, seq, LANES), BF16),
                   jax.ShapeDtypeStruct((batch, gg, seq, LANES), BF16),
                   jax.ShapeDtypeStruct((2, batch, gg, seq // CMP_STRIDE, CMP_STRIDE * dh), F32),
                   jax.ShapeDtypeStruct((batch, gg, seq, 3 * rr), F32),
                   jax.ShapeDtypeStruct((batch * seq, ATTN_WIDTH), F32)],
        scratch_shapes=[pltpu.VMEM((2 * KV_WIDTH // LANES, ROW_TILE, LANES), F32)],
        compiler_params=_params("parallel"),
        name="nsa_in_proj",
    )(h, norm.reshape(1, D_MODEL), w, lead(qcols), lead(sel_cols), lead(win_cols), lead(ones_cols)[None])
    return outs, cmp_cols, ones_cols


def _quarter_exchange(parts):
    quarter = lax.broadcasted_iota(jnp.int32, parts[0].shape, 1) // S5_PAIR_LANES
    outs = []
    for b in range(4):
        acc = None
        for a in range(4):
            moved = parts[a] if a == b else pltpu.roll(parts[a], ((a - b) % 4) * S5_PAIR_LANES, 1)
            acc = moved if acc is None else jnp.where(quarter == a, moved, acc)
        outs.append(acc)
    return outs


def _s5_in_kernel(h_ref, g_ref, w_ref, u_ref, z_ref, u_scr):
    xn = _rms(h_ref[...], g_ref[...])
    y = jnp.dot(xn.astype(BF16), w_ref[...], preferred_element_type=F32)
    z_ref[...] = y[:, SSM_WIDTH:]
    nch = ROW_TILE // S5_CHUNK
    for m in range(SSM_WIDTH // 128):
        u_scr[m] = y[:, m * 128:(m + 1) * 128]
        for k in range(S5_CHUNK // 4):
            steps = [u_scr[m, pl.ds(4 * k + a, nch, stride=S5_CHUNK), :] for a in range(4)]
            for b, tile in enumerate(_quarter_exchange(steps)):
                u_ref[4 * m + b, :, k * 128:(k + 1) * 128] = tile


def _s5_in(h, norm, w_in):
    t = h.shape[0]
    nch = ROW_TILE // S5_CHUNK
    row = lambda width: pl.BlockSpec((ROW_TILE, width), lambda i: (i, 0))
    return pl.pallas_call(
        _s5_in_kernel,
        grid=(t // ROW_TILE,),
        in_specs=[row(D_MODEL), pl.BlockSpec((1, D_MODEL), lambda i: (0, 0)),
                  pl.BlockSpec((D_MODEL, 2 * SSM_WIDTH), lambda i: (0, 0))],
        out_specs=[pl.BlockSpec((S5_PAIRS, nch, S5_PAIR), lambda i: (0, i, 0)), row(SSM_WIDTH)],
        out_shape=[jax.ShapeDtypeStruct((S5_PAIRS, t // S5_CHUNK, S5_PAIR), F32),
                   jax.ShapeDtypeStruct((t, SSM_WIDTH), F32)],
        scratch_shapes=[pltpu.VMEM((SSM_WIDTH // 128, ROW_TILE, 128), F32)],
        compiler_params=_params("parallel"),
        name="s5_in_proj",
    )(h, norm.reshape(1, D_MODEL), w_in.astype(BF16))


def _nsa_out_kernel(o_ref, z_ref, w_ref, res_ref, out_ref):
    a = o_ref[...] * _silu(z_ref[...])
    out_ref[...] = res_ref[...] + jnp.dot(a.astype(BF16), w_ref[...], preferred_element_type=F32)


def _nsa_out(o, z, w_out, res):
    t = o.shape[0]
    row = pl.BlockSpec((ROW_TILE, D_MODEL), lambda i: (i, 0))
    return pl.pallas_call(
        _nsa_out_kernel,
        grid=(t // ROW_TILE,),
        in_specs=[row, row, pl.BlockSpec((ATTN_WIDTH, D_MODEL), lambda i: (0, 0)), row],
        out_specs=row,
        out_shape=jax.ShapeDtypeStruct((t, D_MODEL), F32),
        compiler_params=_params("parallel"),
        name="nsa_out_proj",
    )(o, z, w_out.astype(BF16), res)


def _s5_out_kernel(y_ref, z_ref, wg_ref, wo_ref, res_ref, fn_ref, out_ref, y_scr, *, final):
    nch = ROW_TILE // S5_CHUNK
    for m in range(SSM_WIDTH // 128):
        for k in range(S5_CHUNK // 4):
            pairs = [y_ref[4 * m + b, :, k * 128:(k + 1) * 128] for b in range(4)]
            for a, tile in enumerate(_quarter_exchange(pairs)):
                y_scr[m, pl.ds(4 * k + a, nch, stride=S5_CHUNK), :] = tile
    y = jnp.concatenate([y_scr[m] for m in range(SSM_WIDTH // 128)], axis=1)
    gl = jnp.dot(y.astype(BF16), wg_ref[...], preferred_element_type=F32)
    v = gl[:, :SSM_WIDTH] * jax.nn.sigmoid(gl[:, SSM_WIDTH:])
    v = v * _silu(z_ref[...])
    h = res_ref[...] + jnp.dot(v.astype(BF16), wo_ref[...], preferred_element_type=F32)
    out_ref[...] = _rms(h, fn_ref[...]) if final else h


def _s5_out(y, z, w_glu, w_out, res, final_norm, final):
    t = z.shape[0]
    row = pl.BlockSpec((ROW_TILE, D_MODEL), lambda i: (i, 0))
    return pl.pallas_call(
        functools.partial(_s5_out_kernel, final=final),
        grid=(t // ROW_TILE,),
        in_specs=[pl.BlockSpec((S5_PAIRS, ROW_TILE // S5_CHUNK, S5_PAIR), lambda i: (0, i, 0)), row,
                  pl.BlockSpec((SSM_WIDTH, 2 * SSM_WIDTH), lambda i: (0, 0)),
                  pl.BlockSpec((SSM_WIDTH, D_MODEL), lambda i: (0, 0)), row,
                  pl.BlockSpec((1, D_MODEL), lambda i: (0, 0))],
        out_specs=row,
        out_shape=jax.ShapeDtypeStruct((t, D_MODEL), F32),
        scratch_shapes=[pltpu.VMEM((SSM_WIDTH // 128, ROW_TILE, 128), F32)],
        compiler_params=_params("parallel"),
        name="s5_glu_out_proj",
    )(y, z, w_glu.astype(BF16), w_out.astype(BF16), res, final_norm.reshape(1, D_MODEL))


def _cmp_one(a_ref, pe_ref, w1_ref, w2_ref, out_ref):
    nc = a_ref.shape[0]
    a = a_ref[...]
    pe = pe_ref[...]
    h_top = jnp.dot((a + pe[0:1]).astype(BF16), w1_ref[0], preferred_element_type=F32)
    h_bot = jnp.dot((a + pe[1:2]).astype(BF16), w1_ref[1], preferred_element_type=F32)
    hid = _silu(h_top + pltpu.roll(h_bot, nc - 1, 0))
    out = jnp.dot(hid.astype(BF16), w2_ref[...], preferred_element_type=F32)
    keep = lax.broadcasted_iota(jnp.int32, out.shape, 0) < nc - 1
    out_ref[...] = jnp.where(keep, out, 0.0).astype(BF16)


def _cmp_kernel(ka_ref, va_ref, kpe_ref, kw1_ref, kw2_ref, vpe_ref, vw1_ref, vw2_ref, kc_ref, vc_ref):
    _cmp_one(ka_ref, kpe_ref, kw1_ref, kw2_ref, kc_ref)
    _cmp_one(va_ref, vpe_ref, vw1_ref, vw2_ref, vc_ref)


def _compress(ka, va, k_pe, k_w1, k_w2, v_pe, v_w1, v_w2):
    bg, nc, half = ka.shape
    blk = pl.BlockSpec((None, nc, half), lambda i: (i, 0, 0))
    full = lambda shape: pl.BlockSpec(shape, lambda i: (0,) * len(shape))
    prep = lambda pe, w1, w2: (pe.reshape(2, half), w1.reshape(2, half, CMP_HIDDEN).astype(BF16), w2.astype(BF16))
    out = pl.BlockSpec((None, nc, HEAD_DIM), lambda i: (i, 0, 0))
    wspecs = [full((2, half)), full((2, half, CMP_HIDDEN)), full((CMP_HIDDEN, HEAD_DIM))]
    return pl.pallas_call(
        _cmp_kernel,
        grid=(bg,),
        in_specs=[blk, blk] + wspecs + wspecs,
        out_specs=[out, out],
        out_shape=[jax.ShapeDtypeStruct((bg, nc, HEAD_DIM), BF16)] * 2,
        compiler_params=_params("parallel"),
        name="nsa_compress",
    )(ka, va, *prep(k_pe, k_w1, k_w2), *prep(v_pe, v_w1, v_w2))


def _dot_nt(a, b):
    return lax.dot_general(a, b, (((1,), (1,)), ((), ())), preferred_element_type=F32)


def _split3(x):
    hi = x.astype(BF16)
    r1 = x - hi.astype(F32)
    mid = r1.astype(BF16)
    lo = (r1 - mid.astype(F32)).astype(BF16)
    return hi, mid, lo


def _softmax_values(parts, m_row, va):
    p = [jnp.exp(sc - m_row) for sc in parts]
    acc = jnp.dot(jnp.concatenate(p, axis=1).astype(BF16), va, preferred_element_type=F32)
    return acc, p


def _row_max(parts):
    m = parts[0]
    for sc in parts[1:]:
        m = jnp.maximum(m, sc)
    return jnp.broadcast_to(jnp.max(m, axis=-1, keepdims=True), m.shape)


def _attn_kernel(qa_ref, ksa_ref, vsa_ref, kwa_ref, vwa_ref, kca_ref, vca_ref, gt_ref, ovt_ref,
                 o_ref, m_scr, top_scr, acc_scr, qaug_scr, used_smem, *, seq, tq, tk):
    nb = seq // SEL_BLOCK
    nc = seq // CMP_STRIDE
    rows = HEADS_PER_GROUP * tq
    q0 = pl.program_id(1) * tq

    qa = qa_ref[...].reshape(rows, LANES)
    row_l = lax.broadcasted_iota(jnp.int32, (rows, LANES), 0)
    lane = lax.broadcasted_iota(jnp.int32, (rows, LANES), 1)
    t_l = q0 + (row_l - (row_l // tq) * tq)

    sc_all = _dot_nt(qa, kca_ref[...])
    last_ok = (t_l - (CMP_BLOCK - 1)) >> 4
    parts = [jnp.where(lane + c * LANES <= last_ok, sc_all[:, c * LANES:(c + 1) * LANES], NEG_INF)
             for c in range(nc // LANES)]
    m_c = jnp.where(t_l >= CMP_BLOCK - 1, _row_max(parts), -NEG_INF)
    acc_c, p_c = _softmax_values(parts, m_c, vca_ref[...])
    inv_c = jnp.broadcast_to(1.0 / jnp.maximum(acc_c[:, HEAD_DIM:HEAD_DIM + 1], 1e-30), (rows, LANES))
    o_c = acc_c[:, :HEAD_DIM] * inv_c[:, :HEAD_DIM]
    p = jnp.concatenate([pc * inv_c for pc in p_c], axis=1)

    wlen = WINDOW + tq
    w0 = pl.multiple_of(jnp.maximum(q0 - WINDOW, 0), tq)
    sw_all = _dot_nt(qa, kwa_ref[pl.ds(w0, wlen), :])
    parts = []
    for c in range(wlen // LANES):
        key = w0 + c * LANES + lane
        ok = key <= t_l
        if c * LANES < tq:
            ok = ok & (key > t_l - WINDOW)
        parts.append(jnp.where(ok, sw_all[:, c * LANES:(c + 1) * LANES], NEG_INF))
    acc_w, _ = _softmax_values(parts, _row_max(parts), vwa_ref[pl.ds(w0, wlen), :])
    o_w = acc_w[:, :HEAD_DIM] / jnp.maximum(acc_w[:, HEAD_DIM:HEAD_DIM + 1], 1e-30)

    psum = p[0:tq] + p[tq:2 * tq] + p[2 * tq:3 * tq] + p[3 * tq:4 * tq]
    ovt = ovt_ref[...]
    imp = sum(_dot_nt(ovt, part) for part in _split3(psum))
    j = lax.broadcasted_iota(jnp.int32, (nb, tq), 0)
    jt = (q0 + lax.broadcasted_iota(jnp.int32, (nb, tq), 1)) // SEL_BLOCK
    forced = (j == 0) | (j == jt) | (j == jt - 1)
    sel_t = jnp.where(forced, 1.0, 0.0)
    vals = jnp.where(forced, -3e38, jnp.where(j > jt, -FORCE_SCORE, imp))
    jf = j.astype(F32)
    for _ in range(min(SEL_TOPK, nb) - 3):
        best = jnp.max(vals, axis=0, keepdims=True)
        first = jnp.min(jnp.where(vals == best, jf, float(nb)), axis=0, keepdims=True)
        hit = jf == first
        sel_t = jnp.where(hit, 1.0, sel_t)
        vals = jnp.where(hit, -3e38, vals)

    not_chosen = ((1.0 - sel_t) * MASK_BIAS).T.astype(BF16)
    qaug_scr[:, :LANES] = qa
    for r in range(HEADS_PER_GROUP):
        qaug_scr[r * tq:(r + 1) * tq, LANES:LANES + nb] = not_chosen
    if nb < LANES:
        qaug_scr[:, LANES + nb:] = jnp.zeros((rows, LANES - nb), BF16)
    ntile = tk // LANES

    def tile_scores(kt, causal):
        kb = pl.multiple_of(kt * tk, tk)
        sc = _dot_nt(qaug_scr[...], ksa_ref[pl.ds(kb, tk), :])
        if causal:
            key = kb + lax.broadcasted_iota(jnp.int32, (rows, tk), 1)
            sc = jnp.where(jnp.concatenate([t_l] * ntile, axis=1) >= key, sc, NEG_INF)
        return kb, sc

    def lane_max(sc):
        part = sc[:, :LANES]
        for c in range(1, ntile):
            part = jnp.maximum(part, sc[:, c * LANES:(c + 1) * LANES])
        return part

    def value_tile(kt, causal, track):
        kb, sc = tile_scores(kt, causal)
        if track:
            top_scr[...] = jnp.maximum(top_scr[...], lane_max(sc))
        pr = jnp.exp(sc - jnp.concatenate([m_scr[...]] * ntile, axis=1))
        acc_scr[...] += jnp.dot(pr.astype(BF16), vsa_ref[pl.ds(kb, tk), :], preferred_element_type=F32)

    blocks_per_tile = tk // SEL_BLOCK
    for i in range(seq // tk):
        chosen_here = jnp.max(sel_t[i * blocks_per_tile:(i + 1) * blocks_per_tile, :])
        used_smem[i] = (chosen_here > 0.5).astype(jnp.int32)
    n_before = q0 // tk

    def value_sweep(track):
        acc_scr[...] = jnp.zeros((rows, LANES), F32)

        def step(kt, carry):
            @pl.when(used_smem[kt] > 0)
            def _():
                value_tile(kt, False, track)
            return carry
        lax.fori_loop(0, n_before, step, 0)
        value_tile(n_before, True, False)

    _, sc_diag = tile_scores(n_before, True)
    m_scr[...] = jnp.broadcast_to(jnp.max(lane_max(sc_diag), axis=-1, keepdims=True), (rows, LANES))
    top_scr[...] = jnp.full((rows, LANES), NEG_INF, F32)
    value_sweep(True)

    @pl.when(jnp.max(top_scr[...] - m_scr[...]) > EXP_HEADROOM)
    def _():
        exact = jnp.maximum(top_scr[...], m_scr[...])
        m_scr[...] = jnp.broadcast_to(jnp.max(exact, axis=-1, keepdims=True), (rows, LANES))
        value_sweep(False)

    acc = acc_scr[...]
    o_s = acc[:, :HEAD_DIM] / jnp.maximum(acc[:, HEAD_DIM:HEAD_DIM + 1], 1e-30)

    gate = jax.nn.sigmoid(gt_ref[...])
    for r in range(HEADS_PER_GROUP):
        sl = slice(r * tq, (r + 1) * tq)
        gc = lambda x: gate[:, x * HEADS_PER_GROUP + r:x * HEADS_PER_GROUP + r + 1]
        o_ref[:, r * HEAD_DIM:(r + 1) * HEAD_DIM] = gc(0) * o_c[sl] + gc(1) * o_s[sl] + gc(2) * o_w[sl]


def _attention(qa, ksa, vsa, kwa, vwa, kca, vca, gt, batch, seq):
    tq, tk = ATTN_TQ, ATTN_TK
    bg = batch * N_KV_GROUPS
    nq = seq // tq
    nb, nc = seq // SEL_BLOCK, seq // CMP_STRIDE
    assert SEL_TOPK <= nb <= LANES and seq % tk == 0 and tk % tq == 0 and tq % LANES == 0 and nc % LANES == 0
    rows = HEADS_PER_GROUP * tq
    gq = HEADS_PER_GROUP * HEAD_DIM
    cs = jnp.arange(nc)[None, :] * CMP_STRIDE
    ss = jnp.arange(nb)[:, None] * SEL_BLOCK
    ovt = ((cs <= ss + SEL_BLOCK - 1) & (cs + CMP_BLOCK - 1 >= ss)).astype(BF16)
    per_bg = lambda n, w: pl.BlockSpec((None, n, w), lambda i, j: (i, 0, 0))
    return pl.pallas_call(
        functools.partial(_attn_kernel, seq=seq, tq=tq, tk=tk),
        grid=(bg, nq),
        in_specs=[pl.BlockSpec((None, HEADS_PER_GROUP, tq, LANES), lambda i, j: (i, 0, j, 0)),
                  per_bg(seq, 2 * LANES), per_bg(seq, LANES), per_bg(seq, LANES), per_bg(seq, LANES),
                  per_bg(nc, LANES), per_bg(nc, LANES),
                  pl.BlockSpec((None, tq, 3 * HEADS_PER_GROUP), lambda i, j: (i, j, 0)),
                  pl.BlockSpec((nb, nc), lambda i, j: (0, 0))],
        out_specs=pl.BlockSpec((tq, gq), lambda i, j: ((i // N_KV_GROUPS) * nq + j, i % N_KV_GROUPS)),
        out_shape=jax.ShapeDtypeStruct((batch * seq, ATTN_WIDTH), F32),
        scratch_shapes=[pltpu.VMEM((rows, LANES), F32), pltpu.VMEM((rows, LANES), F32), pltpu.VMEM((rows, LANES), F32),
                        pltpu.VMEM((rows, 2 * LANES), BF16), pltpu.SMEM((seq // tk,), jnp.int32)],
        compiler_params=_params("parallel", "arbitrary"),
        name="nsa_attention",
    )(qa, ksa, vsa, kwa, vwa, kca, vca, gt, ovt)


def _alibi_columns(seq):
    dh = HEAD_DIM
    nb, nc = seq // SEL_BLOCK, seq // CMP_STRIDE
    head = jnp.arange(1, N_HEADS + 1, dtype=F32).reshape(N_KV_GROUPS, HEADS_PER_GROUP)
    parts = _split3(jnp.exp2(-8.0 * head / N_HEADS))
    qcols = jnp.stack([float(SEL_BLOCK) * p.astype(F32) for p in parts] + [p.astype(F32) for p in parts], axis=-1)
    qcols = jnp.pad(qcols, ((0, 0), (0, 0), (0, LANES - dh - qcols.shape[-1]))).astype(BF16)

    def key_cols(u, scale):
        cols = jnp.stack([u // SEL_BLOCK] * 3 + [u % SEL_BLOCK] * 3, axis=-1).astype(F32) * scale
        return jnp.pad(cols, ((0, 0), (0, LANES - dh - cols.shape[-1])))

    pos = jnp.arange(seq)
    onehot = (pos[:, None] // SEL_BLOCK == jnp.arange(LANES)[None, :]) & (jnp.arange(LANES) < nb)
    win_cols = key_cols(pos, 1.0).astype(BF16)
    sel_cols = jnp.concatenate([win_cols, onehot.astype(BF16)], axis=1)
    cmp_cols = key_cols(jnp.arange(nc), float(CMP_STRIDE)).astype(BF16)
    ones_cols = (jnp.arange(LANES - dh) == 0).astype(BF16)
    return qcols, sel_cols, win_cols, cmp_cols, ones_cols


def _nsa_layer(h, batch, seq, norm, w_in, k_pe, k_w1, k_w2, v_pe, v_w1, v_w2, w_out):
    gg = N_KV_GROUPS
    bg = batch * gg
    (qa, ksa, vsa, kwa, vwa, kvc, gt, z), cmp_cols, ones_cols = _nsa_in(h, norm, w_in, batch, seq)
    merge = lambda x: x.reshape((bg,) + x.shape[2:])
    with_cols = lambda x, cols: jnp.concatenate([x, jnp.broadcast_to(cols, x.shape[:-1] + cols.shape[-1:])], axis=-1)
    kc, vc = _compress(merge(kvc[0]), merge(kvc[1]), k_pe, k_w1, k_w2, v_pe, v_w1, v_w2)
    o = _attention(merge(qa), merge(ksa), merge(vsa), merge(kwa), merge(vwa),
                   with_cols(kc, cmp_cols), with_cols(vc, ones_cols), merge(gt), batch, seq)
    return _nsa_out(o, z, w_out, h)


def _split2(x):
    hi = x.astype(BF16)
    return hi, (x - hi.astype(F32)).astype(BF16)


def _dot3(a, b_hi, b_lo):
    a_hi, a_lo = _split2(a)
    return (jnp.dot(a_hi, b_hi, preferred_element_type=F32) + jnp.dot(a_lo, b_hi, preferred_element_type=F32)
            + jnp.dot(a_hi, b_lo, preferred_element_type=F32))


def _member(shape, axis, width):
    return (lax.broadcasted_iota(jnp.int32, shape, axis) // width) % 2


def _s5_state_kernel(u_ref, w_ref, xr_ref, xi_ref):
    w = w_ref[...]
    p = SSM_STATE
    wide = jnp.concatenate([w[:, :p], w[:, :p], w[:, p:], w[:, p:]], axis=1)
    w_pair = jnp.where(_member(wide.shape, 0, SSM_GROUP) == _member(wide.shape, 1, p), wide, 0.0)
    x = _dot3(u_ref[...], *_split2(w_pair))
    half = x.shape[1] // 2
    xr_ref[...] = x[:, :half]
    xi_ref[...] = x[:, half:]


def _s5_scan_kernel(xr_ref, xi_ref, ar_ref, ai_ref, cr_ref, ci_ref, *, batch, nchunk):
    ar, ai = ar_ref[...], ai_ref[...]
    width = ar.shape[1]

    def step(n, carry):
        new = []
        for b in range(batch):
            cr, ci = carry[2 * b], carry[2 * b + 1]
            idx = b * nchunk + n
            cr_ref[pl.ds(idx, 1), :] = cr
            ci_ref[pl.ds(idx, 1), :] = ci
            lr, li = xr_ref[pl.ds(idx, 1), :], xi_ref[pl.ds(idx, 1), :]
            new += [ar * cr - ai * ci + lr, ar * ci + ai * cr + li]
        return tuple(new)

    zero = jnp.zeros((1, width), F32)
    lax.fori_loop(0, nchunk, step, (zero,) * (2 * batch))


def _s5_y_kernel(u_ref, cr_ref, ci_ref, k_ref, v_ref, d_ref, y_ref, m_scr):
    k = k_ref[...]
    col = lax.broadcasted_iota(jnp.int32, k.shape, 1)
    m_scr[0:S5_PAIR_LANES, :] = k
    for s in range(1, S5_CHUNK):
        shifted = pltpu.roll(k, s * S5_PAIR_LANES, 1)
        m_scr[s * S5_PAIR_LANES:(s + 1) * S5_PAIR_LANES, :] = jnp.where(col >= s * S5_PAIR_LANES, shifted, 0.0)
    v = v_ref[...]
    p = SSM_STATE
    tall = jnp.concatenate([v[:p], v[:p], v[p:], v[p:]], axis=0)
    v_pair = jnp.where(_member(tall.shape, 0, p) == _member(tall.shape, 1, SSM_GROUP), tall, 0.0)
    u = u_ref[...]
    carry = jnp.concatenate([cr_ref[...], ci_ref[...]], axis=1)
    y = _dot3(u, *_split2(m_scr[...])) + _dot3(carry, *_split2(v_pair)) + d_ref[...] * u
    y_ref[...] = jax.nn.gelu(y)


def _s5_matrices(log_dt, lambda_re, lambda_im, b_re, b_im, c_re, c_im, d_skip):
    hp = lax.Precision.HIGHEST
    gn, pn, cn, ln = SSM_GROUPS, SSM_STATE, SSM_GROUP, S5_CHUNK
    dt = jnp.exp(log_dt.astype(F32))[:, None]
    lre = jnp.minimum(lambda_re.astype(F32), -1e-4)
    lim = lambda_im.astype(F32)
    mag = jnp.exp(lre * dt)
    ab_re, ab_im = mag * jnp.cos(lim * dt), mag * jnp.sin(lim * dt)
    den = lre * lre + lim * lim
    nr = ab_re - 1.0
    coef_re = (nr * lre + ab_im * lim) / den
    coef_im = (ab_im * lre - nr * lim) / den
    br32, bi32 = b_re.astype(F32), b_im.astype(F32)
    bb_re = coef_re[..., None] * br32 - coef_im[..., None] * bi32
    bb_im = coef_re[..., None] * bi32 + coef_im[..., None] * br32
    cr32, ci32 = c_re.astype(F32), c_im.astype(F32)
    pr, pi = [jnp.ones_like(ab_re)], [jnp.zeros_like(ab_re)]
    for _ in range(ln):
        pr, pi = pr + [pr[-1] * ab_re - pi[-1] * ab_im], pi + [pr[-1] * ab_im + pi[-1] * ab_re]
    pw_re, pw_im = jnp.stack(pr), jnp.stack(pi)
    ab_b_re = pw_re[:ln, :, :, None] * bb_re - pw_im[:ln, :, :, None] * bb_im
    ab_b_im = pw_re[:ln, :, :, None] * bb_im + pw_im[:ln, :, :, None] * bb_re
    kern = (jnp.einsum('gop,kgpi->kgoi', cr32, ab_b_re, precision=hp)
            - jnp.einsum('gop,kgpi->kgoi', ci32, ab_b_im, precision=hp))
    gp = gn // 2
    same = jnp.eye(2, dtype=F32)
    k_pair = kern.reshape(ln, gp, 2, cn, cn).transpose(1, 2, 4, 0, 3)
    k_pair = k_pair[:, :, :, :, None, :] * same[None, :, None, None, :, None]
    k_pair = k_pair.reshape(gp, S5_PAIR_LANES, S5_PAIR)
    w = jnp.stack([ab_b_re[::-1], ab_b_im[::-1]]).reshape(2, ln, gp, 2, pn, cn)
    w_pair = w.transpose(2, 1, 3, 5, 0, 4).reshape(gp, S5_PAIR, 2 * pn)
    ar1, ai1 = pw_re[1:], pw_im[1:]
    v_re = cr32[None] * ar1[:, :, None, :] - ci32[None] * ai1[:, :, None, :]
    v_im = -(cr32[None] * ai1[:, :, None, :] + ci32[None] * ar1[:, :, None, :])
    v = jnp.stack([v_re, v_im]).reshape(2, ln, gp, 2, cn, pn)
    v_pair = v.transpose(2, 0, 5, 1, 3, 4).reshape(gp, 2 * pn, S5_PAIR)
    d_pair = jnp.tile(d_skip.astype(F32).reshape(gp, 1, 2 * cn), (1, 1, ln))
    a_end_re = pw_re[ln].reshape(1, gn * pn)
    a_end_im = pw_im[ln].reshape(1, gn * pn)
    return k_pair, w_pair, v_pair, d_pair, a_end_re, a_end_im


def _s5_layer(h, batch, seq, norm, w_in, log_dt, lambda_re, lambda_im, b_re, b_im, c_re, c_im, d_skip,
              w_glu, w_out, final_norm, final):
    gp, ln, cn = SSM_GROUPS // 2, S5_CHUNK, SSM_GROUP
    nchunk = seq // ln
    rows = batch * nchunk
    state_w = 2 * SSM_STATE
    k_pair, w_pair, v_pair, d_pair, a_re, a_im = _s5_matrices(
        log_dt, lambda_re, lambda_im, b_re, b_im, c_re, c_im, d_skip)
    up, z = _s5_in(h, norm, w_in)
    pair3 = lambda a, b: pl.BlockSpec((None, a, b), lambda i: (i, 0, 0))
    slab = pl.BlockSpec((rows, state_w), lambda i: (0, i))
    xr, xi = pl.pallas_call(
        _s5_state_kernel,
        grid=(gp,),
        in_specs=[pair3(rows, S5_PAIR), pair3(S5_PAIR, state_w)],
        out_specs=[slab, slab],
        out_shape=[jax.ShapeDtypeStruct((rows, gp * state_w), F32)] * 2,
        compiler_params=_params("parallel"),
        name="s5_chunk_state",
    )(up, w_pair)
    scan_w = 4 * state_w
    wide = pl.BlockSpec((rows, scan_w), lambda i: (0, i))
    coef = pl.BlockSpec((1, scan_w), lambda i: (0, i))
    cr, ci = pl.pallas_call(
        functools.partial(_s5_scan_kernel, batch=batch, nchunk=nchunk),
        grid=(gp * state_w // scan_w,),
        in_specs=[wide, wide, coef, coef],
        out_specs=[wide, wide],
        out_shape=[jax.ShapeDtypeStruct((rows, gp * state_w), F32)] * 2,
        compiler_params=_params("parallel"),
        name="s5_carry_scan",
    )(xr, xi, a_re, a_im)
    yp = pl.pallas_call(
        _s5_y_kernel,
        grid=(gp,),
        in_specs=[pair3(rows, S5_PAIR), slab, slab, pair3(S5_PAIR_LANES, S5_PAIR), pair3(state_w, S5_PAIR),
                  pair3(1, S5_PAIR)],
        out_specs=pair3(rows, S5_PAIR),
        out_shape=jax.ShapeDtypeStruct((gp, rows, S5_PAIR), F32),
        scratch_shapes=[pltpu.VMEM((S5_PAIR, S5_PAIR), F32)],
        compiler_params=_params("parallel"),
        name="s5_chunk_output",
    )(up, cr, ci, k_pair, v_pair, d_pair)
    return _s5_out(yp, z, w_glu, w_out, h, final_norm, final)


def kernel(x, l0_norm, l0_w_in, l0_cmp_k_pe, l0_cmp_k_w1, l0_cmp_k_w2, l0_cmp_v_pe, l0_cmp_v_w1, l0_cmp_v_w2, l0_w_out, l1_norm, l1_w_in, l1_log_dt, l1_lambda_re, l1_lambda_im, l1_b_re, l1_b_im, l1_c_re, l1_c_im, l1_d, l1_w_glu, l1_w_out, l2_norm, l2_w_in, l2_cmp_k_pe, l2_cmp_k_w1, l2_cmp_k_w2, l2_cmp_v_pe, l2_cmp_v_w1, l2_cmp_v_w2, l2_w_out, l3_norm, l3_w_in, l3_log_dt, l3_lambda_re, l3_lambda_im, l3_b_re, l3_b_im, l3_c_re, l3_c_im, l3_d, l3_w_glu, l3_w_out, final_norm):
    batch, seq, _ = x.shape
    h = x.reshape(batch * seq, D_MODEL)
    h = _nsa_layer(h, batch, seq, l0_norm, l0_w_in, l0_cmp_k_pe, l0_cmp_k_w1, l0_cmp_k_w2,
                   l0_cmp_v_pe, l0_cmp_v_w1, l0_cmp_v_w2, l0_w_out)
    h = _s5_layer(h, batch, seq, l1_norm, l1_w_in, l1_log_dt, l1_lambda_re, l1_lambda_im, l1_b_re, l1_b_im,
                  l1_c_re, l1_c_im, l1_d, l1_w_glu, l1_w_out, final_norm, False)
    h = _nsa_layer(h, batch, seq, l2_norm, l2_w_in, l2_cmp_k_pe, l2_cmp_k_w1, l2_cmp_k_w2,
                   l2_cmp_v_pe, l2_cmp_v_w1, l2_cmp_v_w2, l2_w_out)
    h = _s5_layer(h, batch, seq, l3_norm, l3_w_in, l3_log_dt, l3_lambda_re, l3_lambda_im, l3_b_re, l3_b_im,
                  l3_c_re, l3_c_im, l3_d, l3_w_glu, l3_w_out, final_norm, True)
    return h.reshape(batch, seq, D_MODEL)
```

```python
import functools

import jax
import jax.numpy as jnp
from jax import lax
from jax.experimental import pallas as pl
from jax.experimental.pallas import tpu as pltpu

F32 = jnp.float32
BF16 = jnp.bfloat16

D_MODEL = 1024
EPS = 1e-6
NEG_INF = -1e30
FORCE_SCORE = 1e9

N_HEADS = 16
HEAD_DIM = 64
N_KV_GROUPS = 4
HEADS_PER_GROUP = N_HEADS // N_KV_GROUPS
ATTN_WIDTH = N_HEADS * HEAD_DIM
KV_WIDTH = N_KV_GROUPS * HEAD_DIM
CMP_BLOCK = 32
CMP_STRIDE = 16
CMP_HIDDEN = 256
SEL_BLOCK = 64
SEL_TOPK = 16
WINDOW = 512
N_GATES = 3 * N_HEADS
GATES_PAD = 128

SSM_WIDTH = D_MODEL
SSM_GROUP = 16
SSM_GROUPS = SSM_WIDTH // SSM_GROUP
SSM_STATE = 64
S5_CHUNK = 16
S5_PAIRS = SSM_GROUPS // 2
S5_PAIR_LANES = 2 * SSM_GROUP
S5_PAIR = S5_CHUNK * S5_PAIR_LANES

ROW_TILE = 512
ATTN_TQ = 256
ATTN_TK = 512
LANES = 128
MASK_BIAS = -2.0 ** 100
EXP_HEADROOM = 60.0
VMEM_LIMIT = 56 * 1024 * 1024


def _params(*sem):
    return pltpu.CompilerParams(dimension_semantics=sem, vmem_limit_bytes=VMEM_LIMIT)


def _rms(x, g):
    return x * lax.rsqrt(jnp.mean(x * x, axis=-1, keepdims=True) + EPS) * g


def _silu(x):
    return x * jax.nn.sigmoid(x)


def _nsa_in_kernel(h_ref, g_ref, w_ref, qcols_ref, selc_ref, winc_ref, ones_ref,
                   qa_ref, ksa_ref, vsa_ref, kwa_ref, vwa_ref, kvc_ref, gt_ref, z_ref, c_scr):
    gg, rr, dh = N_KV_GROUPS, HEADS_PER_GROUP, HEAD_DIM
    xn = _rms(h_ref[...], g_ref[...])
    y = jnp.dot(xn.astype(BF16), w_ref[...], preferred_element_type=F32)
    for g in range(gg):
        for r in range(rr):
            k = g * rr + r
            qa_ref[g, r] = jnp.broadcast_to(qcols_ref[g, r:r + 1, :], (ROW_TILE, LANES))
            qa_ref[g, r, :, :dh] = (y[:, k * dh:(k + 1) * dh] * (dh ** -0.5)).astype(BF16)
    o = ATTN_WIDTH + 2 * KV_WIDTH
    consts = (selc_ref[...], jnp.broadcast_to(ones_ref[...], (ROW_TILE, LANES)), winc_ref[...],
              jnp.broadcast_to(ones_ref[...], (ROW_TILE, LANES)))
    for kind, ref in enumerate((ksa_ref, vsa_ref, kwa_ref, vwa_ref)):
        for g in range(gg):
            c = o + (kind * gg + g) * dh
            ref[g] = consts[kind]
            ref[g, :, :dh] = y[:, c:c + dh].astype(BF16)
    lane = lax.broadcasted_iota(jnp.int32, (ROW_TILE // CMP_STRIDE, LANES), 1)
    for n in range(2 * KV_WIDTH // LANES):
        c_scr[n] = y[:, ATTN_WIDTH + n * LANES:ATTN_WIDTH + (n + 1) * LANES]
        kind, g0 = divmod(2 * n, gg)
        for a in range(CMP_STRIDE // 2):
            even = c_scr[n, pl.ds(2 * a, ROW_TILE // CMP_STRIDE, stride=CMP_STRIDE), :]
            odd = c_scr[n, pl.ds(2 * a + 1, ROW_TILE // CMP_STRIDE, stride=CMP_STRIDE), :]
            kvc_ref[kind, g0, :, a * LANES:(a + 1) * LANES] = jnp.where(lane < dh, even, pltpu.roll(odd, dh, 1))
            kvc_ref[kind, g0 + 1, :, a * LANES:(a + 1) * LANES] = jnp.where(lane < dh, pltpu.roll(even, dh, 1), odd)
    o += 4 * KV_WIDTH
    z_ref[...] = y[:, o:o + ATTN_WIDTH]
    o += ATTN_WIDTH
    per_group = 3 * rr
    for g in range(gg):
        gt_ref[g] = y[:, o + g * per_group:o + (g + 1) * per_group]


def _nsa_in(h, norm, w_in, batch, seq):
    gg, rr, dh = N_KV_GROUPS, HEADS_PER_GROUP, HEAD_DIM
    nt = seq // ROW_TILE
    nc_tile = ROW_TILE // CMP_STRIDE
    a, b = ATTN_WIDTH + 6 * KV_WIDTH, ATTN_WIDTH + 6 * KV_WIDTH + N_GATES
    w_gates = w_in[:, a:b].reshape(D_MODEL, 3, gg, rr).transpose(0, 2, 1, 3).reshape(D_MODEL, N_GATES)
    w = jnp.concatenate([w_in[:, :a], w_in[:, b:], w_gates,
                         jnp.zeros((D_MODEL, GATES_PAD - N_GATES), w_in.dtype)], axis=1).astype(BF16)
    n = w.shape[1]
    qcols, sel_cols, win_cols, cmp_cols, ones_cols = _alibi_columns(seq)
    lead = lambda cols: jnp.pad(cols, [(0, 0)] * (cols.ndim - 1) + [(dh, 0)])
    const = lambda shape: pl.BlockSpec(shape, lambda i: (0,) * len(shape))
    by_pos = lambda width: pl.BlockSpec((ROW_TILE, width), lambda i: (i % nt, 0))
    per_b = lambda *tail: pl.BlockSpec((None, gg) + tail, lambda i: (i // nt, 0) + (0,) * (len(tail) - 2) + (i % nt, 0))
    outs = pl.pallas_call(
        _nsa_in_kernel,
        grid=(batch * nt,),
        in_specs=[pl.BlockSpec((ROW_TILE, D_MODEL), lambda i: (i, 0)), const((1, D_MODEL)), const((D_MODEL, n)),
                  const((gg, rr, LANES)), by_pos(2 * LANES), by_pos(LANES), const((1, LANES))],
        out_specs=[per_b(rr, ROW_TILE, LANES), per_b(ROW_TILE, 2 * LANES), per_b(ROW_TILE, LANES),
                   per_b(ROW_TILE, LANES), per_b(ROW_TILE, LANES),
                   pl.BlockSpec((2, None, gg, nc_tile, CMP_STRIDE * dh), lambda i: (0, i // nt, 0, i % nt, 0)),
                   per_b(ROW_TILE, 3 * rr), pl.BlockSpec((ROW_TILE, ATTN_WIDTH), lambda i: (i, 0))],
        out_shape=[jax.ShapeDtypeStruct((batch, gg, rr, seq, LANES), BF16),
                   jax.ShapeDtypeStruct((batch, gg, seq, 2 * LANES), BF16),
                   jax.ShapeDtypeStruct((batch, gg, seq, LANES), BF16),
                   jax.ShapeDtypeStruct((batch, gg, seq, LANES), BF16),
                   jax.ShapeDtypeStruct((batch, gg, seq, LANES), BF16),
                   jax.ShapeDtypeStruct((2, batch, gg, seq // CMP_STRIDE, CMP_STRIDE * dh), F32),
                   jax.ShapeDtypeStruct((batch, gg, seq, 3 * rr), F32),
                   jax.ShapeDtypeStruct((batch * seq, ATTN_WIDTH), F32)],
        scratch_shapes=[pltpu.VMEM((2 * KV_WIDTH // LANES, ROW_TILE, LANES), F32)],
        compiler_params=_params("parallel"),
        name="nsa_in_proj",
    )(h, norm.reshape(1, D_MODEL), w, lead(qcols), lead(sel_cols), lead(win_cols), lead(ones_cols)[None])
    return outs, cmp_cols, ones_cols


def _quarter_exchange(parts):
    quarter = lax.broadcasted_iota(jnp.int32, parts[0].shape, 1) // S5_PAIR_LANES
    outs = []
    for b in range(4):
        acc = None
        for a in range(4):
            moved = parts[a] if a == b else pltpu.roll(parts[a], ((a - b) % 4) * S5_PAIR_LANES, 1)
            acc = moved if acc is None else jnp.where(quarter == a, moved, acc)
        outs.append(acc)
    return outs


def _s5_in_kernel(h_ref, g_ref, w_ref, u_ref, z_ref, u_scr):
    xn = _rms(h_ref[...], g_ref[...])
    y = jnp.dot(xn.astype(BF16), w_ref[...], preferred_element_type=F32)
    z_ref[...] = y[:, SSM_WIDTH:]
    nch = ROW_TILE // S5_CHUNK
    for m in range(SSM_WIDTH // 128):
        u_scr[m] = y[:, m * 128:(m + 1) * 128]
        for k in range(S5_CHUNK // 4):
            steps = [u_scr[m, pl.ds(4 * k + a, nch, stride=S5_CHUNK), :] for a in range(4)]
            for b, tile in enumerate(_quarter_exchange(steps)):
                u_ref[4 * m + b, :, k * 128:(k + 1) * 128] = tile


def _s5_in(h, norm, w_in):
    t = h.shape[0]
    nch = ROW_TILE // S5_CHUNK
    row = lambda width: pl.BlockSpec((ROW_TILE, width), lambda i: (i, 0))
    return pl.pallas_call(
        _s5_in_kernel,
        grid=(t // ROW_TILE,),
        in_specs=[row(D_MODEL), pl.BlockSpec((1, D_MODEL), lambda i: (0, 0)),
                  pl.BlockSpec((D_MODEL, 2 * SSM_WIDTH), lambda i: (0, 0))],
        out_specs=[pl.BlockSpec((S5_PAIRS, nch, S5_PAIR), lambda i: (0, i, 0)), row(SSM_WIDTH)],
        out_shape=[jax.ShapeDtypeStruct((S5_PAIRS, t // S5_CHUNK, S5_PAIR), F32),
                   jax.ShapeDtypeStruct((t, SSM_WIDTH), F32)],
        scratch_shapes=[pltpu.VMEM((SSM_WIDTH // 128, ROW_TILE, 128), F32)],
        compiler_params=_params("parallel"),
        name="s5_in_proj",
    )(h, norm.reshape(1, D_MODEL), w_in.astype(BF16))


def _nsa_out_kernel(o_ref, z_ref, w_ref, res_ref, out_ref):
    a = o_ref[...] * _silu(z_ref[...])
    out_ref[...] = res_ref[...] + jnp.dot(a.astype(BF16), w_ref[...], preferred_element_type=F32)


def _nsa_out(o, z, w_out, res):
    t = o.shape[0]
    row = pl.BlockSpec((ROW_TILE, D_MODEL), lambda i: (i, 0))
    return pl.pallas_call(
        _nsa_out_kernel,
        grid=(t // ROW_TILE,),
        in_specs=[row, row, pl.BlockSpec((ATTN_WIDTH, D_MODEL), lambda i: (0, 0)), row],
        out_specs=row,
        out_shape=jax.ShapeDtypeStruct((t, D_MODEL), F32),
        compiler_params=_params("parallel"),
        name="nsa_out_proj",
    )(o, z, w_out.astype(BF16), res)


def _s5_out_kernel(y_ref, z_ref, wg_ref, wo_ref, res_ref, fn_ref, out_ref, y_scr, *, final):
    nch = ROW_TILE // S5_CHUNK
    for m in range(SSM_WIDTH // 128):
        for k in range(S5_CHUNK // 4):
            pairs = [y_ref[4 * m + b, :, k * 128:(k + 1) * 128] for b in range(4)]
            for a, tile in enumerate(_quarter_exchange(pairs)):
                y_scr[m, pl.ds(4 * k + a, nch, stride=S5_CHUNK), :] = tile
    y = jnp.concatenate([y_scr[m] for m in range(SSM_WIDTH // 128)], axis=1)
    gl = jnp.dot(y.astype(BF16), wg_ref[...], preferred_element_type=F32)
    v = gl[:, :SSM_WIDTH] * jax.nn.sigmoid(gl[:, SSM_WIDTH:])
    v = v * _silu(z_ref[...])
    h = res_ref[...] + jnp.dot(v.astype(BF16), wo_ref[...], preferred_element_type=F32)
    out_ref[...] = _rms(h, fn_ref[...]) if final else h


def _s5_out(y, z, w_glu, w_out, res, final_norm, final):
    t = z.shape[0]
    row = pl.BlockSpec((ROW_TILE, D_MODEL), lambda i: (i, 0))
    return pl.pallas_call(
        functools.partial(_s5_out_kernel, final=final),
        grid=(t // ROW_TILE,),
        in_specs=[pl.BlockSpec((S5_PAIRS, ROW_TILE // S5_CHUNK, S5_PAIR), lambda i: (0, i, 0)), row,
                  pl.BlockSpec((SSM_WIDTH, 2 * SSM_WIDTH), lambda i: (0, 0)),
                  pl.BlockSpec((SSM_WIDTH, D_MODEL), lambda i: (0, 0)), row,
                  pl.BlockSpec((1, D_MODEL), lambda i: (0, 0))],
        out_specs=row,
        out_shape=jax.ShapeDtypeStruct((t, D_MODEL), F32),
        scratch_shapes=[pltpu.VMEM((SSM_WIDTH // 128, ROW_TILE, 128), F32)],
        compiler_params=_params("parallel"),
        name="s5_glu_out_proj",
    )(y, z, w_glu.astype(BF16), w_out.astype(BF16), res, final_norm.reshape(1, D_MODEL))


def _cmp_one(a_ref, pe_ref, w1_ref, w2_ref, out_ref):
    nc = a_ref.shape[0]
    a = a_ref[...]
    pe = pe_ref[...]
    h_top = jnp.dot((a + pe[0:1]).astype(BF16), w1_ref[0], preferred_element_type=F32)
    h_bot = jnp.dot((a + pe[1:2]).astype(BF16), w1_ref[1], preferred_element_type=F32)
    hid = _silu(h_top + pltpu.roll(h_bot, nc - 1, 0))
    out = jnp.dot(hid.astype(BF16), w2_ref[...], preferred_element_type=F32)
    keep = lax.broadcasted_iota(jnp.int32, out.shape, 0) < nc - 1
    out_ref[...] = jnp.where(keep, out, 0.0).astype(BF16)


def _cmp_kernel(ka_ref, va_ref, kpe_ref, kw1_ref, kw2_ref, vpe_ref, vw1_ref, vw2_ref, kc_ref, vc_ref):
    _cmp_one(ka_ref, kpe_ref, kw1_ref, kw2_ref, kc_ref)
    _cmp_one(va_ref, vpe_ref, vw1_ref, vw2_ref, vc_ref)


def _compress(ka, va, k_pe, k_w1, k_w2, v_pe, v_w1, v_w2):
    bg, nc, half = ka.shape
    blk = pl.BlockSpec((None, nc, half), lambda i: (i, 0, 0))
    full = lambda shape: pl.BlockSpec(shape, lambda i: (0,) * len(shape))
    prep = lambda pe, w1, w2: (pe.reshape(2, half), w1.reshape(2, half, CMP_HIDDEN).astype(BF16), w2.astype(BF16))
    out = pl.BlockSpec((None, nc, HEAD_DIM), lambda i: (i, 0, 0))
    wspecs = [full((2, half)), full((2, half, CMP_HIDDEN)), full((CMP_HIDDEN, HEAD_DIM))]
    return pl.pallas_call(
        _cmp_kernel,
        grid=(bg,),
        in_specs=[blk, blk] + wspecs + wspecs,
        out_specs=[out, out],
        out_shape=[jax.ShapeDtypeStruct((bg, nc, HEAD_DIM), BF16)] * 2,
        compiler_params=_params("parallel"),
        name="nsa_compress",
    )(ka, va, *prep(k_pe, k_w1, k_w2), *prep(v_pe, v_w1, v_w2))


def _dot_nt(a, b):
    return lax.dot_general(a, b, (((1,), (1,)), ((), ())), preferred_element_type=F32)


def _split3(x):
    hi = x.astype(BF16)
    r1 = x - hi.astype(F32)
    mid = r1.astype(BF16)
    lo = (r1 - mid.astype(F32)).astype(BF16)
    return hi, mid, lo


def _softmax_values(parts, m_row, va):
    p = [jnp.exp(sc - m_row) for sc in parts]
    acc = jnp.dot(jnp.concatenate(p, axis=1).astype(BF16), va, preferred_element_type=F32)
    return acc, p


def _row_max(parts):
    m = parts[0]
    for sc in parts[1:]:
        m = jnp.maximum(m, sc)
    return jnp.broadcast_to(jnp.max(m, axis=-1, keepdims=True), m.shape)


def _attn_kernel(qa_ref, ksa_ref, vsa_ref, kwa_ref, vwa_ref, kca_ref, vca_ref, gt_ref, ovt_ref,
                 o_ref, m_scr, top_scr, acc_scr, qaug_scr, used_smem, *, seq, tq, tk):
    nb = seq // SEL_BLOCK
    nc = seq // CMP_STRIDE
    rows = HEADS_PER_GROUP * tq
    q0 = pl.program_id(1) * tq

    qa = qa_ref[...].reshape(rows, LANES)
    row_l = lax.broadcasted_iota(jnp.int32, (rows, LANES), 0)
    lane = lax.broadcasted_iota(jnp.int32, (rows, LANES), 1)
    t_l = q0 + (row_l - (row_l // tq) * tq)

    sc_all = _dot_nt(qa, kca_ref[...])
    last_ok = (t_l - (CMP_BLOCK - 1)) >> 4
    parts = [jnp.where(lane + c * LANES <= last_ok, sc_all[:, c * LANES:(c + 1) * LANES], NEG_INF)
             for c in range(nc // LANES)]
    m_c = jnp.where(t_l >= CMP_BLOCK - 1, _row_max(parts), -NEG_INF)
    acc_c, p_c = _softmax_values(parts, m_c, vca_ref[...])
    inv_c = jnp.broadcast_to(1.0 / jnp.maximum(acc_c[:, HEAD_DIM:HEAD_DIM + 1], 1e-30), (rows, LANES))
    o_c = acc_c[:, :HEAD_DIM] * inv_c[:, :HEAD_DIM]
    p = jnp.concatenate([pc * inv_c for pc in p_c], axis=1)

    wlen = WINDOW + tq
    w0 = pl.multiple_of(jnp.maximum(q0 - WINDOW, 0), tq)
    sw_all = _dot_nt(qa, kwa_ref[pl.ds(w0, wlen), :])
    parts = []
    for c in range(wlen // LANES):
        key = w0 + c * LANES + lane
        ok = key <= t_l
        if c * LANES < tq:
            ok = ok & (key > t_l - WINDOW)
        parts.append(jnp.where(ok, sw_all[:, c * LANES:(c + 1) * LANES], NEG_INF))
    acc_w, _ = _softmax_values(parts, _row_max(parts), vwa_ref[pl.ds(w0, wlen), :])
    o_w = acc_w[:, :HEAD_DIM] / jnp.maximum(acc_w[:, HEAD_DIM:HEAD_DIM + 1], 1e-30)

    psum = p[0:tq] + p[tq:2 * tq] + p[2 * tq:3 * tq] + p[3 * tq:4 * tq]
    ovt = ovt_ref[...]
    imp = sum(_dot_nt(ovt, part) for part in _split3(psum))
    j = lax.broadcasted_iota(jnp.int32, (nb, tq), 0)
    jt = (q0 + lax.broadcasted_iota(jnp.int32, (nb, tq), 1)) // SEL_BLOCK
    forced = (j == 0) | (j == jt) | (j == jt - 1)
    sel_t = jnp.where(forced, 1.0, 0.0)
    vals = jnp.where(forced, -3e38, jnp.where(j > jt, -FORCE_SCORE, imp))
    jf = j.astype(F32)
    for _ in range(min(SEL_TOPK, nb) - 3):
        best = jnp.max(vals, axis=0, keepdims=True)
        first = jnp.min(jnp.where(vals == best, jf, float(nb)), axis=0, keepdims=True)
        hit = jf == first
        sel_t = jnp.where(hit, 1.0, sel_t)
        vals = jnp.where(hit, -3e38, vals)

    not_chosen = ((1.0 - sel_t) * MASK_BIAS).T.astype(BF16)
    qaug_scr[:, :LANES] = qa
    for r in range(HEADS_PER_GROUP):
        qaug_scr[r * tq:(r + 1) * tq, LANES:LANES + nb] = not_chosen
    if nb < LANES:
        qaug_scr[:, LANES + nb:] = jnp.zeros((rows, LANES - nb), BF16)
    ntile = tk // LANES

    def tile_scores(kt, causal):
        kb = pl.multiple_of(kt * tk, tk)
        sc = _dot_nt(qaug_scr[...], ksa_ref[pl.ds(kb, tk), :])
        if causal:
            key = kb + lax.broadcasted_iota(jnp.int32, (rows, tk), 1)
            sc = jnp.where(jnp.concatenate([t_l] * ntile, axis=1) >= key, sc, NEG_INF)
        return kb, sc

    def lane_max(sc):
        part = sc[:, :LANES]
        for c in range(1, ntile):
            part = jnp.maximum(part, sc[:, c * LANES:(c + 1) * LANES])
        return part

    def value_tile(kt, causal, track):
        kb, sc = tile_scores(kt, causal)
        if track:
            top_scr[...] = jnp.maximum(top_scr[...], lane_max(sc))
        pr = jnp.exp(sc - jnp.concatenate([m_scr[...]] * ntile, axis=1))
        acc_scr[...] += jnp.dot(pr.astype(BF16), vsa_ref[pl.ds(kb, tk), :], preferred_element_type=F32)

    blocks_per_tile = tk // SEL_BLOCK
    for i in range(seq // tk):
        chosen_here = jnp.max(sel_t[i * blocks_per_tile:(i + 1) * blocks_per_tile, :])
        used_smem[i] = (chosen_here > 0.5).astype(jnp.int32)
    n_before = q0 // tk

    def tiles_before(track):
        def step(kt, carry):
            @pl.when(used_smem[kt] > 0)
            def _():
                value_tile(kt, False, track)
            return carry
        lax.fori_loop(0, n_before, step, 0)

    kb_diag, sc_diag = tile_scores(n_before, True)
    m_diag = jnp.broadcast_to(jnp.max(lane_max(sc_diag), axis=-1, keepdims=True), (rows, LANES))
    m_scr[...] = m_diag
    pr_diag = jnp.exp(sc_diag - jnp.concatenate([m_diag] * ntile, axis=1))
    acc_scr[...] = jnp.dot(pr_diag.astype(BF16), vsa_ref[pl.ds(kb_diag, tk), :], preferred_element_type=F32)
    top_scr[...] = jnp.full((rows, LANES), NEG_INF, F32)
    tiles_before(True)

    @pl.when(jnp.max(top_scr[...] - m_scr[...]) > EXP_HEADROOM)
    def _():
        exact = jnp.maximum(top_scr[...], m_scr[...])
        m_scr[...] = jnp.broadcast_to(jnp.max(exact, axis=-1, keepdims=True), (rows, LANES))
        acc_scr[...] = jnp.zeros((rows, LANES), F32)
        tiles_before(False)
        value_tile(n_before, True, False)

    acc = acc_scr[...]
    o_s = acc[:, :HEAD_DIM] / jnp.maximum(acc[:, HEAD_DIM:HEAD_DIM + 1], 1e-30)

    gate = jax.nn.sigmoid(gt_ref[...])
    for r in range(HEADS_PER_GROUP):
        sl = slice(r * tq, (r + 1) * tq)
        gc = lambda x: gate[:, x * HEADS_PER_GROUP + r:x * HEADS_PER_GROUP + r + 1]
        o_ref[:, r * HEAD_DIM:(r + 1) * HEAD_DIM] = gc(0) * o_c[sl] + gc(1) * o_s[sl] + gc(2) * o_w[sl]


def _attention(qa, ksa, vsa, kwa, vwa, kca, vca, gt, batch, seq):
    tq, tk = ATTN_TQ, ATTN_TK
    bg = batch * N_KV_GROUPS
    nq = seq // tq
    nb, nc = seq // SEL_BLOCK, seq // CMP_STRIDE
    assert SEL_TOPK <= nb <= LANES and seq % tk == 0 and tk % tq == 0 and tq % LANES == 0 and nc % LANES == 0
    rows = HEADS_PER_GROUP * tq
    gq = HEADS_PER_GROUP * HEAD_DIM
    cs = jnp.arange(nc)[None, :] * CMP_STRIDE
    ss = jnp.arange(nb)[:, None] * SEL_BLOCK
    ovt = ((cs <= ss + SEL_BLOCK - 1) & (cs + CMP_BLOCK - 1 >= ss)).astype(BF16)
    per_bg = lambda n, w: pl.BlockSpec((None, n, w), lambda i, j: (i, 0, 0))
    return pl.pallas_call(
        functools.partial(_attn_kernel, seq=seq, tq=tq, tk=tk),
        grid=(bg, nq),
        in_specs=[pl.BlockSpec((None, HEADS_PER_GROUP, tq, LANES), lambda i, j: (i, 0, j, 0)),
                  per_bg(seq, 2 * LANES), per_bg(seq, LANES), per_bg(seq, LANES), per_bg(seq, LANES),
                  per_bg(nc, LANES), per_bg(nc, LANES),
                  pl.BlockSpec((None, tq, 3 * HEADS_PER_GROUP), lambda i, j: (i, j, 0)),
                  pl.BlockSpec((nb, nc), lambda i, j: (0, 0))],
        out_specs=pl.BlockSpec((tq, gq), lambda i, j: ((i // N_KV_GROUPS) * nq + j, i % N_KV_GROUPS)),
        out_shape=jax.ShapeDtypeStruct((batch * seq, ATTN_WIDTH), F32),
        scratch_shapes=[pltpu.VMEM((rows, LANES), F32), pltpu.VMEM((rows, LANES), F32), pltpu.VMEM((rows, LANES), F32),
                        pltpu.VMEM((rows, 2 * LANES), BF16), pltpu.SMEM((seq // tk,), jnp.int32)],
        compiler_params=_params("parallel", "arbitrary"),
        name="nsa_attention",
    )(qa, ksa, vsa, kwa, vwa, kca, vca, gt, ovt)


def _alibi_columns(seq):
    dh = HEAD_DIM
    nb, nc = seq // SEL_BLOCK, seq // CMP_STRIDE
    head = jnp.arange(1, N_HEADS + 1, dtype=F32).reshape(N_KV_GROUPS, HEADS_PER_GROUP)
    parts = _split3(jnp.exp2(-8.0 * head / N_HEADS))
    qcols = jnp.stack([float(SEL_BLOCK) * p.astype(F32) for p in parts] + [p.astype(F32) for p in parts], axis=-1)
    qcols = jnp.pad(qcols, ((0, 0), (0, 0), (0, LANES - dh - qcols.shape[-1]))).astype(BF16)

    def key_cols(u, scale):
        cols = jnp.stack([u // SEL_BLOCK] * 3 + [u % SEL_BLOCK] * 3, axis=-1).astype(F32) * scale
        return jnp.pad(cols, ((0, 0), (0, LANES - dh - cols.shape[-1])))

    pos = jnp.arange(seq)
    onehot = (pos[:, None] // SEL_BLOCK == jnp.arange(LANES)[None, :]) & (jnp.arange(LANES) < nb)
    win_cols = key_cols(pos, 1.0).astype(BF16)
    sel_cols = jnp.concatenate([win_cols, onehot.astype(BF16)], axis=1)
    cmp_cols = key_cols(jnp.arange(nc), float(CMP_STRIDE)).astype(BF16)
    ones_cols = (jnp.arange(LANES - dh) == 0).astype(BF16)
    return qcols, sel_cols, win_cols, cmp_cols, ones_cols


def _nsa_layer(h, batch, seq, norm, w_in, k_pe, k_w1, k_w2, v_pe, v_w1, v_w2, w_out):
    gg = N_KV_GROUPS
    bg = batch * gg
    (qa, ksa, vsa, kwa, vwa, kvc, gt, z), cmp_cols, ones_cols = _nsa_in(h, norm, w_in, batch, seq)
    merge = lambda x: x.reshape((bg,) + x.shape[2:])
    with_cols = lambda x, cols: jnp.concatenate([x, jnp.broadcast_to(cols, x.shape[:-1] + cols.shape[-1:])], axis=-1)
    kc, vc = _compress(merge(kvc[0]), merge(kvc[1]), k_pe, k_w1, k_w2, v_pe, v_w1, v_w2)
    o = _attention(merge(qa), merge(ksa), merge(vsa), merge(kwa), merge(vwa),
                   with_cols(kc, cmp_cols), with_cols(vc, ones_cols), merge(gt), batch, seq)
    return _nsa_out(o, z, w_out, h)


def _split2(x):
    hi = x.astype(BF16)
    return hi, (x - hi.astype(F32)).astype(BF16)


def _dot3(a, b_hi, b_lo):
    a_hi, a_lo = _split2(a)
    return (jnp.dot(a_hi, b_hi, preferred_element_type=F32) + jnp.dot(a_lo, b_hi, preferred_element_type=F32)
            + jnp.dot(a_hi, b_lo, preferred_element_type=F32))


def _member(shape, axis, width):
    return (lax.broadcasted_iota(jnp.int32, shape, axis) // width) % 2


def _s5_state_kernel(u_ref, w_ref, xr_ref, xi_ref):
    w = w_ref[...]
    p = SSM_STATE
    wide = jnp.concatenate([w[:, :p], w[:, :p], w[:, p:], w[:, p:]], axis=1)
    w_pair = jnp.where(_member(wide.shape, 0, SSM_GROUP) == _member(wide.shape, 1, p), wide, 0.0)
    x = _dot3(u_ref[...], *_split2(w_pair))
    half = x.shape[1] // 2
    xr_ref[...] = x[:, :half]
    xi_ref[...] = x[:, half:]


def _s5_scan_kernel(xr_ref, xi_ref, ar_ref, ai_ref, cr_ref, ci_ref, *, batch, nchunk):
    ar, ai = ar_ref[...], ai_ref[...]
    width = ar.shape[1]

    def step(n, carry):
        new = []
        for b in range(batch):
            cr, ci = carry[2 * b], carry[2 * b + 1]
            idx = b * nchunk + n
            cr_ref[pl.ds(idx, 1), :] = cr
            ci_ref[pl.ds(idx, 1), :] = ci
            lr, li = xr_ref[pl.ds(idx, 1), :], xi_ref[pl.ds(idx, 1), :]
            new += [ar * cr - ai * ci + lr, ar * ci + ai * cr + li]
        return tuple(new)

    zero = jnp.zeros((1, width), F32)
    lax.fori_loop(0, nchunk, step, (zero,) * (2 * batch))


def _s5_y_kernel(u_ref, cr_ref, ci_ref, k_ref, v_ref, d_ref, y_ref, m_scr):
    k = k_ref[...]
    col = lax.broadcasted_iota(jnp.int32, k.shape, 1)
    m_scr[0:S5_PAIR_LANES, :] = k
    for s in range(1, S5_CHUNK):
        shifted = pltpu.roll(k, s * S5_PAIR_LANES, 1)
        m_scr[s * S5_PAIR_LANES:(s + 1) * S5_PAIR_LANES, :] = jnp.where(col >= s * S5_PAIR_LANES, shifted, 0.0)
    v = v_ref[...]
    p = SSM_STATE
    tall = jnp.concatenate([v[:p], v[:p], v[p:], v[p:]], axis=0)
    v_pair = jnp.where(_member(tall.shape, 0, p) == _member(tall.shape, 1, SSM_GROUP), tall, 0.0)
    u = u_ref[...]
    carry = jnp.concatenate([cr_ref[...], ci_ref[...]], axis=1)
    y = _dot3(u, *_split2(m_scr[...])) + _dot3(carry, *_split2(v_pair)) + d_ref[...] * u
    y_ref[...] = jax.nn.gelu(y)


def _s5_matrices(log_dt, lambda_re, lambda_im, b_re, b_im, c_re, c_im, d_skip):
    hp = lax.Precision.HIGHEST
    gn, pn, cn, ln = SSM_GROUPS, SSM_STATE, SSM_GROUP, S5_CHUNK
    dt = jnp.exp(log_dt.astype(F32))[:, None]
    lre = jnp.minimum(lambda_re.astype(F32), -1e-4)
    lim = lambda_im.astype(F32)
    mag = jnp.exp(lre * dt)
    ab_re, ab_im = mag * jnp.cos(lim * dt), mag * jnp.sin(lim * dt)
    den = lre * lre + lim * lim
    nr = ab_re - 1.0
    coef_re = (nr * lre + ab_im * lim) / den
    coef_im = (ab_im * lre - nr * lim) / den
    br32, bi32 = b_re.astype(F32), b_im.astype(F32)
    bb_re = coef_re[..., None] * br32 - coef_im[..., None] * bi32
    bb_im = coef_re[..., None] * bi32 + coef_im[..., None] * br32
    cr32, ci32 = c_re.astype(F32), c_im.astype(F32)
    k_steps = jnp.arange(ln + 1, dtype=F32)[:, None, None]
    pw_mag = jnp.exp(k_steps * (lre * dt))
    pw_re, pw_im = pw_mag * jnp.cos(k_steps * (lim * dt)), pw_mag * jnp.sin(k_steps * (lim * dt))
    ab_b_re = pw_re[:ln, :, :, None] * bb_re - pw_im[:ln, :, :, None] * bb_im
    ab_b_im = pw_re[:ln, :, :, None] * bb_im + pw_im[:ln, :, :, None] * bb_re
    kern = (jnp.einsum('gop,kgpi->kgoi', cr32, ab_b_re, precision=hp)
            - jnp.einsum('gop,kgpi->kgoi', ci32, ab_b_im, precision=hp))
    gp = gn // 2
    same = jnp.eye(2, dtype=F32)
    k_pair = kern.reshape(ln, gp, 2, cn, cn).transpose(1, 2, 4, 0, 3)
    k_pair = k_pair[:, :, :, :, None, :] * same[None, :, None, None, :, None]
    k_pair = k_pair.reshape(gp, S5_PAIR_LANES, S5_PAIR)
    w = jnp.stack([ab_b_re[::-1], ab_b_im[::-1]]).reshape(2, ln, gp, 2, pn, cn)
    w_pair = w.transpose(2, 1, 3, 5, 0, 4).reshape(gp, S5_PAIR, 2 * pn)
    ar1, ai1 = pw_re[1:], pw_im[1:]
    v_re = cr32[None] * ar1[:, :, None, :] - ci32[None] * ai1[:, :, None, :]
    v_im = -(cr32[None] * ai1[:, :, None, :] + ci32[None] * ar1[:, :, None, :])
    v = jnp.stack([v_re, v_im]).reshape(2, ln, gp, 2, cn, pn)
    v_pair = v.transpose(2, 0, 5, 1, 3, 4).reshape(gp, 2 * pn, S5_PAIR)
    d_pair = jnp.tile(d_skip.astype(F32).reshape(gp, 1, 2 * cn), (1, 1, ln))
    a_end_re = pw_re[ln].reshape(1, gn * pn)
    a_end_im = pw_im[ln].reshape(1, gn * pn)
    return k_pair, w_pair, v_pair, d_pair, a_end_re, a_end_im


def _s5_layer(h, batch, seq, norm, w_in, log_dt, lambda_re, lambda_im, b_re, b_im, c_re, c_im, d_skip,
              w_glu, w_out, final_norm, final):
    gp, ln, cn = SSM_GROUPS // 2, S5_CHUNK, SSM_GROUP
    nchunk = seq // ln
    rows = batch * nchunk
    state_w = 2 * SSM_STATE
    k_pair, w_pair, v_pair, d_pair, a_re, a_im = _s5_matrices(
        log_dt, lambda_re, lambda_im, b_re, b_im, c_re, c_im, d_skip)
    up, z = _s5_in(h, norm, w_in)
    pair3 = lambda a, b: pl.BlockSpec((None, a, b), lambda i: (i, 0, 0))
    slab = pl.BlockSpec((rows, state_w), lambda i: (0, i))
    xr, xi = pl.pallas_call(
        _s5_state_kernel,
        grid=(gp,),
        in_specs=[pair3(rows, S5_PAIR), pair3(S5_PAIR, state_w)],
        out_specs=[slab, slab],
        out_shape=[jax.ShapeDtypeStruct((rows, gp * state_w), F32)] * 2,
        compiler_params=_params("parallel"),
        name="s5_chunk_state",
    )(up, w_pair)
    scan_w = 4 * state_w
    wide = pl.BlockSpec((rows, scan_w), lambda i: (0, i))
    coef = pl.BlockSpec((1, scan_w), lambda i: (0, i))
    cr, ci = pl.pallas_call(
        functools.partial(_s5_scan_kernel, batch=batch, nchunk=nchunk),
        grid=(gp * state_w // scan_w,),
        in_specs=[wide, wide, coef, coef],
        out_specs=[wide, wide],
        out_shape=[jax.ShapeDtypeStruct((rows, gp * state_w), F32)] * 2,
        compiler_params=_params("parallel"),
        name="s5_carry_scan",
    )(xr, xi, a_re, a_im)
    yp = pl.pallas_call(
        _s5_y_kernel,
        grid=(gp,),
        in_specs=[pair3(rows, S5_PAIR), slab, slab, pair3(S5_PAIR_LANES, S5_PAIR), pair3(state_w, S5_PAIR),
                  pair3(1, S5_PAIR)],
        out_specs=pair3(rows, S5_PAIR),
        out_shape=jax.ShapeDtypeStruct((gp, rows, S5_PAIR), F32),
        scratch_shapes=[pltpu.VMEM((S5_PAIR, S5_PAIR), F32)],
        compiler_params=_params("parallel"),
        name="s5_chunk_output",
    )(up, cr, ci, k_pair, v_pair, d_pair)
    return _s5_out(yp, z, w_glu, w_out, h, final_norm, final)


def kernel(x, l0_norm, l0_w_in, l0_cmp_k_pe, l0_cmp_k_w1, l0_cmp_k_w2, l0_cmp_v_pe, l0_cmp_v_w1, l0_cmp_v_w2, l0_w_out, l1_norm, l1_w_in, l1_log_dt, l1_lambda_re, l1_lambda_im, l1_b_re, l1_b_im, l1_c_re, l1_c_im, l1_d, l1_w_glu, l1_w_out, l2_norm, l2_w_in, l2_cmp_k_pe, l2_cmp_k_w1, l2_cmp_k_w2, l2_cmp_v_pe, l2_cmp_v_w1, l2_cmp_v_w2, l2_w_out, l3_norm, l3_w_in, l3_log_dt, l3_lambda_re, l3_lambda_im, l3_b_re, l3_b_im, l3_c_re, l3_c_im, l3_d, l3_w_glu, l3_w_out, final_norm):
    batch, seq, _ = x.shape
    h = x.reshape(batch * seq, D_MODEL)
    h = _nsa_layer(h, batch, seq, l0_norm, l0_w_in, l0_cmp_k_pe, l0_cmp_k_w1, l0_cmp_k_w2,
                   l0_cmp_v_pe, l0_cmp_v_w1, l0_cmp_v_w2, l0_w_out)
    h = _s5_layer(h, batch, seq, l1_norm, l1_w_in, l1_log_dt, l1_lambda_re, l1_lambda_im, l1_b_re, l1_b_im,
                  l1_c_re, l1_c_im, l1_d, l1_w_glu, l1_w_out, final_norm, False)
    h = _nsa_layer(h, batch, seq, l2_norm, l2_w_in, l2_cmp_k_pe, l2_cmp_k_w1, l2_cmp_k_w2,
                   l2_cmp_v_pe, l2_cmp_v_w1, l2_cmp_v_w2, l2_w_out)
    h = _s5_layer(h, batch, seq, l3_norm, l3_w_in, l3_log_dt, l3_lambda_re, l3_lambda_im, l3_b_re, l3_b_im,
                  l3_c_re, l3_c_im, l3_d, l3_w_glu, l3_w_out, final_norm, True)
    return h.reshape(batch, seq, D_MODEL)
```

```python
import functools

import jax
import jax.numpy as jnp
from jax import lax
from jax.experimental import pallas as pl
from jax.experimental.pallas import tpu as pltpu

F32 = jnp.float32
BF16 = jnp.bfloat16

D_MODEL = 1024
EPS = 1e-6
NEG_INF = -1e30
FORCE_SCORE = 1e9

N_HEADS = 16
HEAD_DIM = 64
N_KV_GROUPS = 4
HEADS_PER_GROUP = N_HEADS // N_KV_GROUPS
ATTN_WIDTH = N_HEADS * HEAD_DIM
KV_WIDTH = N_KV_GROUPS * HEAD_DIM
CMP_BLOCK = 32
CMP_STRIDE = 16
CMP_HIDDEN = 256
SEL_BLOCK = 64
SEL_TOPK = 16
WINDOW = 512
N_GATES = 3 * N_HEADS
GATES_PAD = 128

SSM_WIDTH = D_MODEL
SSM_GROUP = 16
SSM_GROUPS = SSM_WIDTH // SSM_GROUP
SSM_STATE = 64
S5_CHUNK = 16
S5_PAIRS = SSM_GROUPS // 2
S5_PAIR_LANES = 2 * SSM_GROUP
S5_PAIR = S5_CHUNK * S5_PAIR_LANES

ROW_TILE = 512
ATTN_TQ = 256
ATTN_TK = 512
LANES = 128
MASK_BIAS = -2.0 ** 100
EXP_HEADROOM = 60.0
VMEM_LIMIT = 56 * 1024 * 1024


def _params(*sem):
    return pltpu.CompilerParams(dimension_semantics=sem, vmem_limit_bytes=VMEM_LIMIT)


def _rms(x, g):
    return x * lax.rsqrt(jnp.mean(x * x, axis=-1, keepdims=True) + EPS) * g


def _silu(x):
    return x * jax.nn.sigmoid(x)


def _nsa_in_kernel(h_ref, g_ref, w_ref, qcols_ref, selc_ref, winc_ref, ones_ref,
                   qa_ref, ksa_ref, vsa_ref, kwa_ref, vwa_ref, kcin_ref, vcin_ref, gt_ref, z_ref, c_scr):
    gg, rr, dh = N_KV_GROUPS, HEADS_PER_GROUP, HEAD_DIM
    xn = _rms(h_ref[...], g_ref[...])
    y = jnp.dot(xn.astype(BF16), w_ref[...], preferred_element_type=F32)
    for g in range(gg):
        for r in range(rr):
            k = g * rr + r
            qa_ref[g, r] = jnp.broadcast_to(qcols_ref[g, r:r + 1, :], (ROW_TILE, LANES))
            qa_ref[g, r, :, :dh] = (y[:, k * dh:(k + 1) * dh] * (dh ** -0.5)).astype(BF16)
    o = ATTN_WIDTH + 2 * KV_WIDTH
    consts = (selc_ref[...], jnp.broadcast_to(ones_ref[...], (ROW_TILE, LANES)), winc_ref[...],
              jnp.broadcast_to(ones_ref[...], (ROW_TILE, LANES)))
    for kind, ref in enumerate((ksa_ref, vsa_ref, kwa_ref, vwa_ref)):
        for g in range(gg):
            c = o + (kind * gg + g) * dh
            ref[g] = consts[kind]
            ref[g, :, :dh] = y[:, c:c + dh].astype(BF16)
    lane = lax.broadcasted_iota(jnp.int32, (ROW_TILE // CMP_STRIDE, LANES), 1)
    for n in range(2 * KV_WIDTH // LANES):
        c_scr[n] = y[:, ATTN_WIDTH + n * LANES:ATTN_WIDTH + (n + 1) * LANES]
        kind, g0 = divmod(2 * n, gg)
        for a in range(CMP_STRIDE // 2):
            even = c_scr[n, pl.ds(2 * a, ROW_TILE // CMP_STRIDE, stride=CMP_STRIDE), :]
            odd = c_scr[n, pl.ds(2 * a + 1, ROW_TILE // CMP_STRIDE, stride=CMP_STRIDE), :]
            dst = (kcin_ref, vcin_ref)[kind]
            dst[g0, :, a * LANES:(a + 1) * LANES] = jnp.where(lane < dh, even, pltpu.roll(odd, dh, 1))
            dst[g0 + 1, :, a * LANES:(a + 1) * LANES] = jnp.where(lane < dh, pltpu.roll(even, dh, 1), odd)
    o += 4 * KV_WIDTH
    z_ref[...] = y[:, o:o + ATTN_WIDTH]
    o += ATTN_WIDTH
    per_group = 3 * rr
    for g in range(gg):
        gt_ref[g] = y[:, o + g * per_group:o + (g + 1) * per_group]


def _nsa_in(h, norm, w_in, batch, seq):
    gg, rr, dh = N_KV_GROUPS, HEADS_PER_GROUP, HEAD_DIM
    nt = seq // ROW_TILE
    nc_tile = ROW_TILE // CMP_STRIDE
    a, b = ATTN_WIDTH + 6 * KV_WIDTH, ATTN_WIDTH + 6 * KV_WIDTH + N_GATES
    w_gates = w_in[:, a:b].reshape(D_MODEL, 3, gg, rr).transpose(0, 2, 1, 3).reshape(D_MODEL, N_GATES)
    w = jnp.concatenate([w_in[:, :a], w_in[:, b:], w_gates,
                         jnp.zeros((D_MODEL, GATES_PAD - N_GATES), w_in.dtype)], axis=1).astype(BF16)
    n = w.shape[1]
    qcols, sel_cols, win_cols, cmp_cols, ones_cols = _alibi_columns(seq)
    lead = lambda cols: jnp.pad(cols, [(0, 0)] * (cols.ndim - 1) + [(dh, 0)])
    const = lambda shape: pl.BlockSpec(shape, lambda i: (0,) * len(shape))
    by_pos = lambda width: pl.BlockSpec((ROW_TILE, width), lambda i: (i % nt, 0))
    per_b = lambda *tail: pl.BlockSpec((None, gg) + tail, lambda i: (i // nt, 0) + (0,) * (len(tail) - 2) + (i % nt, 0))
    outs = pl.pallas_call(
        _nsa_in_kernel,
        grid=(batch * nt,),
        in_specs=[pl.BlockSpec((ROW_TILE, D_MODEL), lambda i: (i, 0)), const((1, D_MODEL)), const((D_MODEL, n)),
                  const((gg, rr, LANES)), by_pos(2 * LANES), by_pos(LANES), const((1, LANES))],
        out_specs=[per_b(rr, ROW_TILE, LANES), per_b(ROW_TILE, 2 * LANES), per_b(ROW_TILE, LANES),
                   per_b(ROW_TILE, LANES), per_b(ROW_TILE, LANES),
                   per_b(nc_tile, CMP_STRIDE * dh), per_b(nc_tile, CMP_STRIDE * dh),
                   per_b(ROW_TILE, 3 * rr), pl.BlockSpec((ROW_TILE, ATTN_WIDTH), lambda i: (i, 0))],
        out_shape=[jax.ShapeDtypeStruct((batch, gg, rr, seq, LANES), BF16),
                   jax.ShapeDtypeStruct((batch, gg, seq, 2 * LANES), BF16),
                   jax.ShapeDtypeStruct((batch, gg, seq, LANES), BF16),
                   jax.ShapeDtypeStruct((batch, gg, seq, LANES), BF16),
                   jax.ShapeDtypeStruct((batch, gg, seq, LANES), BF16),
                   jax.ShapeDtypeStruct((batch, gg, seq // CMP_STRIDE, CMP_STRIDE * dh), F32),
                   jax.ShapeDtypeStruct((batch, gg, seq // CMP_STRIDE, CMP_STRIDE * dh), F32),
                   jax.ShapeDtypeStruct((batch, gg, seq, 3 * rr), F32),
                   jax.ShapeDtypeStruct((batch * seq, ATTN_WIDTH), F32)],
        scratch_shapes=[pltpu.VMEM((2 * KV_WIDTH // LANES, ROW_TILE, LANES), F32)],
        compiler_params=_params("parallel"),
        name="nsa_in_proj",
    )(h, norm.reshape(1, D_MODEL), w, lead(qcols), lead(sel_cols), lead(win_cols), lead(ones_cols)[None])
    return outs, cmp_cols, ones_cols


def _quarter_exchange(parts):
    quarter = lax.broadcasted_iota(jnp.int32, parts[0].shape, 1) // S5_PAIR_LANES
    outs = []
    for b in range(4):
        acc = None
        for a in range(4):
            moved = parts[a] if a == b else pltpu.roll(parts[a], ((a - b) % 4) * S5_PAIR_LANES, 1)
            acc = moved if acc is None else jnp.where(quarter == a, moved, acc)
        outs.append(acc)
    return outs


def _s5_in_kernel(h_ref, g_ref, w_ref, u_ref, z_ref, u_scr):
    xn = _rms(h_ref[...], g_ref[...])
    y = jnp.dot(xn.astype(BF16), w_ref[...], preferred_element_type=F32)
    z_ref[...] = y[:, SSM_WIDTH:]
    nch = ROW_TILE // S5_CHUNK
    for m in range(SSM_WIDTH // 128):
        u_scr[m] = y[:, m * 128:(m + 1) * 128]
        for k in range(S5_CHUNK // 4):
            steps = [u_scr[m, pl.ds(4 * k + a, nch, stride=S5_CHUNK), :] for a in range(4)]
            for b, tile in enumerate(_quarter_exchange(steps)):
                u_ref[4 * m + b, :, k * 128:(k + 1) * 128] = tile


def _s5_in(h, norm, w_in):
    t = h.shape[0]
    nch = ROW_TILE // S5_CHUNK
    row = lambda width: pl.BlockSpec((ROW_TILE, width), lambda i: (i, 0))
    return pl.pallas_call(
        _s5_in_kernel,
        grid=(t // ROW_TILE,),
        in_specs=[row(D_MODEL), pl.BlockSpec((1, D_MODEL), lambda i: (0, 0)),
                  pl.BlockSpec((D_MODEL, 2 * SSM_WIDTH), lambda i: (0, 0))],
        out_specs=[pl.BlockSpec((S5_PAIRS, nch, S5_PAIR), lambda i: (0, i, 0)), row(SSM_WIDTH)],
        out_shape=[jax.ShapeDtypeStruct((S5_PAIRS, t // S5_CHUNK, S5_PAIR), F32),
                   jax.ShapeDtypeStruct((t, SSM_WIDTH), F32)],
        scratch_shapes=[pltpu.VMEM((SSM_WIDTH // 128, ROW_TILE, 128), F32)],
        compiler_params=_params("parallel"),
        name="s5_in_proj",
    )(h, norm.reshape(1, D_MODEL), w_in.astype(BF16))


def _nsa_out_kernel(o_ref, z_ref, w_ref, res_ref, out_ref):
    a = o_ref[...] * _silu(z_ref[...])
    out_ref[...] = res_ref[...] + jnp.dot(a.astype(BF16), w_ref[...], preferred_element_type=F32)


def _nsa_out(o, z, w_out, res):
    t = o.shape[0]
    row = pl.BlockSpec((ROW_TILE, D_MODEL), lambda i: (i, 0))
    return pl.pallas_call(
        _nsa_out_kernel,
        grid=(t // ROW_TILE,),
        in_specs=[row, row, pl.BlockSpec((ATTN_WIDTH, D_MODEL), lambda i: (0, 0)), row],
        out_specs=row,
        out_shape=jax.ShapeDtypeStruct((t, D_MODEL), F32),
        compiler_params=_params("parallel"),
        name="nsa_out_proj",
    )(o, z, w_out.astype(BF16), res)


def _s5_out_kernel(y_ref, z_ref, wg_ref, wo_ref, res_ref, fn_ref, out_ref, y_scr, *, final):
    nch = ROW_TILE // S5_CHUNK
    for m in range(SSM_WIDTH // 128):
        for k in range(S5_CHUNK // 4):
            pairs = [y_ref[4 * m + b, :, k * 128:(k + 1) * 128] for b in range(4)]
            for a, tile in enumerate(_quarter_exchange(pairs)):
                y_scr[m, pl.ds(4 * k + a, nch, stride=S5_CHUNK), :] = tile
    y = jnp.concatenate([y_scr[m] for m in range(SSM_WIDTH // 128)], axis=1)
    gl = jnp.dot(y.astype(BF16), wg_ref[...], preferred_element_type=F32)
    v = gl[:, :SSM_WIDTH] * jax.nn.sigmoid(gl[:, SSM_WIDTH:])
    v = v * _silu(z_ref[...])
    h = res_ref[...] + jnp.dot(v.astype(BF16), wo_ref[...], preferred_element_type=F32)
    out_ref[...] = _rms(h, fn_ref[...]) if final else h


def _s5_out(y, z, w_glu, w_out, res, final_norm, final):
    t = z.shape[0]
    row = pl.BlockSpec((ROW_TILE, D_MODEL), lambda i: (i, 0))
    return pl.pallas_call(
        functools.partial(_s5_out_kernel, final=final),
        grid=(t // ROW_TILE,),
        in_specs=[pl.BlockSpec((S5_PAIRS, ROW_TILE // S5_CHUNK, S5_PAIR), lambda i: (0, i, 0)), row,
                  pl.BlockSpec((SSM_WIDTH, 2 * SSM_WIDTH), lambda i: (0, 0)),
                  pl.BlockSpec((SSM_WIDTH, D_MODEL), lambda i: (0, 0)), row,
                  pl.BlockSpec((1, D_MODEL), lambda i: (0, 0))],
        out_specs=row,
        out_shape=jax.ShapeDtypeStruct((t, D_MODEL), F32),
        scratch_shapes=[pltpu.VMEM((SSM_WIDTH // 128, ROW_TILE, 128), F32)],
        compiler_params=_params("parallel"),
        name="s5_glu_out_proj",
    )(y, z, w_glu.astype(BF16), w_out.astype(BF16), res, final_norm.reshape(1, D_MODEL))


def _cmp_one(a_ref, pe_ref, w1_ref, w2_ref, out_ref):
    nc = a_ref.shape[0]
    a = a_ref[...]
    pe = pe_ref[...]
    h_top = jnp.dot((a + pe[0:1]).astype(BF16), w1_ref[0], preferred_element_type=F32)
    h_bot = jnp.dot((a + pe[1:2]).astype(BF16), w1_ref[1], preferred_element_type=F32)
    hid = _silu(h_top + pltpu.roll(h_bot, nc - 1, 0))
    out = jnp.dot(hid.astype(BF16), w2_ref[...], preferred_element_type=F32)
    keep = lax.broadcasted_iota(jnp.int32, out.shape, 0) < nc - 1
    out_ref[...] = jnp.where(keep, out, 0.0).astype(BF16)


def _cmp_kernel(ka_ref, va_ref, kpe_ref, kw1_ref, kw2_ref, vpe_ref, vw1_ref, vw2_ref, kc_ref, vc_ref):
    _cmp_one(ka_ref, kpe_ref, kw1_ref, kw2_ref, kc_ref)
    _cmp_one(va_ref, vpe_ref, vw1_ref, vw2_ref, vc_ref)


def _compress(ka, va, k_pe, k_w1, k_w2, v_pe, v_w1, v_w2):
    bg, nc, half = ka.shape
    blk = pl.BlockSpec((None, nc, half), lambda i: (i, 0, 0))
    full = lambda shape: pl.BlockSpec(shape, lambda i: (0,) * len(shape))
    prep = lambda pe, w1, w2: (pe.reshape(2, half), w1.reshape(2, half, CMP_HIDDEN).astype(BF16), w2.astype(BF16))
    out = pl.BlockSpec((None, nc, HEAD_DIM), lambda i: (i, 0, 0))
    wspecs = [full((2, half)), full((2, half, CMP_HIDDEN)), full((CMP_HIDDEN, HEAD_DIM))]
    return pl.pallas_call(
        _cmp_kernel,
        grid=(bg,),
        in_specs=[blk, blk] + wspecs + wspecs,
        out_specs=[out, out],
        out_shape=[jax.ShapeDtypeStruct((bg, nc, HEAD_DIM), BF16)] * 2,
        compiler_params=_params("parallel"),
        name="nsa_compress",
    )(ka, va, *prep(k_pe, k_w1, k_w2), *prep(v_pe, v_w1, v_w2))


def _dot_nt(a, b):
    return lax.dot_general(a, b, (((1,), (1,)), ((), ())), preferred_element_type=F32)


def _split3(x):
    hi = x.astype(BF16)
    r1 = x - hi.astype(F32)
    mid = r1.astype(BF16)
    lo = (r1 - mid.astype(F32)).astype(BF16)
    return hi, mid, lo


def _softmax_values(parts, m_row, va):
    p = [jnp.exp(sc - m_row) for sc in parts]
    acc = jnp.dot(jnp.concatenate(p, axis=1).astype(BF16), va, preferred_element_type=F32)
    return acc, p


def _row_max(parts):
    m = parts[0]
    for sc in parts[1:]:
        m = jnp.maximum(m, sc)
    return jnp.broadcast_to(jnp.max(m, axis=-1, keepdims=True), m.shape)


def _attn_kernel(qa_ref, ksa_ref, vsa_ref, kwa_ref, vwa_ref, kca_ref, vca_ref, gt_ref, ovt_ref,
                 o_ref, m_scr, top_scr, acc_scr, qaug_scr, used_smem, *, seq, tq, tk):
    nb = seq // SEL_BLOCK
    nc = seq // CMP_STRIDE
    rows = HEADS_PER_GROUP * tq
    q0 = pl.program_id(1) * tq

    qa = qa_ref[...].reshape(rows, LANES)
    row_l = lax.broadcasted_iota(jnp.int32, (rows, LANES), 0)
    lane = lax.broadcasted_iota(jnp.int32, (rows, LANES), 1)
    t_l = q0 + (row_l - (row_l // tq) * tq)

    sc_all = _dot_nt(qa, kca_ref[...])
    last_ok = (t_l - (CMP_BLOCK - 1)) >> 4
    parts = [jnp.where(lane + c * LANES <= last_ok, sc_all[:, c * LANES:(c + 1) * LANES], NEG_INF)
             for c in range(nc // LANES)]
    m_c = jnp.where(t_l >= CMP_BLOCK - 1, _row_max(parts), -NEG_INF)
    acc_c, p_c = _softmax_values(parts, m_c, vca_ref[...])
    inv_c = jnp.broadcast_to(1.0 / jnp.maximum(acc_c[:, HEAD_DIM:HEAD_DIM + 1], 1e-30), (rows, LANES))
    o_c = acc_c[:, :HEAD_DIM] * inv_c[:, :HEAD_DIM]
    p = jnp.concatenate([pc * inv_c for pc in p_c], axis=1)

    wlen = WINDOW + tq
    w0 = pl.multiple_of(jnp.maximum(q0 - WINDOW, 0), tq)
    sw_all = _dot_nt(qa, kwa_ref[pl.ds(w0, wlen), :])
    parts = []
    for c in range(wlen // LANES):
        key = w0 + c * LANES + lane
        ok = key <= t_l
        if c * LANES < tq:
            ok = ok & (key > t_l - WINDOW)
        parts.append(jnp.where(ok, sw_all[:, c * LANES:(c + 1) * LANES], NEG_INF))
    acc_w, _ = _softmax_values(parts, _row_max(parts), vwa_ref[pl.ds(w0, wlen), :])
    o_w = acc_w[:, :HEAD_DIM] / jnp.maximum(acc_w[:, HEAD_DIM:HEAD_DIM + 1], 1e-30)

    psum = p[0:tq] + p[tq:2 * tq] + p[2 * tq:3 * tq] + p[3 * tq:4 * tq]
    ovt = ovt_ref[...]
    imp = sum(_dot_nt(ovt, part) for part in _split3(psum))
    j = lax.broadcasted_iota(jnp.int32, (nb, tq), 0)
    jt = (q0 + lax.broadcasted_iota(jnp.int32, (nb, tq), 1)) // SEL_BLOCK
    forced = (j == 0) | (j == jt) | (j == jt - 1)
    sel_t = jnp.where(forced, 1.0, 0.0)
    vals = jnp.where(forced, -3e38, jnp.where(j > jt, -FORCE_SCORE, imp))
    jf = j.astype(F32)
    for _ in range(min(SEL_TOPK, nb) - 3):
        best = jnp.max(vals, axis=0, keepdims=True)
        first = jnp.min(jnp.where(vals == best, jf, float(nb)), axis=0, keepdims=True)
        hit = jf == first
        sel_t = jnp.where(hit, 1.0, sel_t)
        vals = jnp.where(hit, -3e38, vals)

    not_chosen = ((1.0 - sel_t) * MASK_BIAS).T.astype(BF16)
    qaug_scr[:, :LANES] = qa
    for r in range(HEADS_PER_GROUP):
        qaug_scr[r * tq:(r + 1) * tq, LANES:LANES + nb] = not_chosen
    if nb < LANES:
        qaug_scr[:, LANES + nb:] = jnp.zeros((rows, LANES - nb), BF16)
    ntile = tk // LANES

    def tile_scores(kt, causal):
        kb = pl.multiple_of(kt * tk, tk)
        sc = _dot_nt(qaug_scr[...], ksa_ref[pl.ds(kb, tk), :])
        if causal:
            key = kb + lax.broadcasted_iota(jnp.int32, (rows, tk), 1)
            sc = jnp.where(jnp.concatenate([t_l] * ntile, axis=1) >= key, sc, NEG_INF)
        return kb, sc

    def lane_max(sc):
        part = sc[:, :LANES]
        for c in range(1, ntile):
            part = jnp.maximum(part, sc[:, c * LANES:(c + 1) * LANES])
        return part

    def value_tile(kt, causal, track):
        kb, sc = tile_scores(kt, causal)
        if track:
            top_scr[...] = jnp.maximum(top_scr[...], lane_max(sc))
        pr = jnp.exp(sc - jnp.concatenate([m_scr[...]] * ntile, axis=1))
        acc_scr[...] += jnp.dot(pr.astype(BF16), vsa_ref[pl.ds(kb, tk), :], preferred_element_type=F32)

    blocks_per_tile = tk // SEL_BLOCK
    for i in range(seq // tk):
        chosen_here = jnp.max(sel_t[i * blocks_per_tile:(i + 1) * blocks_per_tile, :])
        used_smem[i] = (chosen_here > 0.5).astype(jnp.int32)
    n_before = q0 // tk

    def tiles_before(track):
        def step(kt, carry):
            @pl.when(used_smem[kt] > 0)
            def _():
                value_tile(kt, False, track)
            return carry
        lax.fori_loop(0, n_before, step, 0)

    kb_diag, sc_diag = tile_scores(n_before, True)
    m_diag = jnp.broadcast_to(jnp.max(lane_max(sc_diag), axis=-1, keepdims=True), (rows, LANES))
    m_scr[...] = m_diag
    pr_diag = jnp.exp(sc_diag - jnp.concatenate([m_diag] * ntile, axis=1))
    acc_scr[...] = jnp.dot(pr_diag.astype(BF16), vsa_ref[pl.ds(kb_diag, tk), :], preferred_element_type=F32)
    top_scr[...] = jnp.full((rows, LANES), NEG_INF, F32)
    tiles_before(True)

    @pl.when(jnp.max(top_scr[...] - m_scr[...]) > EXP_HEADROOM)
    def _():
        exact = jnp.maximum(top_scr[...], m_scr[...])
        m_scr[...] = jnp.broadcast_to(jnp.max(exact, axis=-1, keepdims=True), (rows, LANES))
        acc_scr[...] = jnp.zeros((rows, LANES), F32)
        tiles_before(False)
        value_tile(n_before, True, False)

    acc = acc_scr[...]
    o_s = acc[:, :HEAD_DIM] / jnp.maximum(acc[:, HEAD_DIM:HEAD_DIM + 1], 1e-30)

    gate = jax.nn.sigmoid(gt_ref[...])
    for r in range(HEADS_PER_GROUP):
        sl = slice(r * tq, (r + 1) * tq)
        gc = lambda x: gate[:, x * HEADS_PER_GROUP + r:x * HEADS_PER_GROUP + r + 1]
        o_ref[:, r * HEAD_DIM:(r + 1) * HEAD_DIM] = gc(0) * o_c[sl] + gc(1) * o_s[sl] + gc(2) * o_w[sl]


def _attention(qa, ksa, vsa, kwa, vwa, kca, vca, gt, batch, seq):
    tq, tk = ATTN_TQ, ATTN_TK
    bg = batch * N_KV_GROUPS
    nq = seq // tq
    nb, nc = seq // SEL_BLOCK, seq // CMP_STRIDE
    assert SEL_TOPK <= nb <= LANES and seq % tk == 0 and tk % tq == 0 and tq % LANES == 0 and nc % LANES == 0
    rows = HEADS_PER_GROUP * tq
    gq = HEADS_PER_GROUP * HEAD_DIM
    cs = jnp.arange(nc)[None, :] * CMP_STRIDE
    ss = jnp.arange(nb)[:, None] * SEL_BLOCK
    ovt = ((cs <= ss + SEL_BLOCK - 1) & (cs + CMP_BLOCK - 1 >= ss)).astype(BF16)
    per_bg = lambda n, w: pl.BlockSpec((None, n, w), lambda i, j: (i, 0, 0))
    return pl.pallas_call(
        functools.partial(_attn_kernel, seq=seq, tq=tq, tk=tk),
        grid=(bg, nq),
        in_specs=[pl.BlockSpec((None, HEADS_PER_GROUP, tq, LANES), lambda i, j: (i, 0, j, 0)),
                  per_bg(seq, 2 * LANES), per_bg(seq, LANES), per_bg(seq, LANES), per_bg(seq, LANES),
                  per_bg(nc, LANES), per_bg(nc, LANES),
                  pl.BlockSpec((None, tq, 3 * HEADS_PER_GROUP), lambda i, j: (i, j, 0)),
                  pl.BlockSpec((nb, nc), lambda i, j: (0, 0))],
        out_specs=pl.BlockSpec((tq, gq), lambda i, j: ((i // N_KV_GROUPS) * nq + j, i % N_KV_GROUPS)),
        out_shape=jax.ShapeDtypeStruct((batch * seq, ATTN_WIDTH), F32),
        scratch_shapes=[pltpu.VMEM((rows, LANES), F32), pltpu.VMEM((rows, LANES), F32), pltpu.VMEM((rows, LANES), F32),
                        pltpu.VMEM((rows, 2 * LANES), BF16), pltpu.SMEM((seq // tk,), jnp.int32)],
        compiler_params=_params("parallel", "arbitrary"),
        name="nsa_attention",
    )(qa, ksa, vsa, kwa, vwa, kca, vca, gt, ovt)


def _alibi_columns(seq):
    dh = HEAD_DIM
    nb, nc = seq // SEL_BLOCK, seq // CMP_STRIDE
    head = jnp.arange(1, N_HEADS + 1, dtype=F32).reshape(N_KV_GROUPS, HEADS_PER_GROUP)
    parts = _split3(jnp.exp2(-8.0 * head / N_HEADS))
    qcols = jnp.stack([float(SEL_BLOCK) * p.astype(F32) for p in parts] + [p.astype(F32) for p in parts], axis=-1)
    qcols = jnp.pad(qcols, ((0, 0), (0, 0), (0, LANES - dh - qcols.shape[-1]))).astype(BF16)

    def key_cols(u, scale):
        cols = jnp.stack([u // SEL_BLOCK] * 3 + [u % SEL_BLOCK] * 3, axis=-1).astype(F32) * scale
        return jnp.pad(cols, ((0, 0), (0, LANES - dh - cols.shape[-1])))

    pos = jnp.arange(seq)
    onehot = (pos[:, None] // SEL_BLOCK == jnp.arange(LANES)[None, :]) & (jnp.arange(LANES) < nb)
    win_cols = key_cols(pos, 1.0).astype(BF16)
    sel_cols = jnp.concatenate([win_cols, onehot.astype(BF16)], axis=1)
    cmp_cols = key_cols(jnp.arange(nc), float(CMP_STRIDE)).astype(BF16)
    ones_cols = (jnp.arange(LANES - dh) == 0).astype(BF16)
    return qcols, sel_cols, win_cols, cmp_cols, ones_cols


def _nsa_layer(h, batch, seq, norm, w_in, k_pe, k_w1, k_w2, v_pe, v_w1, v_w2, w_out):
    gg = N_KV_GROUPS
    bg = batch * gg
    (qa, ksa, vsa, kwa, vwa, kc_in, vc_in, gt, z), cmp_cols, ones_cols = _nsa_in(h, norm, w_in, batch, seq)
    merge = lambda x: x.reshape((bg,) + x.shape[2:])
    with_cols = lambda x, cols: jnp.concatenate([x, jnp.broadcast_to(cols, x.shape[:-1] + cols.shape[-1:])], axis=-1)
    kc, vc = _compress(merge(kc_in), merge(vc_in), k_pe, k_w1, k_w2, v_pe, v_w1, v_w2)
    o = _attention(merge(qa), merge(ksa), merge(vsa), merge(kwa), merge(vwa),
                   with_cols(kc, cmp_cols), with_cols(vc, ones_cols), merge(gt), batch, seq)
    return _nsa_out(o, z, w_out, h)


def _dot_bf16(a, b):
    return jnp.dot(a.astype(BF16), b.astype(BF16), preferred_element_type=F32)


def _member(shape, axis, width):
    return (lax.broadcasted_iota(jnp.int32, shape, axis) // width) % 2


def _s5_state_kernel(u_ref, w_ref, xr_ref, xi_ref):
    w = w_ref[...]
    p = SSM_STATE
    wide = jnp.concatenate([w[:, :p], w[:, :p], w[:, p:], w[:, p:]], axis=1)
    w_pair = jnp.where(_member(wide.shape, 0, SSM_GROUP) == _member(wide.shape, 1, p), wide, 0.0)
    x = _dot_bf16(u_ref[...], w_pair)
    half = x.shape[1] // 2
    xr_ref[...] = x[:, :half]
    xi_ref[...] = x[:, half:]


def _s5_scan_kernel(xr_ref, xi_ref, ar_ref, ai_ref, cr_ref, ci_ref, *, batch, nchunk):
    ar, ai = ar_ref[...], ai_ref[...]
    width = ar.shape[1]

    def step(n, carry):
        new = []
        for b in range(batch):
            cr, ci = carry[2 * b], carry[2 * b + 1]
            idx = b * nchunk + n
            cr_ref[pl.ds(idx, 1), :] = cr
            ci_ref[pl.ds(idx, 1), :] = ci
            lr, li = xr_ref[pl.ds(idx, 1), :], xi_ref[pl.ds(idx, 1), :]
            new += [ar * cr - ai * ci + lr, ar * ci + ai * cr + li]
        return tuple(new)

    zero = jnp.zeros((1, width), F32)
    lax.fori_loop(0, nchunk, step, (zero,) * (2 * batch))


def _s5_y_kernel(u_ref, cr_ref, ci_ref, k_ref, v_ref, d_ref, y_ref, m_scr):
    k = k_ref[...]
    col = lax.broadcasted_iota(jnp.int32, k.shape, 1)
    m_scr[0:S5_PAIR_LANES, :] = k
    for s in range(1, S5_CHUNK):
        shifted = pltpu.roll(k, s * S5_PAIR_LANES, 1)
        m_scr[s * S5_PAIR_LANES:(s + 1) * S5_PAIR_LANES, :] = jnp.where(col >= s * S5_PAIR_LANES, shifted, 0.0)
    v = v_ref[...]
    p = SSM_STATE
    tall = jnp.concatenate([v[:p], v[:p], v[p:], v[p:]], axis=0)
    v_pair = jnp.where(_member(tall.shape, 0, p) == _member(tall.shape, 1, SSM_GROUP), tall, 0.0)
    u = u_ref[...]
    carry = jnp.concatenate([cr_ref[...], ci_ref[...]], axis=1)
    y = _dot_bf16(u, m_scr[...]) + _dot_bf16(carry, v_pair) + d_ref[...] * u
    y_ref[...] = jax.nn.gelu(y)


def _s5_matrices(log_dt, lambda_re, lambda_im, b_re, b_im, c_re, c_im, d_skip):
    hp = lax.Precision.HIGHEST
    gn, pn, cn, ln = SSM_GROUPS, SSM_STATE, SSM_GROUP, S5_CHUNK
    dt = jnp.exp(log_dt.astype(F32))[:, None]
    lre = jnp.minimum(lambda_re.astype(F32), -1e-4)
    lim = lambda_im.astype(F32)
    mag = jnp.exp(lre * dt)
    ab_re, ab_im = mag * jnp.cos(lim * dt), mag * jnp.sin(lim * dt)
    den = lre * lre + lim * lim
    nr = ab_re - 1.0
    coef_re = (nr * lre + ab_im * lim) / den
    coef_im = (ab_im * lre - nr * lim) / den
    br32, bi32 = b_re.astype(F32), b_im.astype(F32)
    bb_re = coef_re[..., None] * br32 - coef_im[..., None] * bi32
    bb_im = coef_re[..., None] * bi32 + coef_im[..., None] * br32
    cr32, ci32 = c_re.astype(F32), c_im.astype(F32)
    k_steps = jnp.arange(ln + 1, dtype=F32)[:, None, None]
    pw_mag = jnp.exp(k_steps * (lre * dt))
    pw_re, pw_im = pw_mag * jnp.cos(k_steps * (lim * dt)), pw_mag * jnp.sin(k_steps * (lim * dt))
    ab_b_re = pw_re[:ln, :, :, None] * bb_re - pw_im[:ln, :, :, None] * bb_im
    ab_b_im = pw_re[:ln, :, :, None] * bb_im + pw_im[:ln, :, :, None] * bb_re
    kern = (jnp.einsum('gop,kgpi->kgoi', cr32, ab_b_re, precision=hp)
            - jnp.einsum('gop,kgpi->kgoi', ci32, ab_b_im, precision=hp))
    gp = gn // 2
    same = jnp.eye(2, dtype=F32)
    k_pair = kern.reshape(ln, gp, 2, cn, cn).transpose(1, 2, 4, 0, 3)
    k_pair = k_pair[:, :, :, :, None, :] * same[None, :, None, None, :, None]
    k_pair = k_pair.reshape(gp, S5_PAIR_LANES, S5_PAIR)
    w = jnp.stack([ab_b_re[::-1], ab_b_im[::-1]]).reshape(2, ln, gp, 2, pn, cn)
    w_pair = w.transpose(2, 1, 3, 5, 0, 4).reshape(gp, S5_PAIR, 2 * pn)
    ar1, ai1 = pw_re[1:], pw_im[1:]
    v_re = cr32[None] * ar1[:, :, None, :] - ci32[None] * ai1[:, :, None, :]
    v_im = -(cr32[None] * ai1[:, :, None, :] + ci32[None] * ar1[:, :, None, :])
    v = jnp.stack([v_re, v_im]).reshape(2, ln, gp, 2, cn, pn)
    v_pair = v.transpose(2, 0, 5, 1, 3, 4).reshape(gp, 2 * pn, S5_PAIR)
    d_pair = jnp.tile(d_skip.astype(F32).reshape(gp, 1, 2 * cn), (1, 1, ln))
    a_end_re = pw_re[ln].reshape(1, gn * pn)
    a_end_im = pw_im[ln].reshape(1, gn * pn)
    return k_pair, w_pair, v_pair, d_pair, a_end_re, a_end_im


def _s5_layer(h, batch, seq, norm, w_in, log_dt, lambda_re, lambda_im, b_re, b_im, c_re, c_im, d_skip,
              w_glu, w_out, final_norm, final):
    gp, ln, cn = SSM_GROUPS // 2, S5_CHUNK, SSM_GROUP
    nchunk = seq // ln
    rows = batch * nchunk
    state_w = 2 * SSM_STATE
    k_pair, w_pair, v_pair, d_pair, a_re, a_im = _s5_matrices(
        log_dt, lambda_re, lambda_im, b_re, b_im, c_re, c_im, d_skip)
    up, z = _s5_in(h, norm, w_in)
    pair3 = lambda a, b: pl.BlockSpec((None, a, b), lambda i: (i, 0, 0))
    slab = pl.BlockSpec((rows, state_w), lambda i: (0, i))
    xr, xi = pl.pallas_call(
        _s5_state_kernel,
        grid=(gp,),
        in_specs=[pair3(rows, S5_PAIR), pair3(S5_PAIR, state_w)],
        out_specs=[slab, slab],
        out_shape=[jax.ShapeDtypeStruct((rows, gp * state_w), F32)] * 2,
        compiler_params=_params("parallel"),
        name="s5_chunk_state",
    )(up, w_pair)
    scan_w = 4 * state_w
    wide = pl.BlockSpec((rows, scan_w), lambda i: (0, i))
    coef = pl.BlockSpec((1, scan_w), lambda i: (0, i))
    cr, ci = pl.pallas_call(
        functools.partial(_s5_scan_kernel, batch=batch, nchunk=nchunk),
        grid=(gp * state_w // scan_w,),
        in_specs=[wide, wide, coef, coef],
        out_specs=[wide, wide],
        out_shape=[jax.ShapeDtypeStruct((rows, gp * state_w), F32)] * 2,
        compiler_params=_params("parallel"),
        name="s5_carry_scan",
    )(xr, xi, a_re, a_im)
    yp = pl.pallas_call(
        _s5_y_kernel,
        grid=(gp,),
        in_specs=[pair3(rows, S5_PAIR), slab, slab, pair3(S5_PAIR_LANES, S5_PAIR), pair3(state_w, S5_PAIR),
                  pair3(1, S5_PAIR)],
        out_specs=pair3(rows, S5_PAIR),
        out_shape=jax.ShapeDtypeStruct((gp, rows, S5_PAIR), F32),
        scratch_shapes=[pltpu.VMEM((S5_PAIR, S5_PAIR), F32)],
        compiler_params=_params("parallel"),
        name="s5_chunk_output",
    )(up, cr, ci, k_pair, v_pair, d_pair)
    return _s5_out(yp, z, w_glu, w_out, h, final_norm, final)


def kernel(x, l0_norm, l0_w_in, l0_cmp_k_pe, l0_cmp_k_w1, l0_cmp_k_w2, l0_cmp_v_pe, l0_cmp_v_w1, l0_cmp_v_w2, l0_w_out, l1_norm, l1_w_in, l1_log_dt, l1_lambda_re, l1_lambda_im, l1_b_re, l1_b_im, l1_c_re, l1_c_im, l1_d, l1_w_glu, l1_w_out, l2_norm, l2_w_in, l2_cmp_k_pe, l2_cmp_k_w1, l2_cmp_k_w2, l2_cmp_v_pe, l2_cmp_v_w1, l2_cmp_v_w2, l2_w_out, l3_norm, l3_w_in, l3_log_dt, l3_lambda_re, l3_lambda_im, l3_b_re, l3_b_im, l3_c_re, l3_c_im, l3_d, l3_w_glu, l3_w_out, final_norm):
    batch, seq, _ = x.shape
    h = x.reshape(batch * seq, D_MODEL)
    h = _nsa_layer(h, batch, seq, l0_norm, l0_w_in, l0_cmp_k_pe, l0_cmp_k_w1, l0_cmp_k_w2,
                   l0_cmp_v_pe, l0_cmp_v_w1, l0_cmp_v_w2, l0_w_out)
    h = _s5_layer(h, batch, seq, l1_norm, l1_w_in, l1_log_dt, l1_lambda_re, l1_lambda_im, l1_b_re, l1_b_im,
                  l1_c_re, l1_c_im, l1_d, l1_w_glu, l1_w_out, final_norm, False)
    h = _nsa_layer(h, batch, seq, l2_norm, l2_w_in, l2_cmp_k_pe, l2_cmp_k_w1, l2_cmp_k_w2,
                   l2_cmp_v_pe, l2_cmp_v_w1, l2_cmp_v_w2, l2_w_out)
    h = _s5_layer(h, batch, seq, l3_norm, l3_w_in, l3_log_dt, l3_lambda_re, l3_lambda_im, l3_b_re, l3_b_im,
                  l3_c_re, l3_c_im, l3_d, l3_w_glu, l3_w_out, final_norm, True)
    return h.reshape(batch, seq, D_MODEL)
```

```python
import functools

import jax
import jax.numpy as jnp
from jax import lax
from jax.experimental import pallas as pl
from jax.experimental.pallas import tpu as pltpu

F32 = jnp.float32
BF16 = jnp.bfloat16

D_MODEL = 1024
EPS = 1e-6
NEG_INF = -1e30
FORCE_SCORE = 1e9

N_HEADS = 16
HEAD_DIM = 64
N_KV_GROUPS = 4
HEADS_PER_GROUP = N_HEADS // N_KV_GROUPS
ATTN_WIDTH = N_HEADS * HEAD_DIM
KV_WIDTH = N_KV_GROUPS * HEAD_DIM
CMP_BLOCK = 32
CMP_STRIDE = 16
CMP_HIDDEN = 256
SEL_BLOCK = 64
SEL_TOPK = 16
WINDOW = 512
N_GATES = 3 * N_HEADS
GATES_PAD = 128

SSM_WIDTH = D_MODEL
SSM_GROUP = 16
SSM_GROUPS = SSM_WIDTH // SSM_GROUP
SSM_STATE = 64
S5_CHUNK = 16
S5_PAIRS = SSM_GROUPS // 2
S5_PAIR_LANES = 2 * SSM_GROUP
S5_PAIR = S5_CHUNK * S5_PAIR_LANES

ROW_TILE = 512
ATTN_TQ = 256
ATTN_TK = 512
LANES = 128
MASK_BIAS = -2.0 ** 100
EXP_HEADROOM = 60.0
VMEM_LIMIT = 56 * 1024 * 1024


def _params(*sem):
    return pltpu.CompilerParams(dimension_semantics=sem, vmem_limit_bytes=VMEM_LIMIT)


def _rms(x, g):
    return x * lax.rsqrt(jnp.mean(x * x, axis=-1, keepdims=True) + EPS) * g


def _silu(x):
    return x * jax.nn.sigmoid(x)


def _nsa_in_kernel(h_ref, g_ref, w_ref, qcols_ref, selc_ref, winc_ref, ones_ref,
                   qa_ref, ksa_ref, vsa_ref, kwa_ref, vwa_ref, kcin_ref, vcin_ref, gt_ref, z_ref, c_scr):
    gg, rr, dh = N_KV_GROUPS, HEADS_PER_GROUP, HEAD_DIM
    xn = _rms(h_ref[...], g_ref[...])
    y = jnp.dot(xn.astype(BF16), w_ref[...], preferred_element_type=F32)
    for g in range(gg):
        for r in range(rr):
            k = g * rr + r
            qa_ref[g, r] = jnp.broadcast_to(qcols_ref[g, r:r + 1, :], (ROW_TILE, LANES))
            qa_ref[g, r, :, :dh] = (y[:, k * dh:(k + 1) * dh] * (dh ** -0.5)).astype(BF16)
    o = ATTN_WIDTH + 2 * KV_WIDTH
    consts = (selc_ref[...], jnp.broadcast_to(ones_ref[...], (ROW_TILE, LANES)), winc_ref[...],
              jnp.broadcast_to(ones_ref[...], (ROW_TILE, LANES)))
    for kind, ref in enumerate((ksa_ref, vsa_ref, kwa_ref, vwa_ref)):
        for g in range(gg):
            c = o + (kind * gg + g) * dh
            ref[g] = consts[kind]
            ref[g, :, :dh] = y[:, c:c + dh].astype(BF16)
    lane = lax.broadcasted_iota(jnp.int32, (ROW_TILE // CMP_STRIDE, LANES), 1)
    for n in range(2 * KV_WIDTH // LANES):
        c_scr[n] = y[:, ATTN_WIDTH + n * LANES:ATTN_WIDTH + (n + 1) * LANES]
        kind, g0 = divmod(2 * n, gg)
        for a in range(CMP_STRIDE // 2):
            even = c_scr[n, pl.ds(2 * a, ROW_TILE // CMP_STRIDE, stride=CMP_STRIDE), :]
            odd = c_scr[n, pl.ds(2 * a + 1, ROW_TILE // CMP_STRIDE, stride=CMP_STRIDE), :]
            dst = (kcin_ref, vcin_ref)[kind]
            dst[g0, :, a * LANES:(a + 1) * LANES] = jnp.where(lane < dh, even, pltpu.roll(odd, dh, 1))
            dst[g0 + 1, :, a * LANES:(a + 1) * LANES] = jnp.where(lane < dh, pltpu.roll(even, dh, 1), odd)
    o += 4 * KV_WIDTH
    z_ref[...] = y[:, o:o + ATTN_WIDTH]
    o += ATTN_WIDTH
    per_group = 3 * rr
    for g in range(gg):
        gt_ref[g] = y[:, o + g * per_group:o + (g + 1) * per_group]


def _nsa_in(h, norm, w_in, batch, seq):
    gg, rr, dh = N_KV_GROUPS, HEADS_PER_GROUP, HEAD_DIM
    nt = seq // ROW_TILE
    nc_tile = ROW_TILE // CMP_STRIDE
    a, b = ATTN_WIDTH + 6 * KV_WIDTH, ATTN_WIDTH + 6 * KV_WIDTH + N_GATES
    w_gates = w_in[:, a:b].reshape(D_MODEL, 3, gg, rr).transpose(0, 2, 1, 3).reshape(D_MODEL, N_GATES)
    w = jnp.concatenate([w_in[:, :a], w_in[:, b:], w_gates,
                         jnp.zeros((D_MODEL, GATES_PAD - N_GATES), w_in.dtype)], axis=1).astype(BF16)
    n = w.shape[1]
    qcols, sel_cols, win_cols, cmp_cols, ones_cols = _alibi_columns(seq)
    lead = lambda cols: jnp.pad(cols, [(0, 0)] * (cols.ndim - 1) + [(dh, 0)])
    const = lambda shape: pl.BlockSpec(shape, lambda i: (0,) * len(shape))
    by_pos = lambda width: pl.BlockSpec((ROW_TILE, width), lambda i: (i % nt, 0))
    per_b = lambda *tail: pl.BlockSpec((None, gg) + tail, lambda i: (i // nt, 0) + (0,) * (len(tail) - 2) + (i % nt, 0))
    outs = pl.pallas_call(
        _nsa_in_kernel,
        grid=(batch * nt,),
        in_specs=[pl.BlockSpec((ROW_TILE, D_MODEL), lambda i: (i, 0)), const((1, D_MODEL)), const((D_MODEL, n)),
                  const((gg, rr, LANES)), by_pos(2 * LANES), by_pos(LANES), const((1, LANES))],
        out_specs=[per_b(rr, ROW_TILE, LANES), per_b(ROW_TILE, 2 * LANES), per_b(ROW_TILE, LANES),
                   per_b(ROW_TILE, LANES), per_b(ROW_TILE, LANES),
                   per_b(nc_tile, CMP_STRIDE * dh), per_b(nc_tile, CMP_STRIDE * dh),
                   per_b(ROW_TILE, 3 * rr), pl.BlockSpec((ROW_TILE, ATTN_WIDTH), lambda i: (i, 0))],
        out_shape=[jax.ShapeDtypeStruct((batch, gg, rr, seq, LANES), BF16),
                   jax.ShapeDtypeStruct((batch, gg, seq, 2 * LANES), BF16),
                   jax.ShapeDtypeStruct((batch, gg, seq, LANES), BF16),
                   jax.ShapeDtypeStruct((batch, gg, seq, LANES), BF16),
                   jax.ShapeDtypeStruct((batch, gg, seq, LANES), BF16),
                   jax.ShapeDtypeStruct((batch, gg, seq // CMP_STRIDE, CMP_STRIDE * dh), F32),
                   jax.ShapeDtypeStruct((batch, gg, seq // CMP_STRIDE, CMP_STRIDE * dh), F32),
                   jax.ShapeDtypeStruct((batch, gg, seq, 3 * rr), F32),
                   jax.ShapeDtypeStruct((batch * seq, ATTN_WIDTH), F32)],
        scratch_shapes=[pltpu.VMEM((2 * KV_WIDTH // LANES, ROW_TILE, LANES), F32)],
        compiler_params=_params("parallel"),
        name="nsa_in_proj",
    )(h, norm.reshape(1, D_MODEL), w, lead(qcols), lead(sel_cols), lead(win_cols), lead(ones_cols)[None])
    return outs, cmp_cols, ones_cols


def _quarter_exchange(parts):
    quarter = lax.broadcasted_iota(jnp.int32, parts[0].shape, 1) // S5_PAIR_LANES
    outs = []
    for b in range(4):
        acc = None
        for a in range(4):
            moved = parts[a] if a == b else pltpu.roll(parts[a], ((a - b) % 4) * S5_PAIR_LANES, 1)
            acc = moved if acc is None else jnp.where(quarter == a, moved, acc)
        outs.append(acc)
    return outs


def _s5_in_kernel(h_ref, g_ref, w_ref, u_ref, z_ref, u_scr):
    xn = _rms(h_ref[...], g_ref[...])
    y = jnp.dot(xn.astype(BF16), w_ref[...], preferred_element_type=F32)
    z_ref[...] = y[:, SSM_WIDTH:]
    nch = ROW_TILE // S5_CHUNK
    for m in range(SSM_WIDTH // 128):
        u_scr[m] = y[:, m * 128:(m + 1) * 128]
        for k in range(S5_CHUNK // 4):
            steps = [u_scr[m, pl.ds(4 * k + a, nch, stride=S5_CHUNK), :] for a in range(4)]
            for b, tile in enumerate(_quarter_exchange(steps)):
                u_ref[4 * m + b, :, k * 128:(k + 1) * 128] = tile


def _s5_in(h, norm, w_in):
    t = h.shape[0]
    nch = ROW_TILE // S5_CHUNK
    row = lambda width: pl.BlockSpec((ROW_TILE, width), lambda i: (i, 0))
    return pl.pallas_call(
        _s5_in_kernel,
        grid=(t // ROW_TILE,),
        in_specs=[row(D_MODEL), pl.BlockSpec((1, D_MODEL), lambda i: (0, 0)),
                  pl.BlockSpec((D_MODEL, 2 * SSM_WIDTH), lambda i: (0, 0))],
        out_specs=[pl.BlockSpec((S5_PAIRS, nch, S5_PAIR), lambda i: (0, i, 0)), row(SSM_WIDTH)],
        out_shape=[jax.ShapeDtypeStruct((S5_PAIRS, t // S5_CHUNK, S5_PAIR), F32),
                   jax.ShapeDtypeStruct((t, SSM_WIDTH), F32)],
        scratch_shapes=[pltpu.VMEM((SSM_WIDTH // 128, ROW_TILE, 128), F32)],
        compiler_params=_params("parallel"),
        name="s5_in_proj",
    )(h, norm.reshape(1, D_MODEL), w_in.astype(BF16))


def _nsa_out_kernel(o_ref, z_ref, w_ref, res_ref, out_ref):
    a = o_ref[...] * _silu(z_ref[...])
    out_ref[...] = res_ref[...] + jnp.dot(a.astype(BF16), w_ref[...], preferred_element_type=F32)


def _nsa_out(o, z, w_out, res):
    t = o.shape[0]
    row = pl.BlockSpec((ROW_TILE, D_MODEL), lambda i: (i, 0))
    return pl.pallas_call(
        _nsa_out_kernel,
        grid=(t // ROW_TILE,),
        in_specs=[row, row, pl.BlockSpec((ATTN_WIDTH, D_MODEL), lambda i: (0, 0)), row],
        out_specs=row,
        out_shape=jax.ShapeDtypeStruct((t, D_MODEL), F32),
        compiler_params=_params("parallel"),
        name="nsa_out_proj",
    )(o, z, w_out.astype(BF16), res)


def _s5_out_kernel(y_ref, z_ref, wg_ref, wo_ref, res_ref, fn_ref, out_ref, y_scr, *, final):
    nch = ROW_TILE // S5_CHUNK
    for m in range(SSM_WIDTH // 128):
        for k in range(S5_CHUNK // 4):
            pairs = [y_ref[4 * m + b, :, k * 128:(k + 1) * 128] for b in range(4)]
            for a, tile in enumerate(_quarter_exchange(pairs)):
                y_scr[m, pl.ds(4 * k + a, nch, stride=S5_CHUNK), :] = tile
    y = jnp.concatenate([y_scr[m] for m in range(SSM_WIDTH // 128)], axis=1)
    gl = jnp.dot(y.astype(BF16), wg_ref[...], preferred_element_type=F32)
    v = gl[:, :SSM_WIDTH] * jax.nn.sigmoid(gl[:, SSM_WIDTH:])
    v = v * _silu(z_ref[...])
    h = res_ref[...] + jnp.dot(v.astype(BF16), wo_ref[...], preferred_element_type=F32)
    out_ref[...] = _rms(h, fn_ref[...]) if final else h


def _s5_out(y, z, w_glu, w_out, res, final_norm, final):
    t = z.shape[0]
    row = pl.BlockSpec((ROW_TILE, D_MODEL), lambda i: (i, 0))
    return pl.pallas_call(
        functools.partial(_s5_out_kernel, final=final),
        grid=(t // ROW_TILE,),
        in_specs=[pl.BlockSpec((S5_PAIRS, ROW_TILE // S5_CHUNK, S5_PAIR), lambda i: (0, i, 0)), row,
                  pl.BlockSpec((SSM_WIDTH, 2 * SSM_WIDTH), lambda i: (0, 0)),
                  pl.BlockSpec((SSM_WIDTH, D_MODEL), lambda i: (0, 0)), row,
                  pl.BlockSpec((1, D_MODEL), lambda i: (0, 0))],
        out_specs=row,
        out_shape=jax.ShapeDtypeStruct((t, D_MODEL), F32),
        scratch_shapes=[pltpu.VMEM((SSM_WIDTH // 128, ROW_TILE, 128), F32)],
        compiler_params=_params("parallel"),
        name="s5_glu_out_proj",
    )(y, z, w_glu.astype(BF16), w_out.astype(BF16), res, final_norm.reshape(1, D_MODEL))


def _cmp_one(a_ref, pe_ref, w1_ref, w2_ref, out_ref):
    nc = a_ref.shape[0]
    a = a_ref[...]
    pe = pe_ref[...]
    h_top = jnp.dot((a + pe[0:1]).astype(BF16), w1_ref[0], preferred_element_type=F32)
    h_bot = jnp.dot((a + pe[1:2]).astype(BF16), w1_ref[1], preferred_element_type=F32)
    hid = _silu(h_top + pltpu.roll(h_bot, nc - 1, 0))
    out = jnp.dot(hid.astype(BF16), w2_ref[...], preferred_element_type=F32)
    keep = lax.broadcasted_iota(jnp.int32, out.shape, 0) < nc - 1
    out_ref[...] = jnp.where(keep, out, 0.0).astype(BF16)


def _cmp_kernel(ka_ref, va_ref, kpe_ref, kw1_ref, kw2_ref, vpe_ref, vw1_ref, vw2_ref, kc_ref, vc_ref):
    _cmp_one(ka_ref, kpe_ref, kw1_ref, kw2_ref, kc_ref)
    _cmp_one(va_ref, vpe_ref, vw1_ref, vw2_ref, vc_ref)


def _compress(ka, va, k_pe, k_w1, k_w2, v_pe, v_w1, v_w2):
    bg, nc, half = ka.shape
    blk = pl.BlockSpec((None, nc, half), lambda i: (i, 0, 0))
    full = lambda shape: pl.BlockSpec(shape, lambda i: (0,) * len(shape))
    prep = lambda pe, w1, w2: (pe.reshape(2, half), w1.reshape(2, half, CMP_HIDDEN).astype(BF16), w2.astype(BF16))
    out = pl.BlockSpec((None, nc, HEAD_DIM), lambda i: (i, 0, 0))
    wspecs = [full((2, half)), full((2, half, CMP_HIDDEN)), full((CMP_HIDDEN, HEAD_DIM))]
    return pl.pallas_call(
        _cmp_kernel,
        grid=(bg,),
        in_specs=[blk, blk] + wspecs + wspecs,
        out_specs=[out, out],
        out_shape=[jax.ShapeDtypeStruct((bg, nc, HEAD_DIM), BF16)] * 2,
        compiler_params=_params("parallel"),
        name="nsa_compress",
    )(ka, va, *prep(k_pe, k_w1, k_w2), *prep(v_pe, v_w1, v_w2))


def _dot_nt(a, b):
    return lax.dot_general(a, b, (((1,), (1,)), ((), ())), preferred_element_type=F32)


def _split3(x):
    hi = x.astype(BF16)
    r1 = x - hi.astype(F32)
    mid = r1.astype(BF16)
    lo = (r1 - mid.astype(F32)).astype(BF16)
    return hi, mid, lo


def _softmax_values(parts, m_row, va):
    p = [jnp.exp(sc - m_row) for sc in parts]
    acc = jnp.dot(jnp.concatenate(p, axis=1).astype(BF16), va, preferred_element_type=F32)
    return acc, p


def _row_max(parts):
    m = parts[0]
    for sc in parts[1:]:
        m = jnp.maximum(m, sc)
    return jnp.broadcast_to(jnp.max(m, axis=-1, keepdims=True), m.shape)


def _attn_kernel(qa_ref, ksa_ref, vsa_ref, kwa_ref, vwa_ref, kca_ref, vca_ref, gt_ref, ovt_ref,
                 o_ref, m_scr, top_scr, acc_scr, oc_scr, ow_scr, qaug_scr, used_smem, *, seq, tq, tk):
    nb = seq // SEL_BLOCK
    nc = seq // CMP_STRIDE
    rows = HEADS_PER_GROUP * tq
    q0 = pl.program_id(1) * tq

    qa = qa_ref[...].reshape(rows, LANES)
    row_l = lax.broadcasted_iota(jnp.int32, (rows, LANES), 0)
    lane = lax.broadcasted_iota(jnp.int32, (rows, LANES), 1)
    t_l = q0 + (row_l - (row_l // tq) * tq)

    ntile = tk // LANES
    n_before = q0 // tk

    def tile_scores(kt, causal):
        kb = pl.multiple_of(kt * tk, tk)
        sc = _dot_nt(qaug_scr[...], ksa_ref[pl.ds(kb, tk), :])
        if causal:
            key = kb + lax.broadcasted_iota(jnp.int32, (rows, tk), 1)
            sc = jnp.where(jnp.concatenate([t_l] * ntile, axis=1) >= key, sc, NEG_INF)
        return kb, sc

    def lane_max(sc):
        part = sc[:, :LANES]
        for c in range(1, ntile):
            part = jnp.maximum(part, sc[:, c * LANES:(c + 1) * LANES])
        return part

    def value_tile(kt, causal, track):
        kb, sc = tile_scores(kt, causal)
        if track:
            top_scr[...] = jnp.maximum(top_scr[...], lane_max(sc))
        pr = jnp.exp(sc - jnp.concatenate([m_scr[...]] * ntile, axis=1))
        acc_scr[...] += jnp.dot(pr.astype(BF16), vsa_ref[pl.ds(kb, tk), :], preferred_element_type=F32)

    def front(nchunk):
        width = nchunk * LANES
        sc_all = _dot_nt(qa, kca_ref[0:width, :])
        last_ok = (t_l - (CMP_BLOCK - 1)) >> 4
        parts = [jnp.where(lane + c * LANES <= last_ok, sc_all[:, c * LANES:(c + 1) * LANES], NEG_INF)
                 for c in range(nchunk)]
        m_c = jnp.where(t_l >= CMP_BLOCK - 1, _row_max(parts), -NEG_INF)
        acc_c, p_c = _softmax_values(parts, m_c, vca_ref[0:width, :])
        inv_c = jnp.broadcast_to(1.0 / jnp.maximum(acc_c[:, HEAD_DIM:HEAD_DIM + 1], 1e-30), (rows, LANES))
        oc_scr[...] = acc_c[:, :HEAD_DIM] * inv_c[:, :HEAD_DIM]
        p = jnp.concatenate([pc * inv_c for pc in p_c], axis=1)

        wlen = WINDOW + tq
        w0 = pl.multiple_of(jnp.maximum(q0 - WINDOW, 0), tq)
        sw_all = _dot_nt(qa, kwa_ref[pl.ds(w0, wlen), :])
        parts = []
        for c in range(wlen // LANES):
            key = w0 + c * LANES + lane
            ok = key <= t_l
            if c * LANES < tq:
                ok = ok & (key > t_l - WINDOW)
            parts.append(jnp.where(ok, sw_all[:, c * LANES:(c + 1) * LANES], NEG_INF))
        acc_w, _ = _softmax_values(parts, _row_max(parts), vwa_ref[pl.ds(w0, wlen), :])
        ow_scr[...] = acc_w[:, :HEAD_DIM] / jnp.maximum(acc_w[:, HEAD_DIM:HEAD_DIM + 1], 1e-30)

        psum = p[0:tq] + p[tq:2 * tq] + p[2 * tq:3 * tq] + p[3 * tq:4 * tq]
        ovt = ovt_ref[:, 0:width]
        imp = sum(_dot_nt(ovt, part) for part in _split3(psum))
        j = lax.broadcasted_iota(jnp.int32, (nb, tq), 0)
        jt = (q0 + lax.broadcasted_iota(jnp.int32, (nb, tq), 1)) // SEL_BLOCK
        forced = (j == 0) | (j == jt) | (j == jt - 1)
        sel_t = jnp.where(forced, 1.0, 0.0)
        vals = jnp.where(forced, -3e38, jnp.where(j > jt, -FORCE_SCORE, imp))
        jf = j.astype(F32)
        for _ in range(min(SEL_TOPK, nb) - 3):
            best = jnp.max(vals, axis=0, keepdims=True)
            first = jnp.min(jnp.where(vals == best, jf, float(nb)), axis=0, keepdims=True)
            hit = jf == first
            sel_t = jnp.where(hit, 1.0, sel_t)
            vals = jnp.where(hit, -3e38, vals)

        not_chosen = ((1.0 - sel_t) * MASK_BIAS).T.astype(BF16)
        qaug_scr[:, :LANES] = qa
        for r in range(HEADS_PER_GROUP):
            qaug_scr[r * tq:(r + 1) * tq, LANES:LANES + nb] = not_chosen
        if nb < LANES:
            qaug_scr[:, LANES + nb:] = jnp.zeros((rows, LANES - nb), BF16)
        blocks_per_tile = tk // SEL_BLOCK
        for i in range(seq // tk):
            chosen_here = jnp.max(sel_t[i * blocks_per_tile:(i + 1) * blocks_per_tile, :])
            used_smem[i] = (chosen_here > 0.5).astype(jnp.int32)

        kb_diag, sc_diag = tile_scores(n_before, True)
        m_diag = jnp.broadcast_to(jnp.max(lane_max(sc_diag), axis=-1, keepdims=True), (rows, LANES))
        m_scr[...] = m_diag
        pr_diag = jnp.exp(sc_diag - jnp.concatenate([m_diag] * ntile, axis=1))
        acc_scr[...] = jnp.dot(pr_diag.astype(BF16), vsa_ref[pl.ds(kb_diag, tk), :], preferred_element_type=F32)
        top_scr[...] = jnp.full((rows, LANES), NEG_INF, F32)

    all_chunks = nc // LANES
    chunks_needed = (q0 + tq - CMP_BLOCK) // (CMP_STRIDE * LANES) + 1
    covered = 0
    for nchunk in sorted({max(1, all_chunks // 2), all_chunks}):
        pl.when((chunks_needed > covered) & (chunks_needed <= nchunk))(functools.partial(front, nchunk))
        covered = nchunk

    def tiles_before(track):
        def step(kt, carry):
            @pl.when(used_smem[kt] > 0)
            def _():
                value_tile(kt, False, track)
            return carry
        lax.fori_loop(0, n_before, step, 0)

    tiles_before(True)

    gate = jax.nn.sigmoid(gt_ref[...])

    def write_output():
        acc = acc_scr[...]
        o_s = acc[:, :HEAD_DIM] / jnp.maximum(acc[:, HEAD_DIM:HEAD_DIM + 1], 1e-30)
        for r in range(HEADS_PER_GROUP):
            sl = slice(r * tq, (r + 1) * tq)
            gc = lambda x: gate[:, x * HEADS_PER_GROUP + r:x * HEADS_PER_GROUP + r + 1]
            o_ref[:, r * HEAD_DIM:(r + 1) * HEAD_DIM] = (gc(0) * oc_scr[sl, :] + gc(1) * o_s[sl]
                                                         + gc(2) * ow_scr[sl, :])

    overflow_risk = jnp.max(top_scr[...] - m_scr[...]) > EXP_HEADROOM
    write_output()

    @pl.when(overflow_risk)
    def _():
        exact = jnp.maximum(top_scr[...], m_scr[...])
        m_scr[...] = jnp.broadcast_to(jnp.max(exact, axis=-1, keepdims=True), (rows, LANES))
        acc_scr[...] = jnp.zeros((rows, LANES), F32)
        tiles_before(False)
        value_tile(n_before, True, False)
        write_output()


def _attention(qa, ksa, vsa, kwa, vwa, kca, vca, gt, batch, seq):
    tq, tk = ATTN_TQ, ATTN_TK
    bg = batch * N_KV_GROUPS
    nq = seq // tq
    nb, nc = seq // SEL_BLOCK, seq // CMP_STRIDE
    assert SEL_TOPK <= nb <= LANES and seq % tk == 0 and tk % tq == 0 and tq % LANES == 0 and nc % LANES == 0
    rows = HEADS_PER_GROUP * tq
    gq = HEADS_PER_GROUP * HEAD_DIM
    cs = jnp.arange(nc)[None, :] * CMP_STRIDE
    ss = jnp.arange(nb)[:, None] * SEL_BLOCK
    ovt = ((cs <= ss + SEL_BLOCK - 1) & (cs + CMP_BLOCK - 1 >= ss)).astype(BF16)
    per_bg = lambda n, w: pl.BlockSpec((None, n, w), lambda i, j: (i, 0, 0))
    return pl.pallas_call(
        functools.partial(_attn_kernel, seq=seq, tq=tq, tk=tk),
        grid=(bg, nq),
        in_specs=[pl.BlockSpec((None, HEADS_PER_GROUP, tq, LANES), lambda i, j: (i, 0, j, 0)),
                  per_bg(seq, 2 * LANES), per_bg(seq, LANES), per_bg(seq, LANES), per_bg(seq, LANES),
                  per_bg(nc, LANES), per_bg(nc, LANES),
                  pl.BlockSpec((None, tq, 3 * HEADS_PER_GROUP), lambda i, j: (i, j, 0)),
                  pl.BlockSpec((nb, nc), lambda i, j: (0, 0))],
        out_specs=pl.BlockSpec((tq, gq), lambda i, j: ((i // N_KV_GROUPS) * nq + j, i % N_KV_GROUPS)),
        out_shape=jax.ShapeDtypeStruct((batch * seq, ATTN_WIDTH), F32),
        scratch_shapes=[pltpu.VMEM((rows, LANES), F32), pltpu.VMEM((rows, LANES), F32), pltpu.VMEM((rows, LANES), F32),
                        pltpu.VMEM((rows, HEAD_DIM), F32), pltpu.VMEM((rows, HEAD_DIM), F32),
                        pltpu.VMEM((rows, 2 * LANES), BF16), pltpu.SMEM((seq // tk,), jnp.int32)],
        compiler_params=_params("parallel", "arbitrary"),
        name="nsa_attention",
    )(qa, ksa, vsa, kwa, vwa, kca, vca, gt, ovt)


def _alibi_columns(seq):
    dh = HEAD_DIM
    nb, nc = seq // SEL_BLOCK, seq // CMP_STRIDE
    head = jnp.arange(1, N_HEADS + 1, dtype=F32).reshape(N_KV_GROUPS, HEADS_PER_GROUP)
    parts = _split3(jnp.exp2(-8.0 * head / N_HEADS))
    qcols = jnp.stack([float(SEL_BLOCK) * p.astype(F32) for p in parts] + [p.astype(F32) for p in parts], axis=-1)
    qcols = jnp.pad(qcols, ((0, 0), (0, 0), (0, LANES - dh - qcols.shape[-1]))).astype(BF16)

    def key_cols(u, scale):
        cols = jnp.stack([u // SEL_BLOCK] * 3 + [u % SEL_BLOCK] * 3, axis=-1).astype(F32) * scale
        return jnp.pad(cols, ((0, 0), (0, LANES - dh - cols.shape[-1])))

    pos = jnp.arange(seq)
    onehot = (pos[:, None] // SEL_BLOCK == jnp.arange(LANES)[None, :]) & (jnp.arange(LANES) < nb)
    win_cols = key_cols(pos, 1.0).astype(BF16)
    sel_cols = jnp.concatenate([win_cols, onehot.astype(BF16)], axis=1)
    cmp_cols = key_cols(jnp.arange(nc), float(CMP_STRIDE)).astype(BF16)
    ones_cols = (jnp.arange(LANES - dh) == 0).astype(BF16)
    return qcols, sel_cols, win_cols, cmp_cols, ones_cols


def _nsa_layer(h, batch, seq, norm, w_in, k_pe, k_w1, k_w2, v_pe, v_w1, v_w2, w_out):
    gg = N_KV_GROUPS
    bg = batch * gg
    (qa, ksa, vsa, kwa, vwa, kc_in, vc_in, gt, z), cmp_cols, ones_cols = _nsa_in(h, norm, w_in, batch, seq)
    merge = lambda x: x.reshape((bg,) + x.shape[2:])
    with_cols = lambda x, cols: jnp.concatenate([x, jnp.broadcast_to(cols, x.shape[:-1] + cols.shape[-1:])], axis=-1)
    kc, vc = _compress(merge(kc_in), merge(vc_in), k_pe, k_w1, k_w2, v_pe, v_w1, v_w2)
    o = _attention(merge(qa), merge(ksa), merge(vsa), merge(kwa), merge(vwa),
                   with_cols(kc, cmp_cols), with_cols(vc, ones_cols), merge(gt), batch, seq)
    return _nsa_out(o, z, w_out, h)


def _dot_bf16(a, b):
    return jnp.dot(a.astype(BF16), b.astype(BF16), preferred_element_type=F32)


def _member(shape, axis, width):
    return (lax.broadcasted_iota(jnp.int32, shape, axis) // width) % 2


def _s5_state_kernel(u_ref, w_ref, xr_ref, xi_ref):
    w = w_ref[...]
    p = SSM_STATE
    wide = jnp.concatenate([w[:, :p], w[:, :p], w[:, p:], w[:, p:]], axis=1)
    w_pair = jnp.where(_member(wide.shape, 0, SSM_GROUP) == _member(wide.shape, 1, p), wide, 0.0)
    x = _dot_bf16(u_ref[...], w_pair)
    half = x.shape[1] // 2
    xr_ref[...] = x[:, :half]
    xi_ref[...] = x[:, half:]


def _s5_scan_kernel(xr_ref, xi_ref, ar_ref, ai_ref, cr_ref, ci_ref, *, batch, nchunk):
    ar, ai = ar_ref[...], ai_ref[...]
    width = ar.shape[1]

    def step(n, carry):
        new = []
        for b in range(batch):
            cr, ci = carry[2 * b], carry[2 * b + 1]
            idx = b * nchunk + n
            cr_ref[pl.ds(idx, 1), :] = cr
            ci_ref[pl.ds(idx, 1), :] = ci
            lr, li = xr_ref[pl.ds(idx, 1), :], xi_ref[pl.ds(idx, 1), :]
            new += [ar * cr - ai * ci + lr, ar * ci + ai * cr + li]
        return tuple(new)

    zero = jnp.zeros((1, width), F32)
    lax.fori_loop(0, nchunk, step, (zero,) * (2 * batch))


def _s5_y_kernel(u_ref, cr_ref, ci_ref, k_ref, v_ref, d_ref, y_ref, m_scr):
    k = k_ref[...]
    col = lax.broadcasted_iota(jnp.int32, k.shape, 1)
    m_scr[0:S5_PAIR_LANES, :] = k
    for s in range(1, S5_CHUNK):
        shifted = pltpu.roll(k, s * S5_PAIR_LANES, 1)
        m_scr[s * S5_PAIR_LANES:(s + 1) * S5_PAIR_LANES, :] = jnp.where(col >= s * S5_PAIR_LANES, shifted, 0.0)
    v = v_ref[...]
    p = SSM_STATE
    tall = jnp.concatenate([v[:p], v[:p], v[p:], v[p:]], axis=0)
    v_pair = jnp.where(_member(tall.shape, 0, p) == _member(tall.shape, 1, SSM_GROUP), tall, 0.0)
    u = u_ref[...]
    carry = jnp.concatenate([cr_ref[...], ci_ref[...]], axis=1)
    y = _dot_bf16(u, m_scr[...]) + _dot_bf16(carry, v_pair) + d_ref[...] * u
    y_ref[...] = jax.nn.gelu(y)


def _s5_matrices(log_dt, lambda_re, lambda_im, b_re, b_im, c_re, c_im, d_skip):
    hp = lax.Precision.HIGHEST
    gn, pn, cn, ln = SSM_GROUPS, SSM_STATE, SSM_GROUP, S5_CHUNK
    dt = jnp.exp(log_dt.astype(F32))[:, None]
    lre = jnp.minimum(lambda_re.astype(F32), -1e-4)
    lim = lambda_im.astype(F32)
    mag = jnp.exp(lre * dt)
    ab_re, ab_im = mag * jnp.cos(lim * dt), mag * jnp.sin(lim * dt)
    den = lre * lre + lim * lim
    nr = ab_re - 1.0
    coef_re = (nr * lre + ab_im * lim) / den
    coef_im = (ab_im * lre - nr * lim) / den
    br32, bi32 = b_re.astype(F32), b_im.astype(F32)
    bb_re = coef_re[..., None] * br32 - coef_im[..., None] * bi32
    bb_im = coef_re[..., None] * bi32 + coef_im[..., None] * br32
    cr32, ci32 = c_re.astype(F32), c_im.astype(F32)
    k_steps = jnp.arange(ln + 1, dtype=F32)[:, None, None]
    pw_mag = jnp.exp(k_steps * (lre * dt))
    pw_re, pw_im = pw_mag * jnp.cos(k_steps * (lim * dt)), pw_mag * jnp.sin(k_steps * (lim * dt))
    ab_b_re = pw_re[:ln, :, :, None] * bb_re - pw_im[:ln, :, :, None] * bb_im
    ab_b_im = pw_re[:ln, :, :, None] * bb_im + pw_im[:ln, :, :, None] * bb_re
    kern = (jnp.einsum('gop,kgpi->kgoi', cr32, ab_b_re, precision=hp)
            - jnp.einsum('gop,kgpi->kgoi', ci32, ab_b_im, precision=hp))
    gp = gn // 2
    same = jnp.eye(2, dtype=F32)
    k_pair = kern.reshape(ln, gp, 2, cn, cn).transpose(1, 2, 4, 0, 3)
    k_pair = k_pair[:, :, :, :, None, :] * same[None, :, None, None, :, None]
    k_pair = k_pair.reshape(gp, S5_PAIR_LANES, S5_PAIR)
    w = jnp.stack([ab_b_re[::-1], ab_b_im[::-1]]).reshape(2, ln, gp, 2, pn, cn)
    w_pair = w.transpose(2, 1, 3, 5, 0, 4).reshape(gp, S5_PAIR, 2 * pn)
    ar1, ai1 = pw_re[1:], pw_im[1:]
    v_re = cr32[None] * ar1[:, :, None, :] - ci32[None] * ai1[:, :, None, :]
    v_im = -(cr32[None] * ai1[:, :, None, :] + ci32[None] * ar1[:, :, None, :])
    v = jnp.stack([v_re, v_im]).reshape(2, ln, gp, 2, cn, pn)
    v_pair = v.transpose(2, 0, 5, 1, 3, 4).reshape(gp, 2 * pn, S5_PAIR)
    d_pair = jnp.tile(d_skip.astype(F32).reshape(gp, 1, 2 * cn), (1, 1, ln))
    a_end_re = pw_re[ln].reshape(1, gn * pn)
    a_end_im = pw_im[ln].reshape(1, gn * pn)
    return k_pair, w_pair, v_pair, d_pair, a_end_re, a_end_im


def _s5_layer(h, batch, seq, norm, w_in, log_dt, lambda_re, lambda_im, b_re, b_im, c_re, c_im, d_skip,
              w_glu, w_out, final_norm, final):
    gp, ln, cn = SSM_GROUPS // 2, S5_CHUNK, SSM_GROUP
    nchunk = seq // ln
    rows = batch * nchunk
    state_w = 2 * SSM_STATE
    k_pair, w_pair, v_pair, d_pair, a_re, a_im = _s5_matrices(
        log_dt, lambda_re, lambda_im, b_re, b_im, c_re, c_im, d_skip)
    up, z = _s5_in(h, norm, w_in)
    pair3 = lambda a, b: pl.BlockSpec((None, a, b), lambda i: (i, 0, 0))
    slab = pl.BlockSpec((rows, state_w), lambda i: (0, i))
    xr, xi = pl.pallas_call(
        _s5_state_kernel,
        grid=(gp,),
        in_specs=[pair3(rows, S5_PAIR), pair3(S5_PAIR, state_w)],
        out_specs=[slab, slab],
        out_shape=[jax.ShapeDtypeStruct((rows, gp * state_w), F32)] * 2,
        compiler_params=_params("parallel"),
        name="s5_chunk_state",
    )(up, w_pair)
    scan_w = 4 * state_w
    wide = pl.BlockSpec((rows, scan_w), lambda i: (0, i))
    coef = pl.BlockSpec((1, scan_w), lambda i: (0, i))
    cr, ci = pl.pallas_call(
        functools.partial(_s5_scan_kernel, batch=batch, nchunk=nchunk),
        grid=(gp * state_w // scan_w,),
        in_specs=[wide, wide, coef, coef],
        out_specs=[wide, wide],
        out_shape=[jax.ShapeDtypeStruct((rows, gp * state_w), F32)] * 2,
        compiler_params=_params("parallel"),
        name="s5_carry_scan",
    )(xr, xi, a_re, a_im)
    yp = pl.pallas_call(
        _s5_y_kernel,
        grid=(gp,),
        in_specs=[pair3(rows, S5_PAIR), slab, slab, pair3(S5_PAIR_LANES, S5_PAIR), pair3(state_w, S5_PAIR),
                  pair3(1, S5_PAIR)],
        out_specs=pair3(rows, S5_PAIR),
        out_shape=jax.ShapeDtypeStruct((gp, rows, S5_PAIR), F32),
        scratch_shapes=[pltpu.VMEM((S5_PAIR, S5_PAIR), F32)],
        compiler_params=_params("parallel"),
        name="s5_chunk_output",
    )(up, cr, ci, k_pair, v_pair, d_pair)
    return _s5_out(yp, z, w_glu, w_out, h, final_norm, final)


def kernel(x, l0_norm, l0_w_in, l0_cmp_k_pe, l0_cmp_k_w1, l0_cmp_k_w2, l0_cmp_v_pe, l0_cmp_v_w1, l0_cmp_v_w2, l0_w_out, l1_norm, l1_w_in, l1_log_dt, l1_lambda_re, l1_lambda_im, l1_b_re, l1_b_im, l1_c_re, l1_c_im, l1_d, l1_w_glu, l1_w_out, l2_norm, l2_w_in, l2_cmp_k_pe, l2_cmp_k_w1, l2_cmp_k_w2, l2_cmp_v_pe, l2_cmp_v_w1, l2_cmp_v_w2, l2_w_out, l3_norm, l3_w_in, l3_log_dt, l3_lambda_re, l3_lambda_im, l3_b_re, l3_b_im, l3_c_re, l3_c_im, l3_d, l3_w_glu, l3_w_out, final_norm):
    batch, seq, _ = x.shape
    h = x.reshape(batch * seq, D_MODEL)
    h = _nsa_layer(h, batch, seq, l0_norm, l0_w_in, l0_cmp_k_pe, l0_cmp_k_w1, l0_cmp_k_w2,
                   l0_cmp_v_pe, l0_cmp_v_w1, l0_cmp_v_w2, l0_w_out)
    h = _s5_layer(h, batch, seq, l1_norm, l1_w_in, l1_log_dt, l1_lambda_re, l1_lambda_im, l1_b_re, l1_b_im,
                  l1_c_re, l1_c_im, l1_d, l1_w_glu, l1_w_out, final_norm, False)
    h = _nsa_layer(h, batch, seq, l2_norm, l2_w_in, l2_cmp_k_pe, l2_cmp_k_w1, l2_cmp_k_w2,
                   l2_cmp_v_pe, l2_cmp_v_w1, l2_cmp_v_w2, l2_w_out)
    h = _s5_layer(h, batch, seq, l3_norm, l3_w_in, l3_log_dt, l3_lambda_re, l3_lambda_im, l3_b_re, l3_b_im,
                  l3_c_re, l3_c_im, l3_d, l3_w_glu, l3_w_out, final_norm, True)
    return h.reshape(batch, seq, D_MODEL)
```

```python
import functools

import jax
import jax.numpy as jnp
from jax import lax
from jax.experimental import pallas as pl
from jax.experimental.pallas import tpu as pltpu

F32 = jnp.float32
BF16 = jnp.bfloat16

D_MODEL = 1024
EPS = 1e-6
NEG_INF = -1e30
FORCE_SCORE = 1e9

N_HEADS = 16
HEAD_DIM = 64
N_KV_GROUPS = 4
HEADS_PER_GROUP = N_HEADS // N_KV_GROUPS
ATTN_WIDTH = N_HEADS * HEAD_DIM
KV_WIDTH = N_KV_GROUPS * HEAD_DIM
CMP_BLOCK = 32
CMP_STRIDE = 16
CMP_HIDDEN = 256
SEL_BLOCK = 64
SEL_TOPK = 16
WINDOW = 512
N_GATES = 3 * N_HEADS
GATES_PAD = 128

SSM_WIDTH = D_MODEL
SSM_GROUP = 16
SSM_GROUPS = SSM_WIDTH // SSM_GROUP
SSM_STATE = 64
S5_CHUNK = 16
S5_PAIRS = SSM_GROUPS // 2
S5_PAIR_LANES = 2 * SSM_GROUP
S5_PAIR = S5_CHUNK * S5_PAIR_LANES

ROW_TILE = 512
ATTN_TQ = 256
ATTN_TK = 512
LANES = 128
MASK_BIAS = -2.0 ** 100
LOG2E = 1.4426950408889634
EXP_HEADROOM = 80.0
VMEM_LIMIT = 56 * 1024 * 1024


def _params(*sem):
    return pltpu.CompilerParams(dimension_semantics=sem, vmem_limit_bytes=VMEM_LIMIT)


def _rms(x, g):
    return x * lax.rsqrt(jnp.mean(x * x, axis=-1, keepdims=True) + EPS) * g


def _silu(x):
    return x * jax.nn.sigmoid(x)


def _nsa_in_kernel(h_ref, g_ref, w_ref, qcols_ref, selc_ref, winc_ref, ones_ref,
                   qa_ref, ksa_ref, vsa_ref, kwa_ref, vwa_ref, kcin_ref, vcin_ref, gt_ref, z_ref, c_scr):
    gg, rr, dh = N_KV_GROUPS, HEADS_PER_GROUP, HEAD_DIM
    xn = _rms(h_ref[...], g_ref[...])
    y = jnp.dot(xn.astype(BF16), w_ref[...], preferred_element_type=F32)
    for g in range(gg):
        for r in range(rr):
            k = g * rr + r
            qa_ref[g, r] = jnp.broadcast_to(qcols_ref[g, r:r + 1, :], (ROW_TILE, LANES))
            qa_ref[g, r, :, :dh] = (y[:, k * dh:(k + 1) * dh] * (dh ** -0.5 * LOG2E)).astype(BF16)
    o = ATTN_WIDTH + 2 * KV_WIDTH
    consts = (selc_ref[...], jnp.broadcast_to(ones_ref[...], (ROW_TILE, LANES)), winc_ref[...],
              jnp.broadcast_to(ones_ref[...], (ROW_TILE, LANES)))
    for kind, ref in enumerate((ksa_ref, vsa_ref, kwa_ref, vwa_ref)):
        for g in range(gg):
            c = o + (kind * gg + g) * dh
            ref[g] = consts[kind]
            ref[g, :, :dh] = y[:, c:c + dh].astype(BF16)
    lane = lax.broadcasted_iota(jnp.int32, (ROW_TILE // CMP_STRIDE, LANES), 1)
    for n in range(2 * KV_WIDTH // LANES):
        c_scr[n] = y[:, ATTN_WIDTH + n * LANES:ATTN_WIDTH + (n + 1) * LANES]
        kind, g0 = divmod(2 * n, gg)
        for a in range(CMP_STRIDE // 2):
            even = c_scr[n, pl.ds(2 * a, ROW_TILE // CMP_STRIDE, stride=CMP_STRIDE), :]
            odd = c_scr[n, pl.ds(2 * a + 1, ROW_TILE // CMP_STRIDE, stride=CMP_STRIDE), :]
            dst = (kcin_ref, vcin_ref)[kind]
            dst[g0, :, a * LANES:(a + 1) * LANES] = jnp.where(lane < dh, even, pltpu.roll(odd, dh, 1))
            dst[g0 + 1, :, a * LANES:(a + 1) * LANES] = jnp.where(lane < dh, pltpu.roll(even, dh, 1), odd)
    o += 4 * KV_WIDTH
    z_ref[...] = y[:, o:o + ATTN_WIDTH]
    o += ATTN_WIDTH
    per_group = 3 * rr
    for g in range(gg):
        gt_ref[g] = y[:, o + g * per_group:o + (g + 1) * per_group]


def _nsa_in(h, norm, w_in, batch, seq):
    gg, rr, dh = N_KV_GROUPS, HEADS_PER_GROUP, HEAD_DIM
    nt = seq // ROW_TILE
    nc_tile = ROW_TILE // CMP_STRIDE
    a, b = ATTN_WIDTH + 6 * KV_WIDTH, ATTN_WIDTH + 6 * KV_WIDTH + N_GATES
    w_gates = w_in[:, a:b].reshape(D_MODEL, 3, gg, rr).transpose(0, 2, 1, 3).reshape(D_MODEL, N_GATES)
    w = jnp.concatenate([w_in[:, :a], w_in[:, b:], w_gates,
                         jnp.zeros((D_MODEL, GATES_PAD - N_GATES), w_in.dtype)], axis=1).astype(BF16)
    n = w.shape[1]
    qcols, sel_cols, win_cols, cmp_cols, ones_cols = _alibi_columns(seq)
    lead = lambda cols: jnp.pad(cols, [(0, 0)] * (cols.ndim - 1) + [(dh, 0)])
    const = lambda shape: pl.BlockSpec(shape, lambda i: (0,) * len(shape))
    by_pos = lambda width: pl.BlockSpec((ROW_TILE, width), lambda i: (i % nt, 0))
    per_b = lambda *tail: pl.BlockSpec((None, gg) + tail, lambda i: (i // nt, 0) + (0,) * (len(tail) - 2) + (i % nt, 0))
    outs = pl.pallas_call(
        _nsa_in_kernel,
        grid=(batch * nt,),
        in_specs=[pl.BlockSpec((ROW_TILE, D_MODEL), lambda i: (i, 0)), const((1, D_MODEL)), const((D_MODEL, n)),
                  const((gg, rr, LANES)), by_pos(2 * LANES), by_pos(LANES), const((1, LANES))],
        out_specs=[per_b(rr, ROW_TILE, LANES), per_b(ROW_TILE, 2 * LANES), per_b(ROW_TILE, LANES),
                   per_b(ROW_TILE, LANES), per_b(ROW_TILE, LANES),
                   per_b(nc_tile, CMP_STRIDE * dh), per_b(nc_tile, CMP_STRIDE * dh),
                   per_b(ROW_TILE, 3 * rr), pl.BlockSpec((ROW_TILE, ATTN_WIDTH), lambda i: (i, 0))],
        out_shape=[jax.ShapeDtypeStruct((batch, gg, rr, seq, LANES), BF16),
                   jax.ShapeDtypeStruct((batch, gg, seq, 2 * LANES), BF16),
                   jax.ShapeDtypeStruct((batch, gg, seq, LANES), BF16),
                   jax.ShapeDtypeStruct((batch, gg, seq, LANES), BF16),
                   jax.ShapeDtypeStruct((batch, gg, seq, LANES), BF16),
                   jax.ShapeDtypeStruct((batch, gg, seq // CMP_STRIDE, CMP_STRIDE * dh), F32),
                   jax.ShapeDtypeStruct((batch, gg, seq // CMP_STRIDE, CMP_STRIDE * dh), F32),
                   jax.ShapeDtypeStruct((batch, gg, seq, 3 * rr), F32),
                   jax.ShapeDtypeStruct((batch * seq, ATTN_WIDTH), F32)],
        scratch_shapes=[pltpu.VMEM((2 * KV_WIDTH // LANES, ROW_TILE, LANES), F32)],
        compiler_params=_params("parallel"),
        name="nsa_in_proj",
    )(h, norm.reshape(1, D_MODEL), w, lead(qcols), lead(sel_cols), lead(win_cols), lead(ones_cols)[None])
    return outs, cmp_cols, ones_cols


def _quarter_exchange(parts):
    quarter = lax.broadcasted_iota(jnp.int32, parts[0].shape, 1) // S5_PAIR_LANES
    outs = []
    for b in range(4):
        acc = None
        for a in range(4):
            moved = parts[a] if a == b else pltpu.roll(parts[a], ((a - b) % 4) * S5_PAIR_LANES, 1)
            acc = moved if acc is None else jnp.where(quarter == a, moved, acc)
        outs.append(acc)
    return outs


def _s5_in_kernel(h_ref, g_ref, w_ref, u_ref, z_ref, u_scr):
    xn = _rms(h_ref[...], g_ref[...])
    y = jnp.dot(xn.astype(BF16), w_ref[...], preferred_element_type=F32)
    z_ref[...] = y[:, SSM_WIDTH:]
    nch = ROW_TILE // S5_CHUNK
    for m in range(SSM_WIDTH // 128):
        u_scr[m] = y[:, m * 128:(m + 1) * 128]
        for k in range(S5_CHUNK // 4):
            steps = [u_scr[m, pl.ds(4 * k + a, nch, stride=S5_CHUNK), :] for a in range(4)]
            for b, tile in enumerate(_quarter_exchange(steps)):
                u_ref[4 * m + b, :, k * 128:(k + 1) * 128] = tile


def _s5_in(h, norm, w_in):
    t = h.shape[0]
    nch = ROW_TILE // S5_CHUNK
    row = lambda width: pl.BlockSpec((ROW_TILE, width), lambda i: (i, 0))
    return pl.pallas_call(
        _s5_in_kernel,
        grid=(t // ROW_TILE,),
        in_specs=[row(D_MODEL), pl.BlockSpec((1, D_MODEL), lambda i: (0, 0)),
                  pl.BlockSpec((D_MODEL, 2 * SSM_WIDTH), lambda i: (0, 0))],
        out_specs=[pl.BlockSpec((S5_PAIRS, nch, S5_PAIR), lambda i: (0, i, 0)), row(SSM_WIDTH)],
        out_shape=[jax.ShapeDtypeStruct((S5_PAIRS, t // S5_CHUNK, S5_PAIR), F32),
                   jax.ShapeDtypeStruct((t, SSM_WIDTH), F32)],
        scratch_shapes=[pltpu.VMEM((SSM_WIDTH // 128, ROW_TILE, 128), F32)],
        compiler_params=_params("parallel"),
        name="s5_in_proj",
    )(h, norm.reshape(1, D_MODEL), w_in.astype(BF16))


def _nsa_out_kernel(o_ref, z_ref, w_ref, res_ref, out_ref):
    a = o_ref[...] * _silu(z_ref[...])
    out_ref[...] = res_ref[...] + jnp.dot(a.astype(BF16), w_ref[...], preferred_element_type=F32)


def _nsa_out(o, z, w_out, res):
    t = o.shape[0]
    row = pl.BlockSpec((ROW_TILE, D_MODEL), lambda i: (i, 0))
    return pl.pallas_call(
        _nsa_out_kernel,
        grid=(t // ROW_TILE,),
        in_specs=[row, row, pl.BlockSpec((ATTN_WIDTH, D_MODEL), lambda i: (0, 0)), row],
        out_specs=row,
        out_shape=jax.ShapeDtypeStruct((t, D_MODEL), F32),
        compiler_params=_params("parallel"),
        name="nsa_out_proj",
    )(o, z, w_out.astype(BF16), res)


def _s5_out_kernel(y_ref, z_ref, wg_ref, wo_ref, res_ref, fn_ref, out_ref, y_scr, *, final):
    nch = ROW_TILE // S5_CHUNK
    for m in range(SSM_WIDTH // 128):
        for k in range(S5_CHUNK // 4):
            pairs = [y_ref[4 * m + b, :, k * 128:(k + 1) * 128] for b in range(4)]
            for a, tile in enumerate(_quarter_exchange(pairs)):
                y_scr[m, pl.ds(4 * k + a, nch, stride=S5_CHUNK), :] = tile
    y = jnp.concatenate([y_scr[m] for m in range(SSM_WIDTH // 128)], axis=1)
    gl = jnp.dot(y.astype(BF16), wg_ref[...], preferred_element_type=F32)
    v = gl[:, :SSM_WIDTH] * jax.nn.sigmoid(gl[:, SSM_WIDTH:])
    v = v * _silu(z_ref[...])
    h = res_ref[...] + jnp.dot(v.astype(BF16), wo_ref[...], preferred_element_type=F32)
    out_ref[...] = _rms(h, fn_ref[...]) if final else h


def _s5_out(y, z, w_glu, w_out, res, final_norm, final):
    t = z.shape[0]
    row = pl.BlockSpec((ROW_TILE, D_MODEL), lambda i: (i, 0))
    return pl.pallas_call(
        functools.partial(_s5_out_kernel, final=final),
        grid=(t // ROW_TILE,),
        in_specs=[pl.BlockSpec((S5_PAIRS, ROW_TILE // S5_CHUNK, S5_PAIR), lambda i: (0, i, 0)), row,
                  pl.BlockSpec((SSM_WIDTH, 2 * SSM_WIDTH), lambda i: (0, 0)),
                  pl.BlockSpec((SSM_WIDTH, D_MODEL), lambda i: (0, 0)), row,
                  pl.BlockSpec((1, D_MODEL), lambda i: (0, 0))],
        out_specs=row,
        out_shape=jax.ShapeDtypeStruct((t, D_MODEL), F32),
        scratch_shapes=[pltpu.VMEM((SSM_WIDTH // 128, ROW_TILE, 128), F32)],
        compiler_params=_params("parallel"),
        name="s5_glu_out_proj",
    )(y, z, w_glu.astype(BF16), w_out.astype(BF16), res, final_norm.reshape(1, D_MODEL))


def _cmp_one(a_ref, pe_ref, w1_ref, w2_ref, out_ref):
    nc = a_ref.shape[0]
    a = a_ref[...]
    pe = pe_ref[...]
    h_top = jnp.dot((a + pe[0:1]).astype(BF16), w1_ref[0], preferred_element_type=F32)
    h_bot = jnp.dot((a + pe[1:2]).astype(BF16), w1_ref[1], preferred_element_type=F32)
    hid = _silu(h_top + pltpu.roll(h_bot, nc - 1, 0))
    out = jnp.dot(hid.astype(BF16), w2_ref[...], preferred_element_type=F32)
    keep = lax.broadcasted_iota(jnp.int32, out.shape, 0) < nc - 1
    out_ref[...] = jnp.where(keep, out, 0.0).astype(BF16)


def _cmp_kernel(ka_ref, va_ref, kpe_ref, kw1_ref, kw2_ref, vpe_ref, vw1_ref, vw2_ref, kc_ref, vc_ref):
    _cmp_one(ka_ref, kpe_ref, kw1_ref, kw2_ref, kc_ref)
    _cmp_one(va_ref, vpe_ref, vw1_ref, vw2_ref, vc_ref)


def _compress(ka, va, k_pe, k_w1, k_w2, v_pe, v_w1, v_w2):
    bg, nc, half = ka.shape
    blk = pl.BlockSpec((None, nc, half), lambda i: (i, 0, 0))
    full = lambda shape: pl.BlockSpec(shape, lambda i: (0,) * len(shape))
    prep = lambda pe, w1, w2: (pe.reshape(2, half), w1.reshape(2, half, CMP_HIDDEN).astype(BF16), w2.astype(BF16))
    out = pl.BlockSpec((None, nc, HEAD_DIM), lambda i: (i, 0, 0))
    wspecs = [full((2, half)), full((2, half, CMP_HIDDEN)), full((CMP_HIDDEN, HEAD_DIM))]
    return pl.pallas_call(
        _cmp_kernel,
        grid=(bg,),
        in_specs=[blk, blk] + wspecs + wspecs,
        out_specs=[out, out],
        out_shape=[jax.ShapeDtypeStruct((bg, nc, HEAD_DIM), BF16)] * 2,
        compiler_params=_params("parallel"),
        name="nsa_compress",
    )(ka, va, *prep(k_pe, k_w1, k_w2), *prep(v_pe, v_w1, v_w2))


def _dot_nt(a, b):
    return lax.dot_general(a, b, (((1,), (1,)), ((), ())), preferred_element_type=F32)


def _split3(x):
    hi = x.astype(BF16)
    r1 = x - hi.astype(F32)
    mid = r1.astype(BF16)
    lo = (r1 - mid.astype(F32)).astype(BF16)
    return hi, mid, lo


def _softmax_values(parts, m_row, va):
    p = [jnp.exp2(sc - m_row) for sc in parts]
    acc = jnp.dot(jnp.concatenate(p, axis=1).astype(BF16), va, preferred_element_type=F32)
    return acc, p


def _row_max(parts):
    m = parts[0]
    for sc in parts[1:]:
        m = jnp.maximum(m, sc)
    return jnp.broadcast_to(jnp.max(m, axis=-1, keepdims=True), m.shape)


def _attn_kernel(qa_ref, ksa_ref, vsa_ref, kwa_ref, vwa_ref, kca_ref, vca_ref, gt_ref, ovt_ref,
                 o_ref, m_scr, top_scr, acc_scr, oc_scr, ow_scr, qaug_scr, used_smem, *, seq, tq, tk):
    nb = seq // SEL_BLOCK
    nc = seq // CMP_STRIDE
    rows = HEADS_PER_GROUP * tq
    q0 = pl.program_id(1) * tq

    qa = qa_ref[...].reshape(rows, LANES)
    t_q = q0 + lax.broadcasted_iota(jnp.int32, (tq, LANES), 0)
    lane = lax.broadcasted_iota(jnp.int32, (tq, LANES), 1)

    def per_head_where(ok, x, fill):
        cols = x.shape[1]
        return jnp.where(ok[None], x.reshape(HEADS_PER_GROUP, tq, cols), fill).reshape(rows, cols)

    ntile = tk // LANES
    n_before = q0 // tk

    def tile_scores(kt, causal):
        kb = pl.multiple_of(kt * tk, tk)
        sc = _dot_nt(qaug_scr[...], ksa_ref[pl.ds(kb, tk), :])
        if causal:
            key = kb + lax.broadcasted_iota(jnp.int32, (tq, tk), 1)
            sc = per_head_where(jnp.concatenate([t_q] * ntile, axis=1) >= key, sc, NEG_INF)
        return kb, sc

    def lane_max(sc):
        part = sc[:, :LANES]
        for c in range(1, ntile):
            part = jnp.maximum(part, sc[:, c * LANES:(c + 1) * LANES])
        return part

    def value_tile(kt, causal, track):
        kb, sc = tile_scores(kt, causal)
        if track:
            top_scr[...] = jnp.maximum(top_scr[...], lane_max(sc))
        pr = jnp.exp2(sc - jnp.concatenate([m_scr[...]] * ntile, axis=1))
        acc_scr[...] += jnp.dot(pr.astype(BF16), vsa_ref[pl.ds(kb, tk), :], preferred_element_type=F32)

    def front(nchunk):
        width = nchunk * LANES
        sc_all = _dot_nt(qa, kca_ref[0:width, :])
        last_ok = (t_q - (CMP_BLOCK - 1)) >> 4
        parts = [per_head_where(lane + c * LANES <= last_ok, sc_all[:, c * LANES:(c + 1) * LANES], NEG_INF)
                 for c in range(nchunk)]
        m_c = per_head_where(t_q >= CMP_BLOCK - 1, _row_max(parts), -NEG_INF)
        acc_c, p_c = _softmax_values(parts, m_c, vca_ref[0:width, :])
        inv_c = jnp.broadcast_to(1.0 / jnp.maximum(acc_c[:, HEAD_DIM:HEAD_DIM + 1], 1e-30), (rows, LANES))
        oc_scr[...] = acc_c[:, :HEAD_DIM] * inv_c[:, :HEAD_DIM]
        p = jnp.concatenate([pc * inv_c for pc in p_c], axis=1)

        wlen = WINDOW + tq
        w0 = pl.multiple_of(jnp.maximum(q0 - WINDOW, 0), tq)
        sw_all = _dot_nt(qa, kwa_ref[pl.ds(w0, wlen), :])
        parts = []
        for c in range(wlen // LANES):
            key = w0 + c * LANES + lane
            ok = key <= t_q
            if c * LANES < tq:
                ok = ok & (key > t_q - WINDOW)
            parts.append(per_head_where(ok, sw_all[:, c * LANES:(c + 1) * LANES], NEG_INF))
        acc_w, _ = _softmax_values(parts, _row_max(parts), vwa_ref[pl.ds(w0, wlen), :])
        ow_scr[...] = acc_w[:, :HEAD_DIM] / jnp.maximum(acc_w[:, HEAD_DIM:HEAD_DIM + 1], 1e-30)

        psum = p[0:tq] + p[tq:2 * tq] + p[2 * tq:3 * tq] + p[3 * tq:4 * tq]
        ovt = ovt_ref[:, 0:width]
        imp = sum(_dot_nt(ovt, part) for part in _split3(psum))
        j = lax.broadcasted_iota(jnp.int32, (nb, tq), 0)
        jt = (q0 + lax.broadcasted_iota(jnp.int32, (nb, tq), 1)) // SEL_BLOCK
        forced = (j == 0) | (j == jt) | (j == jt - 1)
        sel_t = jnp.where(forced, 1.0, 0.0)
        vals = jnp.where(forced, -3e38, jnp.where(j > jt, -FORCE_SCORE, imp))
        jf = j.astype(F32)
        for _ in range(min(SEL_TOPK, nb) - 3):
            best = jnp.max(vals, axis=0, keepdims=True)
            first = jnp.min(jnp.where(vals == best, jf, float(nb)), axis=0, keepdims=True)
            hit = jf == first
            sel_t = jnp.where(hit, 1.0, sel_t)
            vals = jnp.where(hit, -3e38, vals)

        not_chosen = ((1.0 - sel_t) * MASK_BIAS).T.astype(BF16)
        qaug_scr[:, :LANES] = qa
        for r in range(HEADS_PER_GROUP):
            qaug_scr[r * tq:(r + 1) * tq, LANES:LANES + nb] = not_chosen
        if nb < LANES:
            qaug_scr[:, LANES + nb:] = jnp.zeros((rows, LANES - nb), BF16)
        blocks_per_tile = tk // SEL_BLOCK
        for i in range(seq // tk):
            chosen_here = jnp.max(sel_t[i * blocks_per_tile:(i + 1) * blocks_per_tile, :])
            used_smem[i] = (chosen_here > 0.5).astype(jnp.int32)

        kb_diag, sc_diag = tile_scores(n_before, True)
        m_diag = jnp.broadcast_to(jnp.max(lane_max(sc_diag), axis=-1, keepdims=True), (rows, LANES))
        m_scr[...] = m_diag
        pr_diag = jnp.exp2(sc_diag - jnp.concatenate([m_diag] * ntile, axis=1))
        acc_scr[...] = jnp.dot(pr_diag.astype(BF16), vsa_ref[pl.ds(kb_diag, tk), :], preferred_element_type=F32)
        top_scr[...] = jnp.full((rows, LANES), NEG_INF, F32)

    all_chunks = nc // LANES
    chunks_needed = (q0 + tq - CMP_BLOCK) // (CMP_STRIDE * LANES) + 1
    covered = 0
    for nchunk in sorted({max(1, all_chunks // 2), all_chunks}):
        pl.when((chunks_needed > covered) & (chunks_needed <= nchunk))(functools.partial(front, nchunk))
        covered = nchunk

    def tiles_before(track):
        def step(kt, carry):
            @pl.when(used_smem[kt] > 0)
            def _():
                value_tile(kt, False, track)
            return carry
        lax.fori_loop(0, n_before, step, 0)

    tiles_before(True)

    gate = jax.nn.sigmoid(gt_ref[...])

    def write_output():
        acc = acc_scr[...]
        o_s = acc[:, :HEAD_DIM] / jnp.maximum(acc[:, HEAD_DIM:HEAD_DIM + 1], 1e-30)
        for r in range(HEADS_PER_GROUP):
            sl = slice(r * tq, (r + 1) * tq)
            gc = lambda x: gate[:, x * HEADS_PER_GROUP + r:x * HEADS_PER_GROUP + r + 1]
            o_ref[:, r * HEAD_DIM:(r + 1) * HEAD_DIM] = (gc(0) * oc_scr[sl, :] + gc(1) * o_s[sl]
                                                         + gc(2) * ow_scr[sl, :])

    overflow_risk = jnp.max(top_scr[...] - m_scr[...]) > EXP_HEADROOM
    write_output()

    @pl.when(overflow_risk)
    def _():
        exact = jnp.maximum(top_scr[...], m_scr[...])
        m_scr[...] = jnp.broadcast_to(jnp.max(exact, axis=-1, keepdims=True), (rows, LANES))
        acc_scr[...] = jnp.zeros((rows, LANES), F32)
        tiles_before(False)
        value_tile(n_before, True, False)
        write_output()


def _attention(qa, ksa, vsa, kwa, vwa, kca, vca, gt, batch, seq):
    tq, tk = ATTN_TQ, ATTN_TK
    bg = batch * N_KV_GROUPS
    nq = seq // tq
    nb, nc = seq // SEL_BLOCK, seq // CMP_STRIDE
    assert SEL_TOPK <= nb <= LANES and seq % tk == 0 and tk % tq == 0 and tq % LANES == 0 and nc % LANES == 0
    rows = HEADS_PER_GROUP * tq
    gq = HEADS_PER_GROUP * HEAD_DIM
    cs = jnp.arange(nc)[None, :] * CMP_STRIDE
    ss = jnp.arange(nb)[:, None] * SEL_BLOCK
    ovt = ((cs <= ss + SEL_BLOCK - 1) & (cs + CMP_BLOCK - 1 >= ss)).astype(BF16)
    per_bg = lambda n, w: pl.BlockSpec((None, n, w), lambda i, j: (i, 0, 0))
    return pl.pallas_call(
        functools.partial(_attn_kernel, seq=seq, tq=tq, tk=tk),
        grid=(bg, nq),
        in_specs=[pl.BlockSpec((None, HEADS_PER_GROUP, tq, LANES), lambda i, j: (i, 0, j, 0)),
                  per_bg(seq, 2 * LANES), per_bg(seq, LANES), per_bg(seq, LANES), per_bg(seq, LANES),
                  per_bg(nc, LANES), per_bg(nc, LANES),
                  pl.BlockSpec((None, tq, 3 * HEADS_PER_GROUP), lambda i, j: (i, j, 0)),
                  pl.BlockSpec((nb, nc), lambda i, j: (0, 0))],
        out_specs=pl.BlockSpec((tq, gq), lambda i, j: ((i // N_KV_GROUPS) * nq + j, i % N_KV_GROUPS)),
        out_shape=jax.ShapeDtypeStruct((batch * seq, ATTN_WIDTH), F32),
        scratch_shapes=[pltpu.VMEM((rows, LANES), F32), pltpu.VMEM((rows, LANES), F32), pltpu.VMEM((rows, LANES), F32),
                        pltpu.VMEM((rows, HEAD_DIM), F32), pltpu.VMEM((rows, HEAD_DIM), F32),
                        pltpu.VMEM((rows, 2 * LANES), BF16), pltpu.SMEM((seq // tk,), jnp.int32)],
        compiler_params=_params("parallel", "arbitrary"),
        name="nsa_attention",
    )(qa, ksa, vsa, kwa, vwa, kca, vca, gt, ovt)


def _alibi_columns(seq):
    dh = HEAD_DIM
    nb, nc = seq // SEL_BLOCK, seq // CMP_STRIDE
    head = jnp.arange(1, N_HEADS + 1, dtype=F32).reshape(N_KV_GROUPS, HEADS_PER_GROUP)
    parts = _split3(jnp.exp2(-8.0 * head / N_HEADS) * LOG2E)
    qcols = jnp.stack([float(SEL_BLOCK) * p.astype(F32) for p in parts] + [p.astype(F32) for p in parts], axis=-1)
    qcols = jnp.pad(qcols, ((0, 0), (0, 0), (0, LANES - dh - qcols.shape[-1]))).astype(BF16)

    def key_cols(u, scale):
        cols = jnp.stack([u // SEL_BLOCK] * 3 + [u % SEL_BLOCK] * 3, axis=-1).astype(F32) * scale
        return jnp.pad(cols, ((0, 0), (0, LANES - dh - cols.shape[-1])))

    pos = jnp.arange(seq)
    onehot = (pos[:, None] // SEL_BLOCK == jnp.arange(LANES)[None, :]) & (jnp.arange(LANES) < nb)
    win_cols = key_cols(pos, 1.0).astype(BF16)
    sel_cols = jnp.concatenate([win_cols, onehot.astype(BF16)], axis=1)
    cmp_cols = key_cols(jnp.arange(nc), float(CMP_STRIDE)).astype(BF16)
    ones_cols = (jnp.arange(LANES - dh) == 0).astype(BF16)
    return qcols, sel_cols, win_cols, cmp_cols, ones_cols


def _nsa_layer(h, batch, seq, norm, w_in, k_pe, k_w1, k_w2, v_pe, v_w1, v_w2, w_out):
    gg = N_KV_GROUPS
    bg = batch * gg
    (qa, ksa, vsa, kwa, vwa, kc_in, vc_in, gt, z), cmp_cols, ones_cols = _nsa_in(h, norm, w_in, batch, seq)
    merge = lambda x: x.reshape((bg,) + x.shape[2:])
    with_cols = lambda x, cols: jnp.concatenate([x, jnp.broadcast_to(cols, x.shape[:-1] + cols.shape[-1:])], axis=-1)
    kc, vc = _compress(merge(kc_in), merge(vc_in), k_pe, k_w1, k_w2, v_pe, v_w1, v_w2)
    o = _attention(merge(qa), merge(ksa), merge(vsa), merge(kwa), merge(vwa),
                   with_cols(kc, cmp_cols), with_cols(vc, ones_cols), merge(gt), batch, seq)
    return _nsa_out(o, z, w_out, h)


def _dot_bf16(a, b):
    return jnp.dot(a.astype(BF16), b.astype(BF16), preferred_element_type=F32)


def _member(shape, axis, width):
    return (lax.broadcasted_iota(jnp.int32, shape, axis) // width) % 2


def _s5_state_kernel(u_ref, w_ref, xr_ref, xi_ref):
    w = w_ref[...]
    p = SSM_STATE
    wide = jnp.concatenate([w[:, :p], w[:, :p], w[:, p:], w[:, p:]], axis=1)
    w_pair = jnp.where(_member(wide.shape, 0, SSM_GROUP) == _member(wide.shape, 1, p), wide, 0.0)
    x = _dot_bf16(u_ref[...], w_pair)
    half = x.shape[1] // 2
    xr_ref[...] = x[:, :half]
    xi_ref[...] = x[:, half:]


def _s5_scan_kernel(xr_ref, xi_ref, ar_ref, ai_ref, cr_ref, ci_ref, *, batch, nchunk):
    ar, ai = ar_ref[...], ai_ref[...]
    width = ar.shape[1]

    def step(n, carry):
        new = []
        for b in range(batch):
            cr, ci = carry[2 * b], carry[2 * b + 1]
            idx = b * nchunk + n
            cr_ref[pl.ds(idx, 1), :] = cr
            ci_ref[pl.ds(idx, 1), :] = ci
            lr, li = xr_ref[pl.ds(idx, 1), :], xi_ref[pl.ds(idx, 1), :]
            new += [ar * cr - ai * ci + lr, ar * ci + ai * cr + li]
        return tuple(new)

    zero = jnp.zeros((1, width), F32)
    lax.fori_loop(0, nchunk, step, (zero,) * (2 * batch))


def _s5_y_kernel(u_ref, cr_ref, ci_ref, k_ref, v_ref, d_ref, y_ref, m_scr):
    k = k_ref[...]
    col = lax.broadcasted_iota(jnp.int32, k.shape, 1)
    m_scr[0:S5_PAIR_LANES, :] = k
    for s in range(1, S5_CHUNK):
        shifted = pltpu.roll(k, s * S5_PAIR_LANES, 1)
        m_scr[s * S5_PAIR_LANES:(s + 1) * S5_PAIR_LANES, :] = jnp.where(col >= s * S5_PAIR_LANES, shifted, 0.0)
    v = v_ref[...]
    p = SSM_STATE
    tall = jnp.concatenate([v[:p], v[:p], v[p:], v[p:]], axis=0)
    v_pair = jnp.where(_member(tall.shape, 0, p) == _member(tall.shape, 1, SSM_GROUP), tall, 0.0)
    u = u_ref[...]
    carry = jnp.concatenate([cr_ref[...], ci_ref[...]], axis=1)
    y = _dot_bf16(u, m_scr[...]) + _dot_bf16(carry, v_pair) + d_ref[...] * u
    y_ref[...] = jax.nn.gelu(y)


def _s5_matrices(log_dt, lambda_re, lambda_im, b_re, b_im, c_re, c_im, d_skip):
    hp = lax.Precision.HIGHEST
    gn, pn, cn, ln = SSM_GROUPS, SSM_STATE, SSM_GROUP, S5_CHUNK
    dt = jnp.exp(log_dt.astype(F32))[:, None]
    lre = jnp.minimum(lambda_re.astype(F32), -1e-4)
    lim = lambda_im.astype(F32)
    mag = jnp.exp(lre * dt)
    ab_re, ab_im = mag * jnp.cos(lim * dt), mag * jnp.sin(lim * dt)
    den = lre * lre + lim * lim
    nr = ab_re - 1.0
    coef_re = (nr * lre + ab_im * lim) / den
    coef_im = (ab_im * lre - nr * lim) / den
    br32, bi32 = b_re.astype(F32), b_im.astype(F32)
    bb_re = coef_re[..., None] * br32 - coef_im[..., None] * bi32
    bb_im = coef_re[..., None] * bi32 + coef_im[..., None] * br32
    cr32, ci32 = c_re.astype(F32), c_im.astype(F32)
    k_steps = jnp.arange(ln + 1, dtype=F32)[:, None, None]
    pw_mag = jnp.exp(k_steps * (lre * dt))
    pw_re, pw_im = pw_mag * jnp.cos(k_steps * (lim * dt)), pw_mag * jnp.sin(k_steps * (lim * dt))
    ab_b_re = pw_re[:ln, :, :, None] * bb_re - pw_im[:ln, :, :, None] * bb_im
    ab_b_im = pw_re[:ln, :, :, None] * bb_im + pw_im[:ln, :, :, None] * bb_re
    kern = (jnp.einsum('gop,kgpi->kgoi', cr32, ab_b_re, precision=hp)
            - jnp.einsum('gop,kgpi->kgoi', ci32, ab_b_im, precision=hp))
    gp = gn // 2
    same = jnp.eye(2, dtype=F32)
    k_pair = kern.reshape(ln, gp, 2, cn, cn).transpose(1, 2, 4, 0, 3)
    k_pair = k_pair[:, :, :, :, None, :] * same[None, :, None, None, :, None]
    k_pair = k_pair.reshape(gp, S5_PAIR_LANES, S5_PAIR)
    w = jnp.stack([ab_b_re[::-1], ab_b_im[::-1]]).reshape(2, ln, gp, 2, pn, cn)
    w_pair = w.transpose(2, 1, 3, 5, 0, 4).reshape(gp, S5_PAIR, 2 * pn)
    ar1, ai1 = pw_re[1:], pw_im[1:]
    v_re = cr32[None] * ar1[:, :, None, :] - ci32[None] * ai1[:, :, None, :]
    v_im = -(cr32[None] * ai1[:, :, None, :] + ci32[None] * ar1[:, :, None, :])
    v = jnp.stack([v_re, v_im]).reshape(2, ln, gp, 2, cn, pn)
    v_pair = v.transpose(2, 0, 5, 1, 3, 4).reshape(gp, 2 * pn, S5_PAIR)
    d_pair = jnp.tile(d_skip.astype(F32).reshape(gp, 1, 2 * cn), (1, 1, ln))
    a_end_re = pw_re[ln].reshape(1, gn * pn)
    a_end_im = pw_im[ln].reshape(1, gn * pn)
    return k_pair, w_pair, v_pair, d_pair, a_end_re, a_end_im


def _s5_layer(h, batch, seq, norm, w_in, log_dt, lambda_re, lambda_im, b_re, b_im, c_re, c_im, d_skip,
              w_glu, w_out, final_norm, final):
    gp, ln, cn = SSM_GROUPS // 2, S5_CHUNK, SSM_GROUP
    nchunk = seq // ln
    rows = batch * nchunk
    state_w = 2 * SSM_STATE
    k_pair, w_pair, v_pair, d_pair, a_re, a_im = _s5_matrices(
        log_dt, lambda_re, lambda_im, b_re, b_im, c_re, c_im, d_skip)
    up, z = _s5_in(h, norm, w_in)
    pair3 = lambda a, b: pl.BlockSpec((None, a, b), lambda i: (i, 0, 0))
    slab = pl.BlockSpec((rows, state_w), lambda i: (0, i))
    xr, xi = pl.pallas_call(
        _s5_state_kernel,
        grid=(gp,),
        in_specs=[pair3(rows, S5_PAIR), pair3(S5_PAIR, state_w)],
        out_specs=[slab, slab],
        out_shape=[jax.ShapeDtypeStruct((rows, gp * state_w), F32)] * 2,
        compiler_params=_params("parallel"),
        name="s5_chunk_state",
    )(up, w_pair)
    scan_w = 4 * state_w
    wide = pl.BlockSpec((rows, scan_w), lambda i: (0, i))
    coef = pl.BlockSpec((1, scan_w), lambda i: (0, i))
    cr, ci = pl.pallas_call(
        functools.partial(_s5_scan_kernel, batch=batch, nchunk=nchunk),
        grid=(gp * state_w // scan_w,),
        in_specs=[wide, wide, coef, coef],
        out_specs=[wide, wide],
        out_shape=[jax.ShapeDtypeStruct((rows, gp * state_w), F32)] * 2,
        compiler_params=_params("parallel"),
        name="s5_carry_scan",
    )(xr, xi, a_re, a_im)
    yp = pl.pallas_call(
        _s5_y_kernel,
        grid=(gp,),
        in_specs=[pair3(rows, S5_PAIR), slab, slab, pair3(S5_PAIR_LANES, S5_PAIR), pair3(state_w, S5_PAIR),
                  pair3(1, S5_PAIR)],
        out_specs=pair3(rows, S5_PAIR),
        out_shape=jax.ShapeDtypeStruct((gp, rows, S5_PAIR), F32),
        scratch_shapes=[pltpu.VMEM((S5_PAIR, S5_PAIR), F32)],
        compiler_params=_params("parallel"),
        name="s5_chunk_output",
    )(up, cr, ci, k_pair, v_pair, d_pair)
    return _s5_out(yp, z, w_glu, w_out, h, final_norm, final)


def kernel(x, l0_norm, l0_w_in, l0_cmp_k_pe, l0_cmp_k_w1, l0_cmp_k_w2, l0_cmp_v_pe, l0_cmp_v_w1, l0_cmp_v_w2, l0_w_out, l1_norm, l1_w_in, l1_log_dt, l1_lambda_re, l1_lambda_im, l1_b_re, l1_b_im, l1_c_re, l1_c_im, l1_d, l1_w_glu, l1_w_out, l2_norm, l2_w_in, l2_cmp_k_pe, l2_cmp_k_w1, l2_cmp_k_w2, l2_cmp_v_pe, l2_cmp_v_w1, l2_cmp_v_w2, l2_w_out, l3_norm, l3_w_in, l3_log_dt, l3_lambda_re, l3_lambda_im, l3_b_re, l3_b_im, l3_c_re, l3_c_im, l3_d, l3_w_glu, l3_w_out, final_norm):
    batch, seq, _ = x.shape
    h = x.reshape(batch * seq, D_MODEL)
    h = _nsa_layer(h, batch, seq, l0_norm, l0_w_in, l0_cmp_k_pe, l0_cmp_k_w1, l0_cmp_k_w2,
                   l0_cmp_v_pe, l0_cmp_v_w1, l0_cmp_v_w2, l0_w_out)
    h = _s5_layer(h, batch, seq, l1_norm, l1_w_in, l1_log_dt, l1_lambda_re, l1_lambda_im, l1_b_re, l1_b_im,
                  l1_c_re, l1_c_im, l1_d, l1_w_glu, l1_w_out, final_norm, False)
    h = _nsa_layer(h, batch, seq, l2_norm, l2_w_in, l2_cmp_k_pe, l2_cmp_k_w1, l2_cmp_k_w2,
                   l2_cmp_v_pe, l2_cmp_v_w1, l2_cmp_v_w2, l2_w_out)
    h = _s5_layer(h, batch, seq, l3_norm, l3_w_in, l3_log_dt, l3_lambda_re, l3_lambda_im, l3_b_re, l3_b_im,
                  l3_c_re, l3_c_im, l3_d, l3_w_glu, l3_w_out, final_norm, True)
    return h.reshape(batch, seq, D_MODEL)
```

```python
import functools

import jax
import jax.numpy as jnp
from jax import lax
from jax.experimental import pallas as pl
from jax.experimental.pallas import tpu as pltpu

F32 = jnp.float32
BF16 = jnp.bfloat16

D_MODEL = 1024
EPS = 1e-6
NEG_INF = -1e30
FORCE_SCORE = 1e9

N_HEADS = 16
HEAD_DIM = 64
N_KV_GROUPS = 4
HEADS_PER_GROUP = N_HEADS // N_KV_GROUPS
ATTN_WIDTH = N_HEADS * HEAD_DIM
KV_WIDTH = N_KV_GROUPS * HEAD_DIM
CMP_BLOCK = 32
CMP_STRIDE = 16
CMP_HIDDEN = 256
SEL_BLOCK = 64
SEL_TOPK = 16
WINDOW = 512
N_GATES = 3 * N_HEADS
GATES_PAD = 128

SSM_WIDTH = D_MODEL
SSM_GROUP = 16
SSM_GROUPS = SSM_WIDTH // SSM_GROUP
SSM_STATE = 64
S5_CHUNK = 16
S5_PAIRS = SSM_GROUPS // 2
S5_PAIR_LANES = 2 * SSM_GROUP
S5_PAIR = S5_CHUNK * S5_PAIR_LANES

ROW_TILE = 512
ATTN_TQ = 256
ATTN_TK = 512
LANES = 128
MASK_BIAS = -2.0 ** 100
LOG2E = 1.4426950408889634
EXP_HEADROOM = 80.0
VMEM_LIMIT = 56 * 1024 * 1024


def _params(*sem):
    return pltpu.CompilerParams(dimension_semantics=sem, vmem_limit_bytes=VMEM_LIMIT)


def _rms(x, g):
    return x * lax.rsqrt(jnp.mean(x * x, axis=-1, keepdims=True) + EPS) * g


def _silu(x):
    return x * jax.nn.sigmoid(x)


def _nsa_in_kernel(h_ref, g_ref, w_ref, qcols_ref, selc_ref, winc_ref, ones_ref,
                   qa_ref, ksa_ref, vsa_ref, kwa_ref, vwa_ref, kcin_ref, vcin_ref, gt_ref, z_ref, c_scr):
    gg, rr, dh = N_KV_GROUPS, HEADS_PER_GROUP, HEAD_DIM
    xn = _rms(h_ref[...], g_ref[...])
    y = jnp.dot(xn.astype(BF16), w_ref[...], preferred_element_type=F32)
    for g in range(gg):
        for r in range(rr):
            k = g * rr + r
            qa_ref[g, r] = jnp.broadcast_to(qcols_ref[g, r:r + 1, :], (ROW_TILE, LANES))
            qa_ref[g, r, :, :dh] = (y[:, k * dh:(k + 1) * dh] * (dh ** -0.5 * LOG2E)).astype(BF16)
    o = ATTN_WIDTH + 2 * KV_WIDTH
    consts = (selc_ref[...], jnp.broadcast_to(ones_ref[...], (ROW_TILE, LANES)), winc_ref[...],
              jnp.broadcast_to(ones_ref[...], (ROW_TILE, LANES)))
    for kind, ref in enumerate((ksa_ref, vsa_ref, kwa_ref, vwa_ref)):
        for g in range(gg):
            c = o + (kind * gg + g) * dh
            ref[g] = consts[kind]
            ref[g, :, :dh] = y[:, c:c + dh].astype(BF16)
    lane = lax.broadcasted_iota(jnp.int32, (ROW_TILE // CMP_STRIDE, LANES), 1)
    for n in range(2 * KV_WIDTH // LANES):
        c_scr[n] = y[:, ATTN_WIDTH + n * LANES:ATTN_WIDTH + (n + 1) * LANES]
        kind, g0 = divmod(2 * n, gg)
        for a in range(CMP_STRIDE // 2):
            even = c_scr[n, pl.ds(2 * a, ROW_TILE // CMP_STRIDE, stride=CMP_STRIDE), :]
            odd = c_scr[n, pl.ds(2 * a + 1, ROW_TILE // CMP_STRIDE, stride=CMP_STRIDE), :]
            dst = (kcin_ref, vcin_ref)[kind]
            dst[g0, :, a * LANES:(a + 1) * LANES] = jnp.where(lane < dh, even, pltpu.roll(odd, dh, 1))
            dst[g0 + 1, :, a * LANES:(a + 1) * LANES] = jnp.where(lane < dh, pltpu.roll(even, dh, 1), odd)
    o += 4 * KV_WIDTH
    z_ref[...] = y[:, o:o + ATTN_WIDTH]
    o += ATTN_WIDTH
    per_group = 3 * rr
    for g in range(gg):
        gt_ref[g] = y[:, o + g * per_group:o + (g + 1) * per_group]


def _nsa_in(h, norm, w_in, batch, seq):
    gg, rr, dh = N_KV_GROUPS, HEADS_PER_GROUP, HEAD_DIM
    nt = seq // ROW_TILE
    nc_tile = ROW_TILE // CMP_STRIDE
    a, b = ATTN_WIDTH + 6 * KV_WIDTH, ATTN_WIDTH + 6 * KV_WIDTH + N_GATES
    w_gates = w_in[:, a:b].reshape(D_MODEL, 3, gg, rr).transpose(0, 2, 1, 3).reshape(D_MODEL, N_GATES)
    w = jnp.concatenate([w_in[:, :a], w_in[:, b:], w_gates,
                         jnp.zeros((D_MODEL, GATES_PAD - N_GATES), w_in.dtype)], axis=1).astype(BF16)
    n = w.shape[1]
    qcols, sel_cols, win_cols, cmp_cols, ones_cols = _alibi_columns(seq)
    lead = lambda cols: jnp.pad(cols, [(0, 0)] * (cols.ndim - 1) + [(dh, 0)])
    const = lambda shape: pl.BlockSpec(shape, lambda i: (0,) * len(shape))
    by_pos = lambda width: pl.BlockSpec((ROW_TILE, width), lambda i: (i % nt, 0))
    per_b = lambda *tail: pl.BlockSpec((None, gg) + tail, lambda i: (i // nt, 0) + (0,) * (len(tail) - 2) + (i % nt, 0))
    outs = pl.pallas_call(
        _nsa_in_kernel,
        grid=(batch * nt,),
        in_specs=[pl.BlockSpec((ROW_TILE, D_MODEL), lambda i: (i, 0)), const((1, D_MODEL)), const((D_MODEL, n)),
                  const((gg, rr, LANES)), by_pos(2 * LANES), by_pos(LANES), const((1, LANES))],
        out_specs=[per_b(rr, ROW_TILE, LANES), per_b(ROW_TILE, 2 * LANES), per_b(ROW_TILE, LANES),
                   per_b(ROW_TILE, LANES), per_b(ROW_TILE, LANES),
                   per_b(nc_tile, CMP_STRIDE * dh), per_b(nc_tile, CMP_STRIDE * dh),
                   per_b(ROW_TILE, 3 * rr), pl.BlockSpec((ROW_TILE, ATTN_WIDTH), lambda i: (i, 0))],
        out_shape=[jax.ShapeDtypeStruct((batch, gg, rr, seq, LANES), BF16),
                   jax.ShapeDtypeStruct((batch, gg, seq, 2 * LANES), BF16),
                   jax.ShapeDtypeStruct((batch, gg, seq, LANES), BF16),
                   jax.ShapeDtypeStruct((batch, gg, seq, LANES), BF16),
                   jax.ShapeDtypeStruct((batch, gg, seq, LANES), BF16),
                   jax.ShapeDtypeStruct((batch, gg, seq // CMP_STRIDE, CMP_STRIDE * dh), F32),
                   jax.ShapeDtypeStruct((batch, gg, seq // CMP_STRIDE, CMP_STRIDE * dh), F32),
                   jax.ShapeDtypeStruct((batch, gg, seq, 3 * rr), F32),
                   jax.ShapeDtypeStruct((batch * seq, ATTN_WIDTH), F32)],
        scratch_shapes=[pltpu.VMEM((2 * KV_WIDTH // LANES, ROW_TILE, LANES), F32)],
        compiler_params=_params("parallel"),
        name="nsa_in_proj",
    )(h, norm.reshape(1, D_MODEL), w, lead(qcols), lead(sel_cols), lead(win_cols), lead(ones_cols)[None])
    return outs, cmp_cols, ones_cols


def _quarter_exchange(parts):
    quarter = lax.broadcasted_iota(jnp.int32, parts[0].shape, 1) // S5_PAIR_LANES
    outs = []
    for b in range(4):
        acc = None
        for a in range(4):
            moved = parts[a] if a == b else pltpu.roll(parts[a], ((a - b) % 4) * S5_PAIR_LANES, 1)
            acc = moved if acc is None else jnp.where(quarter == a, moved, acc)
        outs.append(acc)
    return outs


def _s5_in_kernel(h_ref, g_ref, w_ref, u_ref, z_ref, u_scr):
    xn = _rms(h_ref[...], g_ref[...])
    y = jnp.dot(xn.astype(BF16), w_ref[...], preferred_element_type=F32)
    z_ref[...] = y[:, SSM_WIDTH:]
    nch = ROW_TILE // S5_CHUNK
    for m in range(SSM_WIDTH // 128):
        u_scr[m] = y[:, m * 128:(m + 1) * 128]
        for k in range(S5_CHUNK // 4):
            steps = [u_scr[m, pl.ds(4 * k + a, nch, stride=S5_CHUNK), :] for a in range(4)]
            for b, tile in enumerate(_quarter_exchange(steps)):
                u_ref[4 * m + b, :, k * 128:(k + 1) * 128] = tile


def _s5_in(h, norm, w_in):
    t = h.shape[0]
    nch = ROW_TILE // S5_CHUNK
    row = lambda width: pl.BlockSpec((ROW_TILE, width), lambda i: (i, 0))
    return pl.pallas_call(
        _s5_in_kernel,
        grid=(t // ROW_TILE,),
        in_specs=[row(D_MODEL), pl.BlockSpec((1, D_MODEL), lambda i: (0, 0)),
                  pl.BlockSpec((D_MODEL, 2 * SSM_WIDTH), lambda i: (0, 0))],
        out_specs=[pl.BlockSpec((S5_PAIRS, nch, S5_PAIR), lambda i: (0, i, 0)), row(SSM_WIDTH)],
        out_shape=[jax.ShapeDtypeStruct((S5_PAIRS, t // S5_CHUNK, S5_PAIR), F32),
                   jax.ShapeDtypeStruct((t, SSM_WIDTH), F32)],
        scratch_shapes=[pltpu.VMEM((SSM_WIDTH // 128, ROW_TILE, 128), F32)],
        compiler_params=_params("parallel"),
        name="s5_in_proj",
    )(h, norm.reshape(1, D_MODEL), w_in.astype(BF16))


def _nsa_out_kernel(o_ref, z_ref, w_ref, res_ref, out_ref):
    a = o_ref[...] * _silu(z_ref[...])
    out_ref[...] = res_ref[...] + jnp.dot(a.astype(BF16), w_ref[...], preferred_element_type=F32)


def _nsa_out(o, z, w_out, res):
    t = o.shape[0]
    row = pl.BlockSpec((ROW_TILE, D_MODEL), lambda i: (i, 0))
    return pl.pallas_call(
        _nsa_out_kernel,
        grid=(t // ROW_TILE,),
        in_specs=[row, row, pl.BlockSpec((ATTN_WIDTH, D_MODEL), lambda i: (0, 0)), row],
        out_specs=row,
        out_shape=jax.ShapeDtypeStruct((t, D_MODEL), F32),
        compiler_params=_params("parallel"),
        name="nsa_out_proj",
    )(o, z, w_out.astype(BF16), res)


def _s5_out_kernel(y_ref, z_ref, wg_ref, wo_ref, res_ref, fn_ref, out_ref, y_scr, *, final):
    nch = ROW_TILE // S5_CHUNK
    for m in range(SSM_WIDTH // 128):
        for k in range(S5_CHUNK // 4):
            pairs = [y_ref[4 * m + b, :, k * 128:(k + 1) * 128] for b in range(4)]
            for a, tile in enumerate(_quarter_exchange(pairs)):
                y_scr[m, pl.ds(4 * k + a, nch, stride=S5_CHUNK), :] = tile
    y = jnp.concatenate([y_scr[m] for m in range(SSM_WIDTH // 128)], axis=1)
    gl = jnp.dot(y.astype(BF16), wg_ref[...], preferred_element_type=F32)
    v = gl[:, :SSM_WIDTH] * jax.nn.sigmoid(gl[:, SSM_WIDTH:])
    v = v * _silu(z_ref[...])
    h = res_ref[...] + jnp.dot(v.astype(BF16), wo_ref[...], preferred_element_type=F32)
    out_ref[...] = _rms(h, fn_ref[...]) if final else h


def _s5_out(y, z, w_glu, w_out, res, final_norm, final):
    t = z.shape[0]
    row = pl.BlockSpec((ROW_TILE, D_MODEL), lambda i: (i, 0))
    return pl.pallas_call(
        functools.partial(_s5_out_kernel, final=final),
        grid=(t // ROW_TILE,),
        in_specs=[pl.BlockSpec((S5_PAIRS, ROW_TILE // S5_CHUNK, S5_PAIR), lambda i: (0, i, 0)), row,
                  pl.BlockSpec((SSM_WIDTH, 2 * SSM_WIDTH), lambda i: (0, 0)),
                  pl.BlockSpec((SSM_WIDTH, D_MODEL), lambda i: (0, 0)), row,
                  pl.BlockSpec((1, D_MODEL), lambda i: (0, 0))],
        out_specs=row,
        out_shape=jax.ShapeDtypeStruct((t, D_MODEL), F32),
        scratch_shapes=[pltpu.VMEM((SSM_WIDTH // 128, ROW_TILE, 128), F32)],
        compiler_params=_params("parallel"),
        name="s5_glu_out_proj",
    )(y, z, w_glu.astype(BF16), w_out.astype(BF16), res, final_norm.reshape(1, D_MODEL))


def _cmp_one(a_ref, pe_ref, w1_ref, w2_ref, out_ref):
    nc = a_ref.shape[0]
    a = a_ref[...]
    pe = pe_ref[...]
    h_top = jnp.dot((a + pe[0:1]).astype(BF16), w1_ref[0], preferred_element_type=F32)
    h_bot = jnp.dot((a + pe[1:2]).astype(BF16), w1_ref[1], preferred_element_type=F32)
    hid = _silu(h_top + pltpu.roll(h_bot, nc - 1, 0))
    out = jnp.dot(hid.astype(BF16), w2_ref[...], preferred_element_type=F32)
    keep = lax.broadcasted_iota(jnp.int32, out.shape, 0) < nc - 1
    out_ref[...] = jnp.where(keep, out, 0.0).astype(BF16)


def _cmp_kernel(ka_ref, va_ref, kpe_ref, kw1_ref, kw2_ref, vpe_ref, vw1_ref, vw2_ref, kc_ref, vc_ref):
    _cmp_one(ka_ref, kpe_ref, kw1_ref, kw2_ref, kc_ref)
    _cmp_one(va_ref, vpe_ref, vw1_ref, vw2_ref, vc_ref)


def _compress(ka, va, k_pe, k_w1, k_w2, v_pe, v_w1, v_w2):
    bg, nc, half = ka.shape
    blk = pl.BlockSpec((None, nc, half), lambda i: (i, 0, 0))
    full = lambda shape: pl.BlockSpec(shape, lambda i: (0,) * len(shape))
    prep = lambda pe, w1, w2: (pe.reshape(2, half), w1.reshape(2, half, CMP_HIDDEN).astype(BF16), w2.astype(BF16))
    out = pl.BlockSpec((None, nc, HEAD_DIM), lambda i: (i, 0, 0))
    wspecs = [full((2, half)), full((2, half, CMP_HIDDEN)), full((CMP_HIDDEN, HEAD_DIM))]
    return pl.pallas_call(
        _cmp_kernel,
        grid=(bg,),
        in_specs=[blk, blk] + wspecs + wspecs,
        out_specs=[out, out],
        out_shape=[jax.ShapeDtypeStruct((bg, nc, HEAD_DIM), BF16)] * 2,
        compiler_params=_params("parallel"),
        name="nsa_compress",
    )(ka, va, *prep(k_pe, k_w1, k_w2), *prep(v_pe, v_w1, v_w2))


def _dot_nt(a, b):
    return lax.dot_general(a, b, (((1,), (1,)), ((), ())), preferred_element_type=F32)


def _split3(x):
    hi = x.astype(BF16)
    r1 = x - hi.astype(F32)
    mid = r1.astype(BF16)
    lo = (r1 - mid.astype(F32)).astype(BF16)
    return hi, mid, lo


def _softmax_values(parts, m_row, va):
    p = [jnp.exp2(sc - m_row) for sc in parts]
    acc = jnp.dot(jnp.concatenate(p, axis=1).astype(BF16), va, preferred_element_type=F32)
    return acc, p


def _row_max(parts):
    m = parts[0]
    for sc in parts[1:]:
        m = jnp.maximum(m, sc)
    return jnp.broadcast_to(jnp.max(m, axis=-1, keepdims=True), m.shape)


def _attn_kernel(qa_ref, ksa_ref, vsa_ref, kwa_ref, vwa_ref, kca_ref, vca_ref, gt_ref, ovt_ref,
                 o_ref, m_scr, top_scr, acc_scr, oc_scr, ow_scr, qaug_scr, used_smem, *, seq, tq, tk):
    nb = seq // SEL_BLOCK
    nc = seq // CMP_STRIDE
    rows = HEADS_PER_GROUP * tq
    q0 = pl.program_id(1) * tq

    qa = qa_ref[...].reshape(rows, LANES)
    t_q = q0 + lax.broadcasted_iota(jnp.int32, (tq, LANES), 0)
    lane = lax.broadcasted_iota(jnp.int32, (tq, LANES), 1)

    def per_head_where(ok, x, fill):
        cols = x.shape[1]
        return jnp.where(ok[None], x.reshape(HEADS_PER_GROUP, tq, cols), fill).reshape(rows, cols)

    ntile = tk // LANES
    n_before = q0 // tk

    def tile_scores(kt, causal):
        kb = pl.multiple_of(kt * tk, tk)
        sc = _dot_nt(qaug_scr[...], ksa_ref[pl.ds(kb, tk), :])
        if causal:
            key = kb + lax.broadcasted_iota(jnp.int32, (tq, tk), 1)
            sc = per_head_where(jnp.concatenate([t_q] * ntile, axis=1) >= key, sc, NEG_INF)
        return kb, sc

    def lane_max(sc):
        part = sc[:, :LANES]
        for c in range(1, ntile):
            part = jnp.maximum(part, sc[:, c * LANES:(c + 1) * LANES])
        return part

    def value_tile(kt, causal, track):
        kb, sc = tile_scores(kt, causal)
        if track:
            top_scr[...] = jnp.maximum(top_scr[...], lane_max(sc))
        pr = jnp.exp2(sc - jnp.concatenate([m_scr[...]] * ntile, axis=1))
        acc_scr[...] += jnp.dot(pr.astype(BF16), vsa_ref[pl.ds(kb, tk), :], preferred_element_type=F32)

    def front(nchunk, first_tile):
        width = nchunk * LANES
        sc_all = _dot_nt(qa, kca_ref[0:width, :])
        last_ok = (t_q - (CMP_BLOCK - 1)) >> 4
        parts = [per_head_where(lane + c * LANES <= last_ok, sc_all[:, c * LANES:(c + 1) * LANES], NEG_INF)
                 for c in range(nchunk)]
        m_c = per_head_where(t_q >= CMP_BLOCK - 1, _row_max(parts), -NEG_INF)
        acc_c, p_c = _softmax_values(parts, m_c, vca_ref[0:width, :])
        inv_c = jnp.broadcast_to(1.0 / jnp.maximum(acc_c[:, HEAD_DIM:HEAD_DIM + 1], 1e-30), (rows, LANES))
        oc_scr[...] = acc_c[:, :HEAD_DIM] * inv_c[:, :HEAD_DIM]
        p = jnp.concatenate([pc * inv_c for pc in p_c], axis=1)

        wlen = WINDOW + tq
        w0 = pl.multiple_of(jnp.maximum(q0 - WINDOW, 0), tq)
        sw_all = _dot_nt(qa, kwa_ref[pl.ds(w0, wlen), :])
        parts = []
        for c in range(wlen // LANES):
            key = w0 + c * LANES + lane
            ok = key <= t_q
            if c * LANES < tq:
                ok = ok & (key > t_q - WINDOW)
            parts.append(per_head_where(ok, sw_all[:, c * LANES:(c + 1) * LANES], NEG_INF))
        acc_w, _ = _softmax_values(parts, _row_max(parts), vwa_ref[pl.ds(w0, wlen), :])
        ow_scr[...] = acc_w[:, :HEAD_DIM] / jnp.maximum(acc_w[:, HEAD_DIM:HEAD_DIM + 1], 1e-30)

        psum = p[0:tq] + p[tq:2 * tq] + p[2 * tq:3 * tq] + p[3 * tq:4 * tq]
        ovt = ovt_ref[:, 0:width]
        imp = sum(_dot_nt(ovt, part) for part in _split3(psum))
        j = lax.broadcasted_iota(jnp.int32, (nb, tq), 0)
        jt = (q0 + lax.broadcasted_iota(jnp.int32, (nb, tq), 1)) // SEL_BLOCK
        forced = (j == 0) | (j == jt) | (j == jt - 1)
        sel_t = jnp.where(forced, 1.0, 0.0)
        vals = jnp.where(forced, -3e38, jnp.where(j > jt, -FORCE_SCORE, imp))
        jf = j.astype(F32)
        for _ in range(min(SEL_TOPK, nb) - 3):
            best = jnp.max(vals, axis=0, keepdims=True)
            first = jnp.min(jnp.where(vals == best, jf, float(nb)), axis=0, keepdims=True)
            hit = jf == first
            sel_t = jnp.where(hit, 1.0, sel_t)
            vals = jnp.where(hit, -3e38, vals)

        not_chosen = ((1.0 - sel_t) * MASK_BIAS).T.astype(BF16)
        qaug_scr[:, :LANES] = qa
        for r in range(HEADS_PER_GROUP):
            qaug_scr[r * tq:(r + 1) * tq, LANES:LANES + nb] = not_chosen
        if nb < LANES:
            qaug_scr[:, LANES + nb:] = jnp.zeros((rows, LANES - nb), BF16)
        blocks_per_tile = tk // SEL_BLOCK
        for i in range(seq // tk):
            chosen_here = jnp.max(sel_t[i * blocks_per_tile:(i + 1) * blocks_per_tile, :])
            used_smem[i] = (chosen_here > 0.5).astype(jnp.int32)

        kb_diag, sc_diag = tile_scores(n_before, True)
        m_diag = jnp.broadcast_to(jnp.max(lane_max(sc_diag), axis=-1, keepdims=True), (rows, LANES))
        m_scr[...] = m_diag
        pr_diag = jnp.exp2(sc_diag - jnp.concatenate([m_diag] * ntile, axis=1))
        acc_scr[...] = jnp.dot(pr_diag.astype(BF16), vsa_ref[pl.ds(kb_diag, tk), :], preferred_element_type=F32)
        top_scr[...] = jnp.full((rows, LANES), NEG_INF, F32)
        if first_tile:
            value_tile(0, False, True)

    all_chunks = nc // LANES
    half_chunks = max(1, all_chunks // 2)
    chunks_needed = (q0 + tq - CMP_BLOCK) // (CMP_STRIDE * LANES) + 1
    pl.when((chunks_needed <= half_chunks) & (q0 < tk))(functools.partial(front, half_chunks, False))
    pl.when((chunks_needed <= half_chunks) & (q0 >= tk))(functools.partial(front, half_chunks, True))
    if all_chunks > half_chunks:
        assert half_chunks * CMP_STRIDE * LANES >= tk + tq
        pl.when(chunks_needed > half_chunks)(functools.partial(front, all_chunks, True))

    def tiles_before(track, first):
        def step(kt, carry):
            @pl.when(used_smem[kt] > 0)
            def _():
                value_tile(kt, False, track)
            return carry
        lax.fori_loop(first, n_before, step, 0)

    tiles_before(True, jnp.minimum(n_before, 1))

    gate = jax.nn.sigmoid(gt_ref[...])

    def write_output():
        acc = acc_scr[...]
        o_s = acc[:, :HEAD_DIM] / jnp.maximum(acc[:, HEAD_DIM:HEAD_DIM + 1], 1e-30)
        for r in range(HEADS_PER_GROUP):
            sl = slice(r * tq, (r + 1) * tq)
            gc = lambda x: gate[:, x * HEADS_PER_GROUP + r:x * HEADS_PER_GROUP + r + 1]
            o_ref[:, r * HEAD_DIM:(r + 1) * HEAD_DIM] = (gc(0) * oc_scr[sl, :] + gc(1) * o_s[sl]
                                                         + gc(2) * ow_scr[sl, :])

    overflow_risk = jnp.max(top_scr[...] - m_scr[...]) > EXP_HEADROOM
    write_output()

    @pl.when(overflow_risk)
    def _():
        exact = jnp.maximum(top_scr[...], m_scr[...])
        m_scr[...] = jnp.broadcast_to(jnp.max(exact, axis=-1, keepdims=True), (rows, LANES))
        acc_scr[...] = jnp.zeros((rows, LANES), F32)
        tiles_before(False, 0)
        value_tile(n_before, True, False)
        write_output()


def _attention(qa, ksa, vsa, kwa, vwa, kca, vca, gt, batch, seq):
    tq, tk = ATTN_TQ, ATTN_TK
    bg = batch * N_KV_GROUPS
    nq = seq // tq
    nb, nc = seq // SEL_BLOCK, seq // CMP_STRIDE
    assert SEL_TOPK <= nb <= LANES and seq % tk == 0 and tk % tq == 0 and tq % LANES == 0 and nc % LANES == 0
    rows = HEADS_PER_GROUP * tq
    gq = HEADS_PER_GROUP * HEAD_DIM
    cs = jnp.arange(nc)[None, :] * CMP_STRIDE
    ss = jnp.arange(nb)[:, None] * SEL_BLOCK
    ovt = ((cs <= ss + SEL_BLOCK - 1) & (cs + CMP_BLOCK - 1 >= ss)).astype(BF16)
    per_bg = lambda n, w: pl.BlockSpec((None, n, w), lambda i, j: (i, 0, 0))
    return pl.pallas_call(
        functools.partial(_attn_kernel, seq=seq, tq=tq, tk=tk),
        grid=(bg, nq),
        in_specs=[pl.BlockSpec((None, HEADS_PER_GROUP, tq, LANES), lambda i, j: (i, 0, j, 0)),
                  per_bg(seq, 2 * LANES), per_bg(seq, LANES), per_bg(seq, LANES), per_bg(seq, LANES),
                  per_bg(nc, LANES), per_bg(nc, LANES),
                  pl.BlockSpec((None, tq, 3 * HEADS_PER_GROUP), lambda i, j: (i, j, 0)),
                  pl.BlockSpec((nb, nc), lambda i, j: (0, 0))],
        out_specs=pl.BlockSpec((tq, gq), lambda i, j: ((i // N_KV_GROUPS) * nq + j, i % N_KV_GROUPS)),
        out_shape=jax.ShapeDtypeStruct((batch * seq, ATTN_WIDTH), F32),
        scratch_shapes=[pltpu.VMEM((rows, LANES), F32), pltpu.VMEM((rows, LANES), F32), pltpu.VMEM((rows, LANES), F32),
                        pltpu.VMEM((rows, HEAD_DIM), F32), pltpu.VMEM((rows, HEAD_DIM), F32),
                        pltpu.VMEM((rows, 2 * LANES), BF16), pltpu.SMEM((seq // tk,), jnp.int32)],
        compiler_params=_params("parallel", "arbitrary"),
        name="nsa_attention",
    )(qa, ksa, vsa, kwa, vwa, kca, vca, gt, ovt)


def _alibi_columns(seq):
    dh = HEAD_DIM
    nb, nc = seq // SEL_BLOCK, seq // CMP_STRIDE
    head = jnp.arange(1, N_HEADS + 1, dtype=F32).reshape(N_KV_GROUPS, HEADS_PER_GROUP)
    parts = _split3(jnp.exp2(-8.0 * head / N_HEADS) * LOG2E)
    qcols = jnp.stack([float(SEL_BLOCK) * p.astype(F32) for p in parts] + [p.astype(F32) for p in parts], axis=-1)
    qcols = jnp.pad(qcols, ((0, 0), (0, 0), (0, LANES - dh - qcols.shape[-1]))).astype(BF16)

    def key_cols(u, scale):
        cols = jnp.stack([u // SEL_BLOCK] * 3 + [u % SEL_BLOCK] * 3, axis=-1).astype(F32) * scale
        return jnp.pad(cols, ((0, 0), (0, LANES - dh - cols.shape[-1])))

    pos = jnp.arange(seq)
    onehot = (pos[:, None] // SEL_BLOCK == jnp.arange(LANES)[None, :]) & (jnp.arange(LANES) < nb)
    win_cols = key_cols(pos, 1.0).astype(BF16)
    sel_cols = jnp.concatenate([win_cols, onehot.astype(BF16)], axis=1)
    cmp_cols = key_cols(jnp.arange(nc), float(CMP_STRIDE)).astype(BF16)
    ones_cols = (jnp.arange(LANES - dh) == 0).astype(BF16)
    return qcols, sel_cols, win_cols, cmp_cols, ones_cols


def _nsa_layer(h, batch, seq, norm, w_in, k_pe, k_w1, k_w2, v_pe, v_w1, v_w2, w_out):
    gg = N_KV_GROUPS
    bg = batch * gg
    (qa, ksa, vsa, kwa, vwa, kc_in, vc_in, gt, z), cmp_cols, ones_cols = _nsa_in(h, norm, w_in, batch, seq)
    merge = lambda x: x.reshape((bg,) + x.shape[2:])
    with_cols = lambda x, cols: jnp.concatenate([x, jnp.broadcast_to(cols, x.shape[:-1] + cols.shape[-1:])], axis=-1)
    kc, vc = _compress(merge(kc_in), merge(vc_in), k_pe, k_w1, k_w2, v_pe, v_w1, v_w2)
    o = _attention(merge(qa), merge(ksa), merge(vsa), merge(kwa), merge(vwa),
                   with_cols(kc, cmp_cols), with_cols(vc, ones_cols), merge(gt), batch, seq)
    return _nsa_out(o, z, w_out, h)


def _dot_bf16(a, b):
    return jnp.dot(a.astype(BF16), b.astype(BF16), preferred_element_type=F32)


def _member(shape, axis, width):
    return (lax.broadcasted_iota(jnp.int32, shape, axis) // width) % 2


def _s5_state_kernel(u_ref, w_ref, xr_ref, xi_ref):
    w = w_ref[...]
    p = SSM_STATE
    wide = jnp.concatenate([w[:, :p], w[:, :p], w[:, p:], w[:, p:]], axis=1)
    w_pair = jnp.where(_member(wide.shape, 0, SSM_GROUP) == _member(wide.shape, 1, p), wide, 0.0)
    x = _dot_bf16(u_ref[...], w_pair)
    half = x.shape[1] // 2
    xr_ref[...] = x[:, :half]
    xi_ref[...] = x[:, half:]


def _s5_scan_kernel(xr_ref, xi_ref, ar_ref, ai_ref, cr_ref, ci_ref, *, batch, nchunk):
    ar, ai = ar_ref[...], ai_ref[...]
    width = ar.shape[1]

    def step(n, carry):
        new = []
        for b in range(batch):
            cr, ci = carry[2 * b], carry[2 * b + 1]
            idx = b * nchunk + n
            cr_ref[pl.ds(idx, 1), :] = cr
            ci_ref[pl.ds(idx, 1), :] = ci
            lr, li = xr_ref[pl.ds(idx, 1), :], xi_ref[pl.ds(idx, 1), :]
            new += [ar * cr - ai * ci + lr, ar * ci + ai * cr + li]
        return tuple(new)

    zero = jnp.zeros((1, width), F32)
    lax.fori_loop(0, nchunk, step, (zero,) * (2 * batch))


def _s5_y_kernel(u_ref, cr_ref, ci_ref, k_ref, v_ref, d_ref, y_ref, m_scr):
    k = k_ref[...]
    col = lax.broadcasted_iota(jnp.int32, k.shape, 1)
    m_scr[0:S5_PAIR_LANES, :] = k
    for s in range(1, S5_CHUNK):
        shifted = pltpu.roll(k, s * S5_PAIR_LANES, 1)
        m_scr[s * S5_PAIR_LANES:(s + 1) * S5_PAIR_LANES, :] = jnp.where(col >= s * S5_PAIR_LANES, shifted, 0.0)
    v = v_ref[...]
    p = SSM_STATE
    tall = jnp.concatenate([v[:p], v[:p], v[p:], v[p:]], axis=0)
    v_pair = jnp.where(_member(tall.shape, 0, p) == _member(tall.shape, 1, SSM_GROUP), tall, 0.0)
    u = u_ref[...]
    carry = jnp.concatenate([cr_ref[...], ci_ref[...]], axis=1)
    y = _dot_bf16(u, m_scr[...]) + _dot_bf16(carry, v_pair) + d_ref[...] * u
    y_ref[...] = jax.nn.gelu(y)


def _s5_matrices(log_dt, lambda_re, lambda_im, b_re, b_im, c_re, c_im, d_skip):
    hp = lax.Precision.HIGHEST
    gn, pn, cn, ln = SSM_GROUPS, SSM_STATE, SSM_GROUP, S5_CHUNK
    dt = jnp.exp(log_dt.astype(F32))[:, None]
    lre = jnp.minimum(lambda_re.astype(F32), -1e-4)
    lim = lambda_im.astype(F32)
    mag = jnp.exp(lre * dt)
    ab_re, ab_im = mag * jnp.cos(lim * dt), mag * jnp.sin(lim * dt)
    den = lre * lre + lim * lim
    nr = ab_re - 1.0
    coef_re = (nr * lre + ab_im * lim) / den
    coef_im = (ab_im * lre - nr * lim) / den
    br32, bi32 = b_re.astype(F32), b_im.astype(F32)
    bb_re = coef_re[..., None] * br32 - coef_im[..., None] * bi32
    bb_im = coef_re[..., None] * bi32 + coef_im[..., None] * br32
    cr32, ci32 = c_re.astype(F32), c_im.astype(F32)
    k_steps = jnp.arange(ln + 1, dtype=F32)[:, None, None]
    pw_mag = jnp.exp(k_steps * (lre * dt))
    pw_re, pw_im = pw_mag * jnp.cos(k_steps * (lim * dt)), pw_mag * jnp.sin(k_steps * (lim * dt))
    ab_b_re = pw_re[:ln, :, :, None] * bb_re - pw_im[:ln, :, :, None] * bb_im
    ab_b_im = pw_re[:ln, :, :, None] * bb_im + pw_im[:ln, :, :, None] * bb_re
    kern = (jnp.einsum('gop,kgpi->kgoi', cr32, ab_b_re, precision=hp)
            - jnp.einsum('gop,kgpi->kgoi', ci32, ab_b_im, precision=hp))
    gp = gn // 2
    same = jnp.eye(2, dtype=F32)
    k_pair = kern.reshape(ln, gp, 2, cn, cn).transpose(1, 2, 4, 0, 3)
    k_pair = k_pair[:, :, :, :, None, :] * same[None, :, None, None, :, None]
    k_pair = k_pair.reshape(gp, S5_PAIR_LANES, S5_PAIR)
    w = jnp.stack([ab_b_re[::-1], ab_b_im[::-1]]).reshape(2, ln, gp, 2, pn, cn)
    w_pair = w.transpose(2, 1, 3, 5, 0, 4).reshape(gp, S5_PAIR, 2 * pn)
    ar1, ai1 = pw_re[1:], pw_im[1:]
    v_re = cr32[None] * ar1[:, :, None, :] - ci32[None] * ai1[:, :, None, :]
    v_im = -(cr32[None] * ai1[:, :, None, :] + ci32[None] * ar1[:, :, None, :])
    v = jnp.stack([v_re, v_im]).reshape(2, ln, gp, 2, cn, pn)
    v_pair = v.transpose(2, 0, 5, 1, 3, 4).reshape(gp, 2 * pn, S5_PAIR)
    d_pair = jnp.tile(d_skip.astype(F32).reshape(gp, 1, 2 * cn), (1, 1, ln))
    a_end_re = pw_re[ln].reshape(1, gn * pn)
    a_end_im = pw_im[ln].reshape(1, gn * pn)
    return k_pair, w_pair, v_pair, d_pair, a_end_re, a_end_im


def _s5_layer(h, batch, seq, norm, w_in, log_dt, lambda_re, lambda_im, b_re, b_im, c_re, c_im, d_skip,
              w_glu, w_out, final_norm, final):
    gp, ln, cn = SSM_GROUPS // 2, S5_CHUNK, SSM_GROUP
    nchunk = seq // ln
    rows = batch * nchunk
    state_w = 2 * SSM_STATE
    k_pair, w_pair, v_pair, d_pair, a_re, a_im = _s5_matrices(
        log_dt, lambda_re, lambda_im, b_re, b_im, c_re, c_im, d_skip)
    up, z = _s5_in(h, norm, w_in)
    pair3 = lambda a, b: pl.BlockSpec((None, a, b), lambda i: (i, 0, 0))
    slab = pl.BlockSpec((rows, state_w), lambda i: (0, i))
    xr, xi = pl.pallas_call(
        _s5_state_kernel,
        grid=(gp,),
        in_specs=[pair3(rows, S5_PAIR), pair3(S5_PAIR, state_w)],
        out_specs=[slab, slab],
        out_shape=[jax.ShapeDtypeStruct((rows, gp * state_w), F32)] * 2,
        compiler_params=_params("parallel"),
        name="s5_chunk_state",
    )(up, w_pair)
    scan_w = 4 * state_w
    wide = pl.BlockSpec((rows, scan_w), lambda i: (0, i))
    coef = pl.BlockSpec((1, scan_w), lambda i: (0, i))
    cr, ci = pl.pallas_call(
        functools.partial(_s5_scan_kernel, batch=batch, nchunk=nchunk),
        grid=(gp * state_w // scan_w,),
        in_specs=[wide, wide, coef, coef],
        out_specs=[wide, wide],
        out_shape=[jax.ShapeDtypeStruct((rows, gp * state_w), F32)] * 2,
        compiler_params=_params("parallel"),
        name="s5_carry_scan",
    )(xr, xi, a_re, a_im)
    yp = pl.pallas_call(
        _s5_y_kernel,
        grid=(gp,),
        in_specs=[pair3(rows, S5_PAIR), slab, slab, pair3(S5_PAIR_LANES, S5_PAIR), pair3(state_w, S5_PAIR),
                  pair3(1, S5_PAIR)],
        out_specs=pair3(rows, S5_PAIR),
        out_shape=jax.ShapeDtypeStruct((gp, rows, S5_PAIR), F32),
        scratch_shapes=[pltpu.VMEM((S5_PAIR, S5_PAIR), F32)],
        compiler_params=_params("parallel"),
        name="s5_chunk_output",
    )(up, cr, ci, k_pair, v_pair, d_pair)
    return _s5_out(yp, z, w_glu, w_out, h, final_norm, final)


def kernel(x, l0_norm, l0_w_in, l0_cmp_k_pe, l0_cmp_k_w1, l0_cmp_k_w2, l0_cmp_v_pe, l0_cmp_v_w1, l0_cmp_v_w2, l0_w_out, l1_norm, l1_w_in, l1_log_dt, l1_lambda_re, l1_lambda_im, l1_b_re, l1_b_im, l1_c_re, l1_c_im, l1_d, l1_w_glu, l1_w_out, l2_norm, l2_w_in, l2_cmp_k_pe, l2_cmp_k_w1, l2_cmp_k_w2, l2_cmp_v_pe, l2_cmp_v_w1, l2_cmp_v_w2, l2_w_out, l3_norm, l3_w_in, l3_log_dt, l3_lambda_re, l3_lambda_im, l3_b_re, l3_b_im, l3_c_re, l3_c_im, l3_d, l3_w_glu, l3_w_out, final_norm):
    batch, seq, _ = x.shape
    h = x.reshape(batch * seq, D_MODEL)
    h = _nsa_layer(h, batch, seq, l0_norm, l0_w_in, l0_cmp_k_pe, l0_cmp_k_w1, l0_cmp_k_w2,
                   l0_cmp_v_pe, l0_cmp_v_w1, l0_cmp_v_w2, l0_w_out)
    h = _s5_layer(h, batch, seq, l1_norm, l1_w_in, l1_log_dt, l1_lambda_re, l1_lambda_im, l1_b_re, l1_b_im,
                  l1_c_re, l1_c_im, l1_d, l1_w_glu, l1_w_out, final_norm, False)
    h = _nsa_layer(h, batch, seq, l2_norm, l2_w_in, l2_cmp_k_pe, l2_cmp_k_w1, l2_cmp_k_w2,
                   l2_cmp_v_pe, l2_cmp_v_w1, l2_cmp_v_w2, l2_w_out)
    h = _s5_layer(h, batch, seq, l3_norm, l3_w_in, l3_log_dt, l3_lambda_re, l3_lambda_im, l3_b_re, l3_b_im,
                  l3_c_re, l3_c_im, l3_d, l3_w_glu, l3_w_out, final_norm, True)
    return h.reshape(batch, seq, D_MODEL)
```

```python
import functools

import jax
import jax.numpy as jnp
from jax import lax
from jax.experimental import pallas as pl
from jax.experimental.pallas import tpu as pltpu

F32 = jnp.float32
BF16 = jnp.bfloat16

D_MODEL = 1024
EPS = 1e-6
NEG_INF = -1e30
FORCE_SCORE = 1e9

N_HEADS = 16
HEAD_DIM = 64
N_KV_GROUPS = 4
HEADS_PER_GROUP = N_HEADS // N_KV_GROUPS
ATTN_WIDTH = N_HEADS * HEAD_DIM
KV_WIDTH = N_KV_GROUPS * HEAD_DIM
CMP_BLOCK = 32
CMP_STRIDE = 16
CMP_HIDDEN = 256
SEL_BLOCK = 64
SEL_TOPK = 16
WINDOW = 512
N_GATES = 3 * N_HEADS
GATES_PAD = 128

SSM_WIDTH = D_MODEL
SSM_GROUP = 16
SSM_GROUPS = SSM_WIDTH // SSM_GROUP
SSM_STATE = 64
S5_CHUNK = 16
S5_PAIRS = SSM_GROUPS // 2
S5_PAIR_LANES = 2 * SSM_GROUP
S5_PAIR = S5_CHUNK * S5_PAIR_LANES

ROW_TILE = 512
ATTN_TQ = 256
ATTN_TK = 512
LANES = 128
MASK_BIAS = -2.0 ** 100
LOG2E = 1.4426950408889634
EXP_HEADROOM = 80.0
VMEM_LIMIT = 56 * 1024 * 1024


def _params(*sem):
    return pltpu.CompilerParams(dimension_semantics=sem, vmem_limit_bytes=VMEM_LIMIT)


def _rms(x, g):
    return x * lax.rsqrt(jnp.mean(x * x, axis=-1, keepdims=True) + EPS) * g


def _silu(x):
    return x * jax.nn.sigmoid(x)


def _nsa_in_kernel(h_ref, g_ref, w_ref, qcols_ref, selc_ref, winc_ref, ones_ref,
                   qa_ref, ksa_ref, vsa_ref, kwa_ref, vwa_ref, kcin_ref, vcin_ref, gt_ref, z_ref, c_scr):
    gg, rr, dh = N_KV_GROUPS, HEADS_PER_GROUP, HEAD_DIM
    xn = _rms(h_ref[...], g_ref[...])
    y = jnp.dot(xn.astype(BF16), w_ref[...], preferred_element_type=F32)
    for g in range(gg):
        for r in range(rr):
            k = g * rr + r
            qa_ref[g, r] = jnp.broadcast_to(qcols_ref[g, r:r + 1, :], (ROW_TILE, LANES))
            qa_ref[g, r, :, :dh] = (y[:, k * dh:(k + 1) * dh] * (dh ** -0.5 * LOG2E)).astype(BF16)
    o = ATTN_WIDTH + 2 * KV_WIDTH
    consts = (selc_ref[...], jnp.broadcast_to(ones_ref[...], (ROW_TILE, LANES)), winc_ref[...],
              jnp.broadcast_to(ones_ref[...], (ROW_TILE, LANES)))
    for kind, ref in enumerate((ksa_ref, vsa_ref, kwa_ref, vwa_ref)):
        for g in range(gg):
            c = o + (kind * gg + g) * dh
            ref[g] = consts[kind]
            ref[g, :, :dh] = y[:, c:c + dh].astype(BF16)
    lane = lax.broadcasted_iota(jnp.int32, (ROW_TILE // CMP_STRIDE, LANES), 1)
    for n in range(2 * KV_WIDTH // LANES):
        c_scr[n] = y[:, ATTN_WIDTH + n * LANES:ATTN_WIDTH + (n + 1) * LANES]
        kind, g0 = divmod(2 * n, gg)
        for a in range(CMP_STRIDE // 2):
            even = c_scr[n, pl.ds(2 * a, ROW_TILE // CMP_STRIDE, stride=CMP_STRIDE), :]
            odd = c_scr[n, pl.ds(2 * a + 1, ROW_TILE // CMP_STRIDE, stride=CMP_STRIDE), :]
            dst = (kcin_ref, vcin_ref)[kind]
            dst[g0, :, a * LANES:(a + 1) * LANES] = jnp.where(lane < dh, even, pltpu.roll(odd, dh, 1))
            dst[g0 + 1, :, a * LANES:(a + 1) * LANES] = jnp.where(lane < dh, pltpu.roll(even, dh, 1), odd)
    o += 4 * KV_WIDTH
    z_ref[...] = y[:, o:o + ATTN_WIDTH]
    o += ATTN_WIDTH
    per_group = 3 * rr
    for g in range(gg):
        gt_ref[g] = y[:, o + g * per_group:o + (g + 1) * per_group]


def _nsa_in(h, norm, w_in, batch, seq):
    gg, rr, dh = N_KV_GROUPS, HEADS_PER_GROUP, HEAD_DIM
    nt = seq // ROW_TILE
    nc_tile = ROW_TILE // CMP_STRIDE
    a, b = ATTN_WIDTH + 6 * KV_WIDTH, ATTN_WIDTH + 6 * KV_WIDTH + N_GATES
    w_gates = w_in[:, a:b].reshape(D_MODEL, 3, gg, rr).transpose(0, 2, 1, 3).reshape(D_MODEL, N_GATES)
    w = jnp.concatenate([w_in[:, :a], w_in[:, b:], w_gates,
                         jnp.zeros((D_MODEL, GATES_PAD - N_GATES), w_in.dtype)], axis=1).astype(BF16)
    n = w.shape[1]
    qcols, sel_cols, win_cols, cmp_cols, ones_cols = _alibi_columns(seq)
    lead = lambda cols: jnp.pad(cols, [(0, 0)] * (cols.ndim - 1) + [(dh, 0)])
    const = lambda shape: pl.BlockSpec(shape, lambda i: (0,) * len(shape))
    by_pos = lambda width: pl.BlockSpec((ROW_TILE, width), lambda i: (i % nt, 0))
    per_b = lambda *tail: pl.BlockSpec((None, gg) + tail, lambda i: (i // nt, 0) + (0,) * (len(tail) - 2) + (i % nt, 0))
    outs = pl.pallas_call(
        _nsa_in_kernel,
        grid=(batch * nt,),
        in_specs=[pl.BlockSpec((ROW_TILE, D_MODEL), lambda i: (i, 0)), const((1, D_MODEL)), const((D_MODEL, n)),
                  const((gg, rr, LANES)), by_pos(2 * LANES), by_pos(LANES), const((1, LANES))],
        out_specs=[per_b(rr, ROW_TILE, LANES), per_b(ROW_TILE, 2 * LANES), per_b(ROW_TILE, LANES),
                   per_b(ROW_TILE, LANES), per_b(ROW_TILE, LANES),
                   per_b(nc_tile, CMP_STRIDE * dh), per_b(nc_tile, CMP_STRIDE * dh),
                   per_b(ROW_TILE, 3 * rr), pl.BlockSpec((ROW_TILE, ATTN_WIDTH), lambda i: (i, 0))],
        out_shape=[jax.ShapeDtypeStruct((batch, gg, rr, seq, LANES), BF16),
                   jax.ShapeDtypeStruct((batch, gg, seq, 2 * LANES), BF16),
                   jax.ShapeDtypeStruct((batch, gg, seq, LANES), BF16),
                   jax.ShapeDtypeStruct((batch, gg, seq, LANES), BF16),
                   jax.ShapeDtypeStruct((batch, gg, seq, LANES), BF16),
                   jax.ShapeDtypeStruct((batch, gg, seq // CMP_STRIDE, CMP_STRIDE * dh), F32),
                   jax.ShapeDtypeStruct((batch, gg, seq // CMP_STRIDE, CMP_STRIDE * dh), F32),
                   jax.ShapeDtypeStruct((batch, gg, seq, 3 * rr), F32),
                   jax.ShapeDtypeStruct((batch * seq, ATTN_WIDTH), F32)],
        scratch_shapes=[pltpu.VMEM((2 * KV_WIDTH // LANES, ROW_TILE, LANES), F32)],
        compiler_params=_params("parallel"),
        name="nsa_in_proj",
    )(h, norm.reshape(1, D_MODEL), w, lead(qcols), lead(sel_cols), lead(win_cols), lead(ones_cols)[None])
    return outs, cmp_cols, ones_cols


def _quarter_exchange(parts):
    quarter = lax.broadcasted_iota(jnp.int32, parts[0].shape, 1) // S5_PAIR_LANES
    outs = []
    for b in range(4):
        acc = None
        for a in range(4):
            moved = parts[a] if a == b else pltpu.roll(parts[a], ((a - b) % 4) * S5_PAIR_LANES, 1)
            acc = moved if acc is None else jnp.where(quarter == a, moved, acc)
        outs.append(acc)
    return outs


def _s5_in_kernel(h_ref, g_ref, w_ref, u_ref, z_ref, u_scr):
    xn = _rms(h_ref[...], g_ref[...])
    y = jnp.dot(xn.astype(BF16), w_ref[...], preferred_element_type=F32)
    z_ref[...] = y[:, SSM_WIDTH:]
    nch = ROW_TILE // S5_CHUNK
    for m in range(SSM_WIDTH // 128):
        u_scr[m] = y[:, m * 128:(m + 1) * 128]
        for k in range(S5_CHUNK // 4):
            steps = [u_scr[m, pl.ds(4 * k + a, nch, stride=S5_CHUNK), :] for a in range(4)]
            for b, tile in enumerate(_quarter_exchange(steps)):
                u_ref[4 * m + b, :, k * 128:(k + 1) * 128] = tile


def _s5_in(h, norm, w_in):
    t = h.shape[0]
    nch = ROW_TILE // S5_CHUNK
    row = lambda width: pl.BlockSpec((ROW_TILE, width), lambda i: (i, 0))
    return pl.pallas_call(
        _s5_in_kernel,
        grid=(t // ROW_TILE,),
        in_specs=[row(D_MODEL), pl.BlockSpec((1, D_MODEL), lambda i: (0, 0)),
                  pl.BlockSpec((D_MODEL, 2 * SSM_WIDTH), lambda i: (0, 0))],
        out_specs=[pl.BlockSpec((S5_PAIRS, nch, S5_PAIR), lambda i: (0, i, 0)), row(SSM_WIDTH)],
        out_shape=[jax.ShapeDtypeStruct((S5_PAIRS, t // S5_CHUNK, S5_PAIR), F32),
                   jax.ShapeDtypeStruct((t, SSM_WIDTH), F32)],
        scratch_shapes=[pltpu.VMEM((SSM_WIDTH // 128, ROW_TILE, 128), F32)],
        compiler_params=_params("parallel"),
        name="s5_in_proj",
    )(h, norm.reshape(1, D_MODEL), w_in.astype(BF16))


def _nsa_out_kernel(o_ref, w_ref, res_ref, out_ref):
    out_ref[...] = res_ref[...] + jnp.dot(o_ref[...], w_ref[...], preferred_element_type=F32)


def _nsa_out(o, w_out, res):
    t = o.shape[0]
    row = pl.BlockSpec((ROW_TILE, D_MODEL), lambda i: (i, 0))
    return pl.pallas_call(
        _nsa_out_kernel,
        grid=(t // ROW_TILE,),
        in_specs=[row, pl.BlockSpec((ATTN_WIDTH, D_MODEL), lambda i: (0, 0)), row],
        out_specs=row,
        out_shape=jax.ShapeDtypeStruct((t, D_MODEL), F32),
        compiler_params=_params("parallel"),
        name="nsa_out_proj",
    )(o, w_out.astype(BF16), res)


def _s5_out_kernel(y_ref, z_ref, wg_ref, wo_ref, res_ref, fn_ref, out_ref, y_scr, *, final):
    nch = ROW_TILE // S5_CHUNK
    for m in range(SSM_WIDTH // 128):
        for k in range(S5_CHUNK // 4):
            pairs = [y_ref[4 * m + b, :, k * 128:(k + 1) * 128] for b in range(4)]
            for a, tile in enumerate(_quarter_exchange(pairs)):
                y_scr[m, pl.ds(4 * k + a, nch, stride=S5_CHUNK), :] = tile
    y = jnp.concatenate([y_scr[m] for m in range(SSM_WIDTH // 128)], axis=1)
    gl = jnp.dot(y.astype(BF16), wg_ref[...], preferred_element_type=F32)
    v = gl[:, :SSM_WIDTH] * jax.nn.sigmoid(gl[:, SSM_WIDTH:])
    v = v * _silu(z_ref[...])
    h = res_ref[...] + jnp.dot(v.astype(BF16), wo_ref[...], preferred_element_type=F32)
    out_ref[...] = _rms(h, fn_ref[...]) if final else h


def _s5_out(y, z, w_glu, w_out, res, final_norm, final):
    t = z.shape[0]
    row = pl.BlockSpec((ROW_TILE, D_MODEL), lambda i: (i, 0))
    return pl.pallas_call(
        functools.partial(_s5_out_kernel, final=final),
        grid=(t // ROW_TILE,),
        in_specs=[pl.BlockSpec((S5_PAIRS, ROW_TILE // S5_CHUNK, S5_PAIR), lambda i: (0, i, 0)), row,
                  pl.BlockSpec((SSM_WIDTH, 2 * SSM_WIDTH), lambda i: (0, 0)),
                  pl.BlockSpec((SSM_WIDTH, D_MODEL), lambda i: (0, 0)), row,
                  pl.BlockSpec((1, D_MODEL), lambda i: (0, 0))],
        out_specs=row,
        out_shape=jax.ShapeDtypeStruct((t, D_MODEL), F32),
        scratch_shapes=[pltpu.VMEM((SSM_WIDTH // 128, ROW_TILE, 128), F32)],
        compiler_params=_params("parallel"),
        name="s5_glu_out_proj",
    )(y, z, w_glu.astype(BF16), w_out.astype(BF16), res, final_norm.reshape(1, D_MODEL))


def _cmp_one(a_ref, pe_ref, w1_ref, w2_ref, out_ref):
    nc = a_ref.shape[0]
    a = a_ref[...]
    pe = pe_ref[...]
    h_top = jnp.dot((a + pe[0:1]).astype(BF16), w1_ref[0], preferred_element_type=F32)
    h_bot = jnp.dot((a + pe[1:2]).astype(BF16), w1_ref[1], preferred_element_type=F32)
    hid = _silu(h_top + pltpu.roll(h_bot, nc - 1, 0))
    out = jnp.dot(hid.astype(BF16), w2_ref[...], preferred_element_type=F32)
    keep = lax.broadcasted_iota(jnp.int32, out.shape, 0) < nc - 1
    out_ref[...] = jnp.where(keep, out, 0.0).astype(BF16)


def _cmp_kernel(ka_ref, va_ref, kpe_ref, kw1_ref, kw2_ref, vpe_ref, vw1_ref, vw2_ref, kc_ref, vc_ref):
    _cmp_one(ka_ref, kpe_ref, kw1_ref, kw2_ref, kc_ref)
    _cmp_one(va_ref, vpe_ref, vw1_ref, vw2_ref, vc_ref)


def _compress(ka, va, k_pe, k_w1, k_w2, v_pe, v_w1, v_w2):
    bg, nc, half = ka.shape
    blk = pl.BlockSpec((None, nc, half), lambda i: (i, 0, 0))
    full = lambda shape: pl.BlockSpec(shape, lambda i: (0,) * len(shape))
    prep = lambda pe, w1, w2: (pe.reshape(2, half), w1.reshape(2, half, CMP_HIDDEN).astype(BF16), w2.astype(BF16))
    out = pl.BlockSpec((None, nc, HEAD_DIM), lambda i: (i, 0, 0))
    wspecs = [full((2, half)), full((2, half, CMP_HIDDEN)), full((CMP_HIDDEN, HEAD_DIM))]
    return pl.pallas_call(
        _cmp_kernel,
        grid=(bg,),
        in_specs=[blk, blk] + wspecs + wspecs,
        out_specs=[out, out],
        out_shape=[jax.ShapeDtypeStruct((bg, nc, HEAD_DIM), BF16)] * 2,
        compiler_params=_params("parallel"),
        name="nsa_compress",
    )(ka, va, *prep(k_pe, k_w1, k_w2), *prep(v_pe, v_w1, v_w2))


def _dot_nt(a, b):
    return lax.dot_general(a, b, (((1,), (1,)), ((), ())), preferred_element_type=F32)


def _split3(x):
    hi = x.astype(BF16)
    r1 = x - hi.astype(F32)
    mid = r1.astype(BF16)
    lo = (r1 - mid.astype(F32)).astype(BF16)
    return hi, mid, lo


def _softmax_values(parts, m_row, va):
    p = [jnp.exp2(sc - m_row) for sc in parts]
    acc = jnp.dot(jnp.concatenate(p, axis=1).astype(BF16), va, preferred_element_type=F32)
    return acc, p


def _row_max(parts):
    m = parts[0]
    for sc in parts[1:]:
        m = jnp.maximum(m, sc)
    return jnp.broadcast_to(jnp.max(m, axis=-1, keepdims=True), m.shape)


def _attn_kernel(qa_ref, ksa_ref, vsa_ref, kwa_ref, vwa_ref, kca_ref, vca_ref, gt_ref, z_ref, ovt_ref,
                 o_ref, out_scr, m_scr, top_scr, acc_scr, oc_scr, ow_scr, qaug_scr, used_smem, *, seq, tq, tk):
    nb = seq // SEL_BLOCK
    nc = seq // CMP_STRIDE
    rows = HEADS_PER_GROUP * tq
    q0 = pl.program_id(1) * tq

    qa = qa_ref[...].reshape(rows, LANES)
    t_q = q0 + lax.broadcasted_iota(jnp.int32, (tq, LANES), 0)
    lane = lax.broadcasted_iota(jnp.int32, (tq, LANES), 1)

    def per_head_where(ok, x, fill):
        cols = x.shape[1]
        return jnp.where(ok[None], x.reshape(HEADS_PER_GROUP, tq, cols), fill).reshape(rows, cols)

    ntile = tk // LANES
    n_before = q0 // tk

    def tile_scores(kt, causal):
        kb = pl.multiple_of(kt * tk, tk)
        sc = _dot_nt(qaug_scr[...], ksa_ref[pl.ds(kb, tk), :])
        if causal:
            key = kb + lax.broadcasted_iota(jnp.int32, (tq, tk), 1)
            sc = per_head_where(jnp.concatenate([t_q] * ntile, axis=1) >= key, sc, NEG_INF)
        return kb, sc

    def lane_max(sc):
        part = sc[:, :LANES]
        for c in range(1, ntile):
            part = jnp.maximum(part, sc[:, c * LANES:(c + 1) * LANES])
        return part

    def value_tile(kt, causal, track):
        kb, sc = tile_scores(kt, causal)
        if track:
            top_scr[...] = jnp.maximum(top_scr[...], lane_max(sc))
        pr = jnp.exp2(sc - jnp.concatenate([m_scr[...]] * ntile, axis=1))
        acc_scr[...] += jnp.dot(pr.astype(BF16), vsa_ref[pl.ds(kb, tk), :], preferred_element_type=F32)

    def front(nchunk, first_tile):
        width = nchunk * LANES
        sc_all = _dot_nt(qa, kca_ref[0:width, :])
        last_ok = (t_q - (CMP_BLOCK - 1)) >> 4
        parts = [per_head_where(lane + c * LANES <= last_ok, sc_all[:, c * LANES:(c + 1) * LANES], NEG_INF)
                 for c in range(nchunk)]
        m_c = per_head_where(t_q >= CMP_BLOCK - 1, _row_max(parts), -NEG_INF)
        acc_c, p_c = _softmax_values(parts, m_c, vca_ref[0:width, :])
        inv_c = jnp.broadcast_to(1.0 / jnp.maximum(acc_c[:, HEAD_DIM:HEAD_DIM + 1], 1e-30), (rows, LANES))
        oc_scr[...] = acc_c[:, :HEAD_DIM] * inv_c[:, :HEAD_DIM]
        p = jnp.concatenate([pc * inv_c for pc in p_c], axis=1)

        wlen = WINDOW + tq
        w0 = pl.multiple_of(jnp.maximum(q0 - WINDOW, 0), tq)
        sw_all = _dot_nt(qa, kwa_ref[pl.ds(w0, wlen), :])
        parts = []
        for c in range(wlen // LANES):
            key = w0 + c * LANES + lane
            ok = key <= t_q
            if c * LANES < tq:
                ok = ok & (key > t_q - WINDOW)
            parts.append(per_head_where(ok, sw_all[:, c * LANES:(c + 1) * LANES], NEG_INF))
        acc_w, _ = _softmax_values(parts, _row_max(parts), vwa_ref[pl.ds(w0, wlen), :])
        ow_scr[...] = acc_w[:, :HEAD_DIM] / jnp.maximum(acc_w[:, HEAD_DIM:HEAD_DIM + 1], 1e-30)

        psum = p[0:tq] + p[tq:2 * tq] + p[2 * tq:3 * tq] + p[3 * tq:4 * tq]
        ovt = ovt_ref[:, 0:width]
        imp = sum(_dot_nt(ovt, part) for part in _split3(psum))
        j = lax.broadcasted_iota(jnp.int32, (nb, tq), 0)
        jt = (q0 + lax.broadcasted_iota(jnp.int32, (nb, tq), 1)) // SEL_BLOCK
        forced = (j == 0) | (j == jt) | (j == jt - 1)
        sel_t = jnp.where(forced, 1.0, 0.0)
        vals = jnp.where(forced, -3e38, jnp.where(j > jt, -FORCE_SCORE, imp))
        jf = j.astype(F32)
        for _ in range(min(SEL_TOPK, nb) - 3):
            best = jnp.max(vals, axis=0, keepdims=True)
            first = jnp.min(jnp.where(vals == best, jf, float(nb)), axis=0, keepdims=True)
            hit = jf == first
            sel_t = jnp.where(hit, 1.0, sel_t)
            vals = jnp.where(hit, -3e38, vals)

        not_chosen = ((1.0 - sel_t) * MASK_BIAS).T.astype(BF16)
        qaug_scr[:, :LANES] = qa
        for r in range(HEADS_PER_GROUP):
            qaug_scr[r * tq:(r + 1) * tq, LANES:LANES + nb] = not_chosen
        if nb < LANES:
            qaug_scr[:, LANES + nb:] = jnp.zeros((rows, LANES - nb), BF16)
        blocks_per_tile = tk // SEL_BLOCK
        for i in range(seq // tk):
            chosen_here = jnp.max(sel_t[i * blocks_per_tile:(i + 1) * blocks_per_tile, :])
            used_smem[i] = (chosen_here > 0.5).astype(jnp.int32)

        kb_diag, sc_diag = tile_scores(n_before, True)
        m_diag = jnp.broadcast_to(jnp.max(lane_max(sc_diag), axis=-1, keepdims=True), (rows, LANES))
        m_scr[...] = m_diag
        pr_diag = jnp.exp2(sc_diag - jnp.concatenate([m_diag] * ntile, axis=1))
        acc_scr[...] = jnp.dot(pr_diag.astype(BF16), vsa_ref[pl.ds(kb_diag, tk), :], preferred_element_type=F32)
        top_scr[...] = jnp.full((rows, LANES), NEG_INF, F32)
        if first_tile:
            value_tile(0, False, True)

    all_chunks = nc // LANES
    half_chunks = max(1, all_chunks // 2)
    chunks_needed = (q0 + tq - CMP_BLOCK) // (CMP_STRIDE * LANES) + 1
    pl.when((chunks_needed <= half_chunks) & (q0 < tk))(functools.partial(front, half_chunks, False))
    pl.when((chunks_needed <= half_chunks) & (q0 >= tk))(functools.partial(front, half_chunks, True))
    if all_chunks > half_chunks:
        assert half_chunks * CMP_STRIDE * LANES >= tk + tq
        pl.when(chunks_needed > half_chunks)(functools.partial(front, all_chunks, True))

    def tiles_before(track, first):
        def step(kt, carry):
            @pl.when(used_smem[kt] > 0)
            def _():
                value_tile(kt, False, track)
            return carry
        lax.fori_loop(first, n_before, step, 0)

    tiles_before(True, jnp.minimum(n_before, 1))

    gate = jax.nn.sigmoid(gt_ref[...])

    def write_output():
        acc = acc_scr[...]
        o_s = acc[:, :HEAD_DIM] / jnp.maximum(acc[:, HEAD_DIM:HEAD_DIM + 1], 1e-30)
        for r in range(HEADS_PER_GROUP):
            sl = slice(r * tq, (r + 1) * tq)
            gc = lambda x: gate[:, x * HEADS_PER_GROUP + r:x * HEADS_PER_GROUP + r + 1]
            out_scr[:, r * HEAD_DIM:(r + 1) * HEAD_DIM] = (gc(0) * oc_scr[sl, :] + gc(1) * o_s[sl]
                                                           + gc(2) * ow_scr[sl, :])
        o_ref[...] = (out_scr[...] * _silu(z_ref[...])).astype(BF16)

    overflow_risk = jnp.max(top_scr[...] - m_scr[...]) > EXP_HEADROOM
    write_output()

    @pl.when(overflow_risk)
    def _():
        exact = jnp.maximum(top_scr[...], m_scr[...])
        m_scr[...] = jnp.broadcast_to(jnp.max(exact, axis=-1, keepdims=True), (rows, LANES))
        acc_scr[...] = jnp.zeros((rows, LANES), F32)
        tiles_before(False, 0)
        value_tile(n_before, True, False)
        write_output()


def _attention(qa, ksa, vsa, kwa, vwa, kca, vca, gt, z, batch, seq):
    tq, tk = ATTN_TQ, ATTN_TK
    bg = batch * N_KV_GROUPS
    nq = seq // tq
    nb, nc = seq // SEL_BLOCK, seq // CMP_STRIDE
    assert SEL_TOPK <= nb <= LANES and seq % tk == 0 and tk % tq == 0 and tq % LANES == 0 and nc % LANES == 0
    rows = HEADS_PER_GROUP * tq
    gq = HEADS_PER_GROUP * HEAD_DIM
    cs = jnp.arange(nc)[None, :] * CMP_STRIDE
    ss = jnp.arange(nb)[:, None] * SEL_BLOCK
    ovt = ((cs <= ss + SEL_BLOCK - 1) & (cs + CMP_BLOCK - 1 >= ss)).astype(BF16)
    per_bg = lambda n, w: pl.BlockSpec((None, n, w), lambda i, j: (i, 0, 0))
    return pl.pallas_call(
        functools.partial(_attn_kernel, seq=seq, tq=tq, tk=tk),
        grid=(bg, nq),
        in_specs=[pl.BlockSpec((None, HEADS_PER_GROUP, tq, LANES), lambda i, j: (i, 0, j, 0)),
                  per_bg(seq, 2 * LANES), per_bg(seq, LANES), per_bg(seq, LANES), per_bg(seq, LANES),
                  per_bg(nc, LANES), per_bg(nc, LANES),
                  pl.BlockSpec((None, tq, 3 * HEADS_PER_GROUP), lambda i, j: (i, j, 0)),
                  pl.BlockSpec((tq, gq), lambda i, j: ((i // N_KV_GROUPS) * nq + j, i % N_KV_GROUPS)),
                  pl.BlockSpec((nb, nc), lambda i, j: (0, 0))],
        out_specs=pl.BlockSpec((tq, gq), lambda i, j: ((i // N_KV_GROUPS) * nq + j, i % N_KV_GROUPS)),
        out_shape=jax.ShapeDtypeStruct((batch * seq, ATTN_WIDTH), BF16),
        scratch_shapes=[pltpu.VMEM((tq, gq), F32), pltpu.VMEM((rows, LANES), F32), pltpu.VMEM((rows, LANES), F32), pltpu.VMEM((rows, LANES), F32),
                        pltpu.VMEM((rows, HEAD_DIM), F32), pltpu.VMEM((rows, HEAD_DIM), F32),
                        pltpu.VMEM((rows, 2 * LANES), BF16), pltpu.SMEM((seq // tk,), jnp.int32)],
        compiler_params=_params("parallel", "arbitrary"),
        name="nsa_attention",
    )(qa, ksa, vsa, kwa, vwa, kca, vca, gt, z, ovt)


def _alibi_columns(seq):
    dh = HEAD_DIM
    nb, nc = seq // SEL_BLOCK, seq // CMP_STRIDE
    head = jnp.arange(1, N_HEADS + 1, dtype=F32).reshape(N_KV_GROUPS, HEADS_PER_GROUP)
    parts = _split3(jnp.exp2(-8.0 * head / N_HEADS) * LOG2E)
    qcols = jnp.stack([float(SEL_BLOCK) * p.astype(F32) for p in parts] + [p.astype(F32) for p in parts], axis=-1)
    qcols = jnp.pad(qcols, ((0, 0), (0, 0), (0, LANES - dh - qcols.shape[-1]))).astype(BF16)

    def key_cols(u, scale):
        cols = jnp.stack([u // SEL_BLOCK] * 3 + [u % SEL_BLOCK] * 3, axis=-1).astype(F32) * scale
        return jnp.pad(cols, ((0, 0), (0, LANES - dh - cols.shape[-1])))

    pos = jnp.arange(seq)
    onehot = (pos[:, None] // SEL_BLOCK == jnp.arange(LANES)[None, :]) & (jnp.arange(LANES) < nb)
    win_cols = key_cols(pos, 1.0).astype(BF16)
    sel_cols = jnp.concatenate([win_cols, onehot.astype(BF16)], axis=1)
    cmp_cols = key_cols(jnp.arange(nc), float(CMP_STRIDE)).astype(BF16)
    ones_cols = (jnp.arange(LANES - dh) == 0).astype(BF16)
    return qcols, sel_cols, win_cols, cmp_cols, ones_cols


def _nsa_layer(h, batch, seq, norm, w_in, k_pe, k_w1, k_w2, v_pe, v_w1, v_w2, w_out):
    gg = N_KV_GROUPS
    bg = batch * gg
    (qa, ksa, vsa, kwa, vwa, kc_in, vc_in, gt, z), cmp_cols, ones_cols = _nsa_in(h, norm, w_in, batch, seq)
    merge = lambda x: x.reshape((bg,) + x.shape[2:])
    with_cols = lambda x, cols: jnp.concatenate([x, jnp.broadcast_to(cols, x.shape[:-1] + cols.shape[-1:])], axis=-1)
    kc, vc = _compress(merge(kc_in), merge(vc_in), k_pe, k_w1, k_w2, v_pe, v_w1, v_w2)
    o = _attention(merge(qa), merge(ksa), merge(vsa), merge(kwa), merge(vwa),
                   with_cols(kc, cmp_cols), with_cols(vc, ones_cols), merge(gt), z, batch, seq)
    return _nsa_out(o, w_out, h)


def _dot_bf16(a, b):
    return jnp.dot(a.astype(BF16), b.astype(BF16), preferred_element_type=F32)


def _member(shape, axis, width):
    return (lax.broadcasted_iota(jnp.int32, shape, axis) // width) % 2


def _s5_state_kernel(u_ref, w_ref, xr_ref, xi_ref):
    w = w_ref[...]
    p = SSM_STATE
    wide = jnp.concatenate([w[:, :p], w[:, :p], w[:, p:], w[:, p:]], axis=1)
    w_pair = jnp.where(_member(wide.shape, 0, SSM_GROUP) == _member(wide.shape, 1, p), wide, 0.0)
    x = _dot_bf16(u_ref[...], w_pair)
    half = x.shape[1] // 2
    xr_ref[...] = x[:, :half]
    xi_ref[...] = x[:, half:]


def _s5_scan_kernel(xr_ref, xi_ref, ar_ref, ai_ref, cr_ref, ci_ref, *, batch, nchunk):
    ar, ai = ar_ref[...], ai_ref[...]
    width = ar.shape[1]

    def step(n, carry):
        new = []
        for b in range(batch):
            cr, ci = carry[2 * b], carry[2 * b + 1]
            idx = b * nchunk + n
            cr_ref[pl.ds(idx, 1), :] = cr
            ci_ref[pl.ds(idx, 1), :] = ci
            lr, li = xr_ref[pl.ds(idx, 1), :], xi_ref[pl.ds(idx, 1), :]
            new += [ar * cr - ai * ci + lr, ar * ci + ai * cr + li]
        return tuple(new)

    zero = jnp.zeros((1, width), F32)
    lax.fori_loop(0, nchunk, step, (zero,) * (2 * batch))


def _s5_y_kernel(u_ref, cr_ref, ci_ref, k_ref, v_ref, d_ref, y_ref, m_scr):
    k = k_ref[...]
    col = lax.broadcasted_iota(jnp.int32, k.shape, 1)
    m_scr[0:S5_PAIR_LANES, :] = k
    for s in range(1, S5_CHUNK):
        shifted = pltpu.roll(k, s * S5_PAIR_LANES, 1)
        m_scr[s * S5_PAIR_LANES:(s + 1) * S5_PAIR_LANES, :] = jnp.where(col >= s * S5_PAIR_LANES, shifted, 0.0)
    v = v_ref[...]
    p = SSM_STATE
    tall = jnp.concatenate([v[:p], v[:p], v[p:], v[p:]], axis=0)
    v_pair = jnp.where(_member(tall.shape, 0, p) == _member(tall.shape, 1, SSM_GROUP), tall, 0.0)
    u = u_ref[...]
    carry = jnp.concatenate([cr_ref[...], ci_ref[...]], axis=1)
    y = _dot_bf16(u, m_scr[...]) + _dot_bf16(carry, v_pair) + d_ref[...] * u
    y_ref[...] = jax.nn.gelu(y)


def _s5_matrices(log_dt, lambda_re, lambda_im, b_re, b_im, c_re, c_im, d_skip):
    hp = lax.Precision.HIGHEST
    gn, pn, cn, ln = SSM_GROUPS, SSM_STATE, SSM_GROUP, S5_CHUNK
    dt = jnp.exp(log_dt.astype(F32))[:, None]
    lre = jnp.minimum(lambda_re.astype(F32), -1e-4)
    lim = lambda_im.astype(F32)
    mag = jnp.exp(lre * dt)
    ab_re, ab_im = mag * jnp.cos(lim * dt), mag * jnp.sin(lim * dt)
    den = lre * lre + lim * lim
    nr = ab_re - 1.0
    coef_re = (nr * lre + ab_im * lim) / den
    coef_im = (ab_im * lre - nr * lim) / den
    br32, bi32 = b_re.astype(F32), b_im.astype(F32)
    bb_re = coef_re[..., None] * br32 - coef_im[..., None] * bi32
    bb_im = coef_re[..., None] * bi32 + coef_im[..., None] * br32
    cr32, ci32 = c_re.astype(F32), c_im.astype(F32)
    k_steps = jnp.arange(ln + 1, dtype=F32)[:, None, None]
    pw_mag = jnp.exp(k_steps * (lre * dt))
    pw_re, pw_im = pw_mag * jnp.cos(k_steps * (lim * dt)), pw_mag * jnp.sin(k_steps * (lim * dt))
    ab_b_re = pw_re[:ln, :, :, None] * bb_re - pw_im[:ln, :, :, None] * bb_im
    ab_b_im = pw_re[:ln, :, :, None] * bb_im + pw_im[:ln, :, :, None] * bb_re
    kern = (jnp.einsum('gop,kgpi->kgoi', cr32, ab_b_re, precision=hp)
            - jnp.einsum('gop,kgpi->kgoi', ci32, ab_b_im, precision=hp))
    gp = gn // 2
    same = jnp.eye(2, dtype=F32)
    k_pair = kern.reshape(ln, gp, 2, cn, cn).transpose(1, 2, 4, 0, 3)
    k_pair = k_pair[:, :, :, :, None, :] * same[None, :, None, None, :, None]
    k_pair = k_pair.reshape(gp, S5_PAIR_LANES, S5_PAIR)
    w = jnp.stack([ab_b_re[::-1], ab_b_im[::-1]]).reshape(2, ln, gp, 2, pn, cn)
    w_pair = w.transpose(2, 1, 3, 5, 0, 4).reshape(gp, S5_PAIR, 2 * pn)
    ar1, ai1 = pw_re[1:], pw_im[1:]
    v_re = cr32[None] * ar1[:, :, None, :] - ci32[None] * ai1[:, :, None, :]
    v_im = -(cr32[None] * ai1[:, :, None, :] + ci32[None] * ar1[:, :, None, :])
    v = jnp.stack([v_re, v_im]).reshape(2, ln, gp, 2, cn, pn)
    v_pair = v.transpose(2, 0, 5, 1, 3, 4).reshape(gp, 2 * pn, S5_PAIR)
    d_pair = jnp.tile(d_skip.astype(F32).reshape(gp, 1, 2 * cn), (1, 1, ln))
    a_end_re = pw_re[ln].reshape(1, gn * pn)
    a_end_im = pw_im[ln].reshape(1, gn * pn)
    return k_pair, w_pair, v_pair, d_pair, a_end_re, a_end_im


def _s5_layer(h, batch, seq, norm, w_in, log_dt, lambda_re, lambda_im, b_re, b_im, c_re, c_im, d_skip,
              w_glu, w_out, final_norm, final):
    gp, ln, cn = SSM_GROUPS // 2, S5_CHUNK, SSM_GROUP
    nchunk = seq // ln
    rows = batch * nchunk
    state_w = 2 * SSM_STATE
    k_pair, w_pair, v_pair, d_pair, a_re, a_im = _s5_matrices(
        log_dt, lambda_re, lambda_im, b_re, b_im, c_re, c_im, d_skip)
    up, z = _s5_in(h, norm, w_in)
    pair3 = lambda a, b: pl.BlockSpec((None, a, b), lambda i: (i, 0, 0))
    slab = pl.BlockSpec((rows, state_w), lambda i: (0, i))
    xr, xi = pl.pallas_call(
        _s5_state_kernel,
        grid=(gp,),
        in_specs=[pair3(rows, S5_PAIR), pair3(S5_PAIR, state_w)],
        out_specs=[slab, slab],
        out_shape=[jax.ShapeDtypeStruct((rows, gp * state_w), F32)] * 2,
        compiler_params=_params("parallel"),
        name="s5_chunk_state",
    )(up, w_pair)
    scan_w = 4 * state_w
    wide = pl.BlockSpec((rows, scan_w), lambda i: (0, i))
    coef = pl.BlockSpec((1, scan_w), lambda i: (0, i))
    cr, ci = pl.pallas_call(
        functools.partial(_s5_scan_kernel, batch=batch, nchunk=nchunk),
        grid=(gp * state_w // scan_w,),
        in_specs=[wide, wide, coef, coef],
        out_specs=[wide, wide],
        out_shape=[jax.ShapeDtypeStruct((rows, gp * state_w), F32)] * 2,
        compiler_params=_params("parallel"),
        name="s5_carry_scan",
    )(xr, xi, a_re, a_im)
    yp = pl.pallas_call(
        _s5_y_kernel,
        grid=(gp,),
        in_specs=[pair3(rows, S5_PAIR), slab, slab, pair3(S5_PAIR_LANES, S5_PAIR), pair3(state_w, S5_PAIR),
                  pair3(1, S5_PAIR)],
        out_specs=pair3(rows, S5_PAIR),
        out_shape=jax.ShapeDtypeStruct((gp, rows, S5_PAIR), F32),
        scratch_shapes=[pltpu.VMEM((S5_PAIR, S5_PAIR), F32)],
        compiler_params=_params("parallel"),
        name="s5_chunk_output",
    )(up, cr, ci, k_pair, v_pair, d_pair)
    return _s5_out(yp, z, w_glu, w_out, h, final_norm, final)


def kernel(x, l0_norm, l0_w_in, l0_cmp_k_pe, l0_cmp_k_w1, l0_cmp_k_w2, l0_cmp_v_pe, l0_cmp_v_w1, l0_cmp_v_w2, l0_w_out, l1_norm, l1_w_in, l1_log_dt, l1_lambda_re, l1_lambda_im, l1_b_re, l1_b_im, l1_c_re, l1_c_im, l1_d, l1_w_glu, l1_w_out, l2_norm, l2_w_in, l2_cmp_k_pe, l2_cmp_k_w1, l2_cmp_k_w2, l2_cmp_v_pe, l2_cmp_v_w1, l2_cmp_v_w2, l2_w_out, l3_norm, l3_w_in, l3_log_dt, l3_lambda_re, l3_lambda_im, l3_b_re, l3_b_im, l3_c_re, l3_c_im, l3_d, l3_w_glu, l3_w_out, final_norm):
    batch, seq, _ = x.shape
    h = x.reshape(batch * seq, D_MODEL)
    h = _nsa_layer(h, batch, seq, l0_norm, l0_w_in, l0_cmp_k_pe, l0_cmp_k_w1, l0_cmp_k_w2,
                   l0_cmp_v_pe, l0_cmp_v_w1, l0_cmp_v_w2, l0_w_out)
    h = _s5_layer(h, batch, seq, l1_norm, l1_w_in, l1_log_dt, l1_lambda_re, l1_lambda_im, l1_b_re, l1_b_im,
                  l1_c_re, l1_c_im, l1_d, l1_w_glu, l1_w_out, final_norm, False)
    h = _nsa_layer(h, batch, seq, l2_norm, l2_w_in, l2_cmp_k_pe, l2_cmp_k_w1, l2_cmp_k_w2,
                   l2_cmp_v_pe, l2_cmp_v_w1, l2_cmp_v_w2, l2_w_out)
    h = _s5_layer(h, batch, seq, l3_norm, l3_w_in, l3_log_dt, l3_lambda_re, l3_lambda_im, l3_b_re, l3_b_im,
                  l3_c_re, l3_c_im, l3_d, l3_w_glu, l3_w_out, final_norm, True)
    return h.reshape(batch, seq, D_MODEL)
```
